```python
import math
import jax, jax.numpy as jnp
from jax import lax
import numpy as np

D_MODEL = 2048
BATCH = 4
SEQ = 4096
DEPTH = 2

CHUNK = 64
Q_BLOCK = 128
SSM_WIDTH = 1024
SSM_GROUP = 16
SSM_GROUPS = SSM_WIDTH // SSM_GROUP
SSM_STATE = 64
ATTN_WIDTH = 1024
N_HEADS = 8
HEAD_DIM = ATTN_WIDTH // N_HEADS // 2
V_DIM = 2 * HEAD_DIM
IN_WIDTH = SSM_WIDTH + 3 * ATTN_WIDTH + 2 * D_MODEL
_FF_RAW = -(-8 * D_MODEL // 3)
D_FF = -(-_FF_RAW // 256) * 256
N_BUCKETS = 32
MAX_DISTANCE = 128
RMS_EPS = 1e-6
SUBLN_EPS = 1e-5

kernel_name = "hybrid_s5_diffattn_gated_block"


def rms_norm(x, w, eps):
    xf = x.astype(jnp.float32)
    y = xf * lax.rsqrt(jnp.mean(xf * xf, axis=-1, keepdims=True) + eps)
    return (y * w.astype(jnp.float32)).astype(x.dtype)


def t5_bucket(rel):
    nb = N_BUCKETS // 2
    ret = jnp.where(rel > 0, nb, 0)
    n = jnp.abs(rel)
    max_exact = nb // 2
    nf = jnp.maximum(n, 1).astype(jnp.float32)
    large = max_exact + (jnp.log(nf / max_exact) / math.log(MAX_DISTANCE / max_exact)
                         * (nb - max_exact)).astype(jnp.int32)
    large = jnp.minimum(large, nb - 1)
    return ret + jnp.where(n < max_exact, n, large)


def _complex_linear_combine(e1, e2):
    ar1, ai1, br1, bi1 = e1
    ar2, ai2, br2, bi2 = e2
    ar = ar1 * ar2 - ai1 * ai2
    ai = ar1 * ai2 + ai1 * ar2
    br = ar2 * br1 - ai2 * bi1 + br2
    bi = ar2 * bi1 + ai2 * br1 + bi2
    return (ar, ai, br, bi)


def s5_mixer(u, lam_re, lam_im, log_step, b_re, b_im, c_re, c_im, d_skip, w_glu, b_glu):
    f32 = jnp.float32
    bsz, L, _ = u.shape
    ug = u.reshape(bsz, L, SSM_GROUPS, SSM_GROUP).astype(f32)
    lr = jnp.minimum(lam_re.astype(f32), -1e-4)
    li = lam_im.astype(f32)
    dt = jnp.exp(log_step.astype(f32))[:, None]
    mag = jnp.exp(lr * dt)
    ab_r = mag * jnp.cos(li * dt)
    ab_i = mag * jnp.sin(li * dt)
    den = lr * lr + li * li
    nr = ab_r - 1.0
    ni = ab_i
    fr = (nr * lr + ni * li) / den
    fi = (ni * lr - nr * li) / den
    br = b_re.astype(f32)
    bi = b_im.astype(f32)
    bb_r = fr[..., None] * br - fi[..., None] * bi
    bb_i = fr[..., None] * bi + fi[..., None] * br
    bu_r = jnp.einsum('blgc,gpc->blgp', ug, bb_r)
    bu_i = jnp.einsum('blgc,gpc->blgp', ug, bb_i)
    a_r = jnp.broadcast_to(ab_r, (1, L, SSM_GROUPS, SSM_STATE))
    a_i = jnp.broadcast_to(ab_i, (1, L, SSM_GROUPS, SSM_STATE))
    _, _, s_r, s_i = lax.associative_scan(_complex_linear_combine, (a_r, a_i, bu_r, bu_i), axis=1)
    y = (jnp.einsum('blgp,gcp->blgc', s_r, c_re.astype(f32))
         - jnp.einsum('blgp,gcp->blgc', s_i, c_im.astype(f32))
         + d_skip.astype(f32).reshape(SSM_GROUPS, SSM_GROUP) * ug)
    y = jax.nn.gelu(y.reshape(bsz, L, SSM_WIDTH).astype(u.dtype))
    return y * jax.nn.sigmoid(y @ w_glu + b_glu)


def diff_attention(q, k, v, q_norm_w, k_norm_w, lq1, lk1, lq2, lk2, subln_w, rel_table, lambda_init):
    f32 = jnp.float32
    bsz, L, _ = q.shape
    q = rms_norm(q.reshape(bsz, L, N_HEADS, 2, HEAD_DIM), q_norm_w, RMS_EPS) * (HEAD_DIM ** -0.5)
    k = rms_norm(k.reshape(bsz, L, N_HEADS, 2, HEAD_DIM), k_norm_w, RMS_EPS)
    v = v.reshape(bsz, L, N_HEADS, V_DIM)
    lam = (jnp.exp(jnp.sum(lq1.astype(f32) * lk1.astype(f32)))
           - jnp.exp(jnp.sum(lq2.astype(f32) * lk2.astype(f32))) + lambda_init)
    kpos = jnp.arange(L)
    n_blocks = L // Q_BLOCK
    qb = q.reshape(bsz, n_blocks, Q_BLOCK, N_HEADS, 2, HEAD_DIM).transpose(1, 0, 2, 3, 4, 5)

    def block(args):
        q_i, i = args
        qpos = i * Q_BLOCK + jnp.arange(Q_BLOCK)
        s = jnp.einsum('bqhmd,bkhmd->bhmqk', q_i, k, preferred_element_type=f32)
        bias = rel_table[t5_bucket(kpos[None, :] - qpos[:, None])]
        bias = jnp.transpose(bias, (2, 0, 1)).astype(f32)[None, :, None]
        mask = (kpos[None, :] // CHUNK) <= (qpos[:, None] // CHUNK)
        s = jnp.where(mask, s + bias, -jnp.inf)
        p = jax.nn.softmax(s, axis=-1)
        p = p[:, :, 0] - lam * p[:, :, 1]
        return jnp.einsum('bhqk,bkhe->bqhe', p.astype(v.dtype), v)

    o = lax.map(block, (qb, jnp.arange(n_blocks)))
    o = o.transpose(1, 0, 2, 3, 4).reshape(bsz, L, N_HEADS, V_DIM)
    o = rms_norm(o, subln_w, SUBLN_EPS) * (1.0 - lambda_init)
    return o.reshape(bsz, L, ATTN_WIDTH)


def setup_inputs(seed: int = 0) -> dict:
    key = jax.random.key(seed)
    ks = jax.random.split(key, 32)
    f32 = jnp.float32
    nrm = lambda k, shape, s: jax.random.normal(k, shape, f32) * s
    lam_im0 = jnp.pi * jnp.arange(SSM_STATE, dtype=f32)
    return {
        "x": jax.random.normal(ks[0], (BATCH, SEQ, D_MODEL), f32),
        "norm1_w": 1.0 + nrm(ks[1], (DEPTH, D_MODEL), 0.02),
        "w_in": nrm(ks[2], (DEPTH, D_MODEL, IN_WIDTH), D_MODEL ** -0.5),
        "lam_re": -0.5 * jnp.exp(nrm(ks[3], (DEPTH, SSM_GROUPS, SSM_STATE), 0.05)),
        "lam_im": lam_im0 + nrm(ks[4], (DEPTH, SSM_GROUPS, SSM_STATE), 0.05),
        "log_step": jax.random.uniform(ks[5], (DEPTH, SSM_GROUPS), f32, math.log(1e-3), math.log(1e-1)),
        "ssm_b_re": nrm(ks[6], (DEPTH, SSM_GROUPS, SSM_STATE, SSM_GROUP), (2 * SSM_GROUP) ** -0.5),
        "ssm_b_im": nrm(ks[7], (DEPTH, SSM_GROUPS, SSM_STATE, SSM_GROUP), (2 * SSM_GROUP) ** -0.5),
        "ssm_c_re": nrm(ks[8], (DEPTH, SSM_GROUPS, SSM_GROUP, SSM_STATE), (2 * SSM_STATE) ** -0.5),
        "ssm_c_im": nrm(ks[9], (DEPTH, SSM_GROUPS, SSM_GROUP, SSM_STATE), (2 * SSM_STATE) ** -0.5),
        "ssm_d": nrm(ks[10], (DEPTH, SSM_WIDTH), 1.0),
        "w_glu": nrm(ks[11], (DEPTH, SSM_WIDTH, SSM_WIDTH), SSM_WIDTH ** -0.5),
        "b_glu": nrm(ks[12], (DEPTH, SSM_WIDTH), 0.02),
        "q_norm_w": 1.0 + nrm(ks[13], (DEPTH, HEAD_DIM), 0.02),
        "k_norm_w": 1.0 + nrm(ks[14], (DEPTH, HEAD_DIM), 0.02),
        "lambda_q1": nrm(ks[15], (DEPTH, HEAD_DIM), 0.1),
        "lambda_k1": nrm(ks[16], (DEPTH, HEAD_DIM), 0.1),
        "lambda_q2": nrm(ks[17], (DEPTH, HEAD_DIM), 0.1),
        "lambda_k2": nrm(ks[18], (DEPTH, HEAD_DIM), 0.1),
        "subln_w": 1.0 + nrm(ks[19], (DEPTH, V_DIM), 0.02),
        "w_proj_ssm": nrm(ks[20], (DEPTH, SSM_WIDTH, D_MODEL), SSM_WIDTH ** -0.5),
        "w_proj_attn": nrm(ks[21], (DEPTH, ATTN_WIDTH, D_MODEL), ATTN_WIDTH ** -0.5),
        "w_out": nrm(ks[22], (DEPTH, D_MODEL, D_MODEL), D_MODEL ** -0.5),
        "rel_bias": nrm(ks[23], (N_BUCKETS, N_HEADS), 0.2),
        "norm2_w": 1.0 + nrm(ks[24], (DEPTH, D_MODEL), 0.02),
        "w_ffn_gate": nrm(ks[25], (DEPTH, D_MODEL, D_FF), D_MODEL ** -0.5),
        "w_ffn_up": nrm(ks[26], (DEPTH, D_MODEL, D_FF), D_MODEL ** -0.5),
        "w_ffn_down": nrm(ks[27], (DEPTH, D_FF, D_MODEL), D_FF ** -0.5),
    }


def reference(x, norm1_w, w_in, lam_re, lam_im, log_step, ssm_b_re, ssm_b_im, ssm_c_re, ssm_c_im,
              ssm_d, w_glu, b_glu, q_norm_w, k_norm_w, lambda_q1, lambda_k1, lambda_q2, lambda_k2,
              subln_w, w_proj_ssm, w_proj_attn, w_out, rel_bias, norm2_w, w_ffn_gate, w_ffn_up,
              w_ffn_down):
    o_q = SSM_WIDTH
    o_k = o_q + ATTN_WIDTH
    o_v = o_k + ATTN_WIDTH
    o_gs = o_v + ATTN_WIDTH
    o_ga = o_gs + D_MODEL
    for l in range(DEPTH):
        lambda_init = 0.8 - 0.6 * math.exp(-0.3 * l)
        h = rms_norm(x, norm1_w[l], RMS_EPS)
        z = h @ w_in[l]
        y_ssm = s5_mixer(z[..., :o_q], lam_re[l], lam_im[l], log_step[l], ssm_b_re[l], ssm_b_im[l],
                         ssm_c_re[l], ssm_c_im[l], ssm_d[l], w_glu[l], b_glu[l])
        y_attn = diff_attention(z[..., o_q:o_k], z[..., o_k:o_v], z[..., o_v:o_gs],
                                q_norm_w[l], k_norm_w[l], lambda_q1[l], lambda_k1[l],
                                lambda_q2[l], lambda_k2[l], subln_w[l], rel_bias, lambda_init)
        m = (jax.nn.sigmoid(z[..., o_gs:o_ga]) * (y_ssm @ w_proj_ssm[l])
             + jax.nn.sigmoid(z[..., o_ga:]) * (y_attn @ w_proj_attn[l]))
        x = x + m @ w_out[l]
        h = rms_norm(x, norm2_w[l], RMS_EPS)
        x = x + (jax.nn.silu(h @ w_ffn_gate[l]) * (h @ w_ffn_up[l])) @ w_ffn_down[l]
    return x
```

```python
import functools
import math

import numpy as np
import jax
import jax.numpy as jnp
from jax import lax
from jax.experimental import pallas as pl
from jax.experimental.pallas import tpu as pltpu

F32 = jnp.float32
BF16 = jnp.bfloat16

CHUNK = 64
MAX_DISTANCE = 128
RMS_EPS = 1e-6
SUBLN_EPS = 1e-5
LAM_RE_MAX = -1e-4
MASK_VALUE = -1e30

LANES = 128
V7X_MXU_DIM = 256
V7X_VMEM_BYTES = 64 * 1024 * 1024

SSM_T = 16


def _vmem_limit(block_bytes, extra_bytes):
    return int(min(2 * block_bytes + extra_bytes + (4 << 20), V7X_VMEM_BYTES - (6 << 20)))


def _params(semantics, block_bytes, extra_bytes):
    return pltpu.CompilerParams(dimension_semantics=semantics,
                                vmem_limit_bytes=_vmem_limit(block_bytes, extra_bytes))


def _rms_norm(x, w, eps):
    ms = jnp.mean(x * x, axis=-1, keepdims=True)
    return x * lax.rsqrt(ms + eps) * w


def _dot_nt(a, b, **kw):
    return lax.dot_general(a, b, (((1,), (1,)), ((), ())), preferred_element_type=F32, **kw)


def _in_proj_kernel(x_ref, nw_ref, w_ref, qw_ref, kw_ref, o_ref, h_ref, *, head_dim):
    j = pl.program_id(1)

    @pl.when(j == 0)
    def _():
        h_ref[...] = _rms_norm(x_ref[...], nw_ref[...], RMS_EPS).astype(BF16)

    z = jnp.dot(h_ref[...], w_ref[...], preferred_element_type=F32)
    tn = z.shape[1]

    def seg_norm_store(w_row, scale):
        lo = lax.broadcasted_iota(jnp.int32, (1, LANES), 1) < head_dim
        for c in range(tn // LANES):
            zc = z[:, c * LANES:(c + 1) * LANES]
            sq = zc * zc
            s_lo = jnp.sum(jnp.where(lo, sq, 0.0), axis=-1, keepdims=True)
            s_hi = jnp.sum(jnp.where(lo, 0.0, sq), axis=-1, keepdims=True)
            ms = jnp.where(lo, s_lo, s_hi) * (1.0 / head_dim)
            y = zc * lax.rsqrt(ms + RMS_EPS) * w_row
            o_ref[:, c * LANES:(c + 1) * LANES] = (y * scale).astype(o_ref.dtype)

    @pl.when(jnp.logical_or(j == 0, j == 3))
    def _():
        o_ref[...] = z.astype(o_ref.dtype)

    @pl.when(j == 1)
    def _():
        seg_norm_store(qw_ref[...], head_dim ** -0.5)

    @pl.when(j == 2)
    def _():
        seg_norm_store(kw_ref[...], 1.0)

    @pl.when(j >= 4)
    def _():
        o_ref[...] = jax.nn.sigmoid(z).astype(o_ref.dtype)


def _in_proj(x2, nw, w, qw2, kw2, *, head_dim, tm, tn):
    n, d = x2.shape
    width = w.shape[1]
    blocks = tm * d * 4 + d * 4 + d * tn * 2 + tm * tn * 2
    return pl.pallas_call(
        functools.partial(_in_proj_kernel, head_dim=head_dim),
        out_shape=jax.ShapeDtypeStruct((n, width), BF16),
        grid=(n // tm, width // tn),
        in_specs=[
            pl.BlockSpec((tm, d), lambda i, j: (i, 0)),
            pl.BlockSpec((1, d), lambda i, j: (0, 0)),
            pl.BlockSpec((d, tn), lambda i, j: (0, j)),
            pl.BlockSpec((1, LANES), lambda i, j: (0, 0)),
            pl.BlockSpec((1, LANES), lambda i, j: (0, 0)),
        ],
        out_specs=pl.BlockSpec((tm, tn), lambda i, j: (i, j)),
        scratch_shapes=[pltpu.VMEM((tm, d), BF16)],
        compiler_params=_params(("parallel", "arbitrary"), blocks, tm * d * 2 + 3 * tm * tn * 4),
        name="in_proj",
    )(x2, nw, w, qw2, kw2)


def _ssm_prep_kernel(lr_ref, li_ref, ls_ref, bcat_ref, bswap_ref, ccat_ref, cswap_ref, d_ref,
                     m_ref, e_ref, ft_ref, p_ref, q_ref, ft_scr, *, state, group, nsteps):
    lane = lax.broadcasted_iota(jnp.int32, (1, 2 * state), 1)
    sign = jnp.where(lane < state, -1.0, 1.0).astype(F32)
    lr = jnp.minimum(lr_ref[0], LAM_RE_MAX)
    li = li_ref[0]
    dt = jnp.exp(ls_ref[0])

    def apow(n):
        mag = jnp.exp(lr * dt * n)
        ang = li * dt * n
        return mag * jnp.cos(ang), mag * jnp.sin(ang)

    ab_r, ab_i = apow(1.0)
    den = lr * lr + li * li
    nr = ab_r - 1.0
    fr = (nr * lr + ab_i * li) / den
    fi = (ab_i * lr - nr * li) / den
    bcat = bcat_ref[0]
    bswap = bswap_ref[0]
    bb_cat = fr * bcat + sign * fi * bswap
    bb_swap = fr * bswap - sign * fi * bcat

    npow = SSM_T + 1
    nrow = lax.broadcasted_iota(jnp.int32, (8 * (-(-npow // 8)), 1), 0).astype(F32)
    pw_r, pw_i = apow(nrow)

    ccat = ccat_ref[0]
    cswap = cswap_ref[0]
    for t in range(npow):
        blk = pw_r[t:t + 1] * ccat + pw_i[t:t + 1] * cswap
        ft_scr[t * group:(t + 1) * group, :] = blk
    ft_ref[0] = ft_scr[group:(SSM_T + 1) * group, :].astype(ft_ref.dtype)

    for j in range(SSM_T):
        n = SSM_T - 1 - j
        blk = pw_r[n:n + 1] * bb_cat + sign * pw_i[n:n + 1] * bb_swap
        e_ref[0, j * group:(j + 1) * group, :] = blk.astype(e_ref.dtype)

    width = SSM_T * group
    r = _dot_nt(bb_cat, ft_scr[0:width, :], precision=lax.Precision.HIGHEST)
    col = lax.broadcasted_iota(jnp.int32, (group, width), 1)
    row = lax.broadcasted_iota(jnp.int32, (group, width), 0)
    dskip = d_ref[0]
    for j in range(SSM_T):
        shifted = r if j == 0 else pltpu.roll(r, j * group, axis=1)
        blk = jnp.where(col >= j * group, shifted, 0.0)
        blk = blk + jnp.where(col == row + j * group, dskip, 0.0)
        m_ref[0, j * group:(j + 1) * group, :] = blk.astype(m_ref.dtype)

    steps = lax.broadcasted_iota(jnp.int32, (8, 1), 0)
    nd = (SSM_T * jnp.left_shift(1, jnp.minimum(steps, nsteps))).astype(F32)
    pd_r, pd_i = apow(nd)
    p_ref[0] = pd_r
    q_ref[0] = sign * pd_i


def _ssm_prep(lr2, li2, ls2, bcat, bswap, ccat, cswap, d_t, *, nsteps):
    g, group, lanes2 = bcat.shape
    state = lanes2 // 2
    width = SSM_T * group
    gspec = lambda *shape: pl.BlockSpec((1,) + shape, lambda i: (i, 0, 0))
    return pl.pallas_call(
        functools.partial(_ssm_prep_kernel, state=state, group=group, nsteps=nsteps),
        out_shape=(
            jax.ShapeDtypeStruct((g, width, width), BF16),
            jax.ShapeDtypeStruct((g, width, lanes2), BF16),
            jax.ShapeDtypeStruct((g, width, lanes2), BF16),
            jax.ShapeDtypeStruct((g, 8, lanes2), F32),
            jax.ShapeDtypeStruct((g, 8, lanes2), F32),
        ),
        grid=(g,),
        in_specs=[gspec(1, lanes2), gspec(1, lanes2), gspec(1, lanes2),
                  gspec(group, lanes2), gspec(group, lanes2),
                  gspec(group, lanes2), gspec(group, lanes2), gspec(1, width)],
        out_specs=(gspec(width, width), gspec(width, lanes2), gspec(width, lanes2),
                   gspec(8, lanes2), gspec(8, lanes2)),
        scratch_shapes=[pltpu.VMEM(((SSM_T + 1) * group, lanes2), F32)],
        compiler_params=pltpu.CompilerParams(dimension_semantics=("parallel",)),
        name="ssm_prep",
    )(lr2, li2, ls2, bcat, bswap, ccat, cswap, d_t)


def _ssm_kernel(u_ref, m_ref, e_ref, ft_ref, p_ref, q_ref, o_ref, *, kchunks, nsteps, state):
    u = u_ref[0]
    rows = u.shape[0]
    s = jnp.dot(u, e_ref[0], preferred_element_type=F32)
    kidx = lax.broadcasted_iota(jnp.int32, (rows, 1), 0) & (kchunks - 1)
    for i in range(nsteps):
        d = 1 << i
        x = jnp.where(kidx >= d, pltpu.roll(s, d, axis=0), 0.0)
        s = s + p_ref[0, i:i + 1, :] * x + q_ref[0, i:i + 1, :] * pltpu.roll(x, state, axis=1)
    s_prev = jnp.where(kidx >= 1, pltpu.roll(s, 1, axis=0), 0.0)
    y = (jnp.dot(u, m_ref[0], preferred_element_type=F32)
         + _dot_nt(s_prev.astype(BF16), ft_ref[0]))
    o_ref[0] = jax.nn.gelu(y).astype(o_ref.dtype)


def _ssm(u_t, m, e, ft, p, q, *, kchunks, nsteps):
    g, rows, width = u_t.shape
    lanes2 = e.shape[2]
    gspec = lambda *shape: pl.BlockSpec((1,) + shape, lambda i: (i, 0, 0))
    blocks = 2 * rows * width * 2 + width * width * 2 + 2 * width * lanes2 * 2
    return pl.pallas_call(
        functools.partial(_ssm_kernel, kchunks=kchunks, nsteps=nsteps, state=lanes2 // 2),
        out_shape=jax.ShapeDtypeStruct((g, rows, width), BF16),
        grid=(g,),
        in_specs=[gspec(rows, width), gspec(width, width), gspec(width, lanes2),
                  gspec(width, lanes2), gspec(8, lanes2), gspec(8, lanes2)],
        out_specs=gspec(rows, width),
        compiler_params=_params(("parallel",), blocks, 6 * rows * lanes2 * 4 + 2 * rows * width * 4),
        name="ssm",
    )(u_t, m, e, ft, p, q)


def _bias_kernel(tab_ref, idx_ref, o_ref, *, n_buckets, far_bucket):
    h = pl.program_id(0)
    far = tab_ref[far_bucket, h]
    for t in range(idx_ref.shape[0]):
        idx = idx_ref[t]
        acc = jnp.full(idx.shape, MASK_VALUE, F32)
        for b in range(n_buckets):
            acc = jnp.where(idx == b, tab_ref[b, h] - far, acc)
        o_ref[0, t] = acc


def _t5_bucket_np(rel, n_buckets):
    nb = n_buckets // 2
    ret = np.where(rel > 0, nb, 0)
    n = np.abs(rel)
    max_exact = nb // 2
    nf = np.maximum(n, 1).astype(np.float32)
    large = max_exact + (np.log(nf / np.float32(max_exact)) / np.float32(math.log(MAX_DISTANCE / max_exact))
                         * np.float32(nb - max_exact)).astype(np.int32)
    large = np.minimum(large, nb - 1)
    return (ret + np.where(n < max_exact, n, large)).astype(np.int32)


def _bias_tiles(rel_bias, *, tq):
    n_buckets, n_heads = rel_bias.shape
    r = np.arange(tq)[:, None]
    c = np.arange(tq)[None, :]
    idx = np.stack([_t5_bucket_np(c - r - d * tq, n_buckets) for d in range(2)])
    idx[0] = np.where(c // CHUNK <= r // CHUNK, idx[0], -1)
    far_bucket = int(_t5_bucket_np(np.array(-MAX_DISTANCE), n_buckets))
    assert tq >= MAX_DISTANCE and (idx[1][:, 0] == far_bucket).all()
    return pl.pallas_call(
        functools.partial(_bias_kernel, n_buckets=n_buckets, far_bucket=far_bucket),
        out_shape=jax.ShapeDtypeStruct((n_heads, 2, tq, tq), F32),
        grid=(n_heads,),
        in_specs=[pl.BlockSpec(memory_space=pltpu.SMEM),
                  pl.BlockSpec((2, tq, tq), lambda h: (0, 0, 0))],
        out_specs=pl.BlockSpec((1, 2, tq, tq), lambda h: (h, 0, 0, 0)),
        compiler_params=pltpu.CompilerParams(dimension_semantics=("parallel",)),
        name="attn_bias",
    )(rel_bias, jnp.asarray(idx))


def _attn_kernel(q_ref, k_ref, v_ref, bias_ref, lq1_ref, lk1_ref, lq2_ref, lk2_ref, sw_ref,
                 o_ref, m_scr, l_scr, acc_scr, *, tq, head_dim, lambda_init):
    i = pl.program_id(2)
    q = q_ref[...]
    lo = lax.broadcasted_iota(jnp.int32, (1, LANES), 1) < head_dim
    zero = jnp.zeros_like(q)
    q_maps = (jnp.where(lo, q, zero), jnp.where(lo, zero, q))

    m_scr[...] = jnp.full(m_scr.shape, MASK_VALUE, F32)
    l_scr[...] = jnp.zeros(l_scr.shape, F32)
    acc_scr[...] = jnp.zeros(acc_scr.shape, F32)

    def step(kb, vb, bias):
        for m in range(2):
            s = _dot_nt(q_maps[m], kb)
            if bias is not None:
                s = s + bias
            m_prev = m_scr[m][:, 0:1]
            m_new = jnp.maximum(m_prev, jnp.max(s, axis=-1, keepdims=True))
            p = jnp.exp(s - m_new)
            alpha = jnp.exp(m_prev - m_new)
            l_new = alpha * l_scr[m][:, 0:1] + jnp.sum(p, axis=-1, keepdims=True)
            acc_scr[m] = alpha * acc_scr[m] + jnp.dot(p.astype(BF16), vb, preferred_element_type=F32)
            m_scr[m] = jnp.broadcast_to(m_new, (tq, LANES))
            l_scr[m] = jnp.broadcast_to(l_new, (tq, LANES))

    def kv(j):
        start = pl.multiple_of(j * tq, tq)
        return k_ref[pl.ds(start, tq), :], v_ref[pl.ds(start, tq), :]

    step(*kv(i), bias_ref[0, 0])
    step(*kv(jnp.maximum(i - 1, 0)), bias_ref[0, 1] + jnp.where(i == 0, MASK_VALUE, 0.0))

    def far_body(j, carry):
        step(*kv(j), None)
        return carry

    lax.fori_loop(0, jnp.maximum(i - 1, 0), far_body, 0)

    s1 = jnp.sum(lq1_ref[...] * lk1_ref[...], axis=-1, keepdims=True)
    s2 = jnp.sum(lq2_ref[...] * lk2_ref[...], axis=-1, keepdims=True)
    lam = jnp.exp(s1) - jnp.exp(s2) + lambda_init
    o = acc_scr[0] / l_scr[0] - lam * (acc_scr[1] / l_scr[1])
    y = _rms_norm(o, sw_ref[...], SUBLN_EPS) * (1.0 - lambda_init)
    o_ref[...] = y.astype(o_ref.dtype)


def _attention(z, bias, lq1, lk1, lq2, lk2, sw, *, batch, seq, n_heads, head_dim, tq,
               q_col, k_col, v_col, lambda_init):
    n = z.shape[0]
    nq = seq // tq
    vec = pl.BlockSpec((1, head_dim), lambda b, h, i: (0, 0))
    blocks = tq * LANES * 2 * 2 + 2 * seq * LANES * 2 + 2 * tq * tq * 4 + tq * LANES * 2
    return pl.pallas_call(
        functools.partial(_attn_kernel, tq=tq, head_dim=head_dim, lambda_init=lambda_init),
        out_shape=jax.ShapeDtypeStruct((n, n_heads * LANES), BF16),
        grid=(batch, n_heads, nq),
        in_specs=[
            pl.BlockSpec((tq, LANES), lambda b, h, i: (b * nq + i, q_col + h)),
            pl.BlockSpec((seq, LANES), lambda b, h, i: (b, k_col + h)),
            pl.BlockSpec((seq, LANES), lambda b, h, i: (b, v_col + h)),
            pl.BlockSpec((1, 2, tq, tq), lambda b, h, i: (h, 0, 0, 0)),
            vec, vec, vec, vec,
            pl.BlockSpec((1, LANES), lambda b, h, i: (0, 0)),
        ],
        out_specs=pl.BlockSpec((tq, LANES), lambda b, h, i: (b * nq + i, h)),
        scratch_shapes=[pltpu.VMEM((2, tq, LANES), F32)] * 3,
        compiler_params=_params(("parallel", "parallel", "arbitrary"), blocks,
                                6 * tq * LANES * 4 + 6 * tq * tq * 4),
        name="attention",
    )(z, z, z, bias, lq1, lk1, lq2, lk2, sw)


def _merge_kernel(yg_ref, wglu_ref, bglu_ref, ya_ref, gs_ref, ga_ref, ps_ref, pa_ref, wo_ref,
                  x_ref, o_ref, ys_ref):
    j = pl.program_id(1)

    @pl.when(j == 0)
    def _():
        yg = yg_ref[...]
        g = jnp.dot(yg, wglu_ref[...], preferred_element_type=F32) + bglu_ref[...]
        ys_ref[...] = (yg.astype(F32) * jax.nn.sigmoid(g)).astype(BF16)
        o_ref[...] = x_ref[...]

    m = (gs_ref[...].astype(F32) * jnp.dot(ys_ref[...], ps_ref[...], preferred_element_type=F32)
         + ga_ref[...].astype(F32) * jnp.dot(ya_ref[...], pa_ref[...], preferred_element_type=F32))
    o_ref[...] += jnp.dot(m.astype(BF16), wo_ref[...], preferred_element_type=F32)


def _merge(yg, wglu, bglu, ya, z, ps, pa, wo, x2, *, gs_col, ga_col, tm, tj):
    n, d = x2.shape
    ws = yg.shape[1]
    wa = ya.shape[1]
    blocks = (tm * ws * 2 + ws * ws * 2 + ws * 4 + tm * wa * 2 + 2 * tm * tj * 2
              + ws * tj * 2 + wa * tj * 2 + tj * d * 2 + 2 * tm * d * 4)
    return pl.pallas_call(
        _merge_kernel,
        out_shape=jax.ShapeDtypeStruct((n, d), F32),
        grid=(n // tm, d // tj),
        in_specs=[
            pl.BlockSpec((tm, ws), lambda i, j: (i, 0)),
            pl.BlockSpec((ws, ws), lambda i, j: (0, 0)),
            pl.BlockSpec((1, ws), lambda i, j: (0, 0)),
            pl.BlockSpec((tm, wa), lambda i, j: (i, 0)),
            pl.BlockSpec((tm, tj), lambda i, j: (i, gs_col + j)),
            pl.BlockSpec((tm, tj), lambda i, j: (i, ga_col + j)),
            pl.BlockSpec((ws, tj), lambda i, j: (0, j)),
            pl.BlockSpec((wa, tj), lambda i, j: (0, j)),
            pl.BlockSpec((tj, d), lambda i, j: (j, 0)),
            pl.BlockSpec((tm, d), lambda i, j: (i, 0)),
        ],
        out_specs=pl.BlockSpec((tm, d), lambda i, j: (i, 0)),
        scratch_shapes=[pltpu.VMEM((tm, ws), BF16)],
        compiler_params=_params(("parallel", "arbitrary"), blocks,
                                tm * ws * 2 + 2 * tm * ws * 4 + 3 * tm * tj * 4 + tm * d * 4),
        name="merge",
    )(yg, wglu, bglu, ya, z, z, ps, pa, wo, x2)


def _ffn_kernel(x_ref, nw_ref, wg_ref, wu_ref, wd_ref, o_ref, h_ref):
    j = pl.program_id(1)

    @pl.when(j == 0)
    def _():
        x = x_ref[...]
        h_ref[...] = _rms_norm(x, nw_ref[...], RMS_EPS).astype(BF16)
        o_ref[...] = x

    h = h_ref[...]
    g = jnp.dot(h, wg_ref[...], preferred_element_type=F32)
    u = jnp.dot(h, wu_ref[...], preferred_element_type=F32)
    a = (jax.nn.silu(g) * u).astype(BF16)
    o_ref[...] += jnp.dot(a, wd_ref[...], preferred_element_type=F32)


def _ffn(x2, nw, wg, wu, wd, *, tm, tf):
    n, d = x2.shape
    dff = wg.shape[1]
    blocks = 2 * tm * d * 4 + d * 4 + 3 * d * tf * 2
    return pl.pallas_call(
        _ffn_kernel,
        out_shape=jax.ShapeDtypeStruct((n, d), F32),
        grid=(n // tm, dff // tf),
        in_specs=[
            pl.BlockSpec((tm, d), lambda i, j: (i, 0)),
            pl.BlockSpec((1, d), lambda i, j: (0, 0)),
            pl.BlockSpec((d, tf), lambda i, j: (0, j)),
            pl.BlockSpec((d, tf), lambda i, j: (0, j)),
            pl.BlockSpec((tf, d), lambda i, j: (j, 0)),
        ],
        out_specs=pl.BlockSpec((tm, d), lambda i, j: (i, 0)),
        scratch_shapes=[pltpu.VMEM((tm, d), BF16)],
        compiler_params=_params(("parallel", "arbitrary"), blocks,
                                tm * d * 2 + 4 * tm * tf * 4 + tm * d * 4),
        name="ffn",
    )(x2, nw, wg, wu, wd)


def _largest_tile(total, target, quantum):
    t = min(total, target)
    while total % t or t % quantum:
        t -= quantum
    return t


def kernel(x, norm1_w, w_in, lam_re, lam_im, log_step, ssm_b_re, ssm_b_im, ssm_c_re, ssm_c_im, ssm_d, w_glu, b_glu, q_norm_w, k_norm_w, lambda_q1, lambda_k1, lambda_q2, lambda_k2, subln_w, w_proj_ssm, w_proj_attn, w_out, rel_bias, norm2_w, w_ffn_gate, w_ffn_up, w_ffn_down):
    batch, seq, d_model = x.shape
    depth = w_in.shape[0]
    n = batch * seq
    _, groups, state = lam_re.shape
    group = ssm_b_re.shape[-1]
    ssm_width = groups * group
    head_dim = q_norm_w.shape[-1]
    v_dim = subln_w.shape[-1]
    attn_width = w_proj_attn.shape[1]
    n_heads = attn_width // v_dim
    d_ff = w_ffn_gate.shape[-1]

    tn = ssm_width
    assert ssm_width == attn_width and d_model % tn == 0
    assert w_in.shape[2] == ssm_width + 3 * attn_width + 2 * d_model
    assert 2 * head_dim == LANES and v_dim == LANES and 2 * state == LANES
    assert SSM_T * group == V7X_MXU_DIM
    kchunks = seq // SSM_T
    nsteps = kchunks.bit_length() - 1
    assert seq % SSM_T == 0 and kchunks == 1 << nsteps and nsteps <= 8

    tm = _largest_tile(n, 512, 8)
    tq = _largest_tile(seq, 256, MAX_DISTANCE)
    assert tq % CHUNK == 0
    tj = _largest_tile(d_model, 512, LANES)
    tf = _largest_tile(d_ff, 512, LANES)

    q_col = ssm_width // LANES
    k_col = q_col + attn_width // LANES
    v_col = k_col + attn_width // LANES
    gs_col = (ssm_width + 3 * attn_width) // tj
    ga_col = gs_col + d_model // tj

    bias = _bias_tiles(rel_bias, tq=tq)
    x2 = x.reshape(n, d_model)
    rows = n // SSM_T
    dup = lambda a: jnp.concatenate([a, a], axis=-1)[:, None, :]

    for l in range(depth):
        lambda_init = 0.8 - 0.6 * math.exp(-0.3 * l)

        qw2 = jnp.tile(q_norm_w[l], LANES // head_dim)[None, :]
        kw2 = jnp.tile(k_norm_w[l], LANES // head_dim)[None, :]
        z = _in_proj(x2, norm1_w[l][None, :], w_in[l].astype(BF16), qw2, kw2,
                     head_dim=head_dim, tm=tm, tn=tn)

        b_re_t = jnp.swapaxes(ssm_b_re[l], 1, 2)
        b_im_t = jnp.swapaxes(ssm_b_im[l], 1, 2)
        c_re, c_im = ssm_c_re[l], ssm_c_im[l]
        m_op, e_op, ft_op, p_op, q_op = _ssm_prep(
            dup(lam_re[l]), dup(lam_im[l]),
            jnp.broadcast_to(log_step[l][:, None, None], (groups, 1, 2 * state)),
            jnp.concatenate([b_re_t, b_im_t], axis=-1), jnp.concatenate([b_im_t, b_re_t], axis=-1),
            jnp.concatenate([c_re, -c_im], axis=-1), jnp.concatenate([-c_im, -c_re], axis=-1),
            jnp.tile(ssm_d[l].reshape(groups, 1, group), (1, 1, SSM_T)),
            nsteps=nsteps)
        u_t = (z[:, :ssm_width].reshape(rows, SSM_T, groups, group)
               .transpose(2, 0, 1, 3).reshape(groups, rows, SSM_T * group))
        y_t = _ssm(u_t, m_op, e_op, ft_op, p_op, q_op, kchunks=kchunks, nsteps=nsteps)
        yg = (y_t.reshape(groups, rows, SSM_T, group)
              .transpose(1, 2, 0, 3).reshape(n, ssm_width))

        ya = _attention(z, bias, lambda_q1[l][None, :], lambda_k1[l][None, :],
                        lambda_q2[l][None, :], lambda_k2[l][None, :], subln_w[l][None, :],
                        batch=batch, seq=seq, n_heads=n_heads, head_dim=head_dim, tq=tq,
                        q_col=q_col, k_col=k_col, v_col=v_col, lambda_init=lambda_init)

        x2 = _merge(yg, w_glu[l].astype(BF16), b_glu[l][None, :], ya, z,
                    w_proj_ssm[l].astype(BF16), w_proj_attn[l].astype(BF16),
                    w_out[l].astype(BF16), x2, gs_col=gs_col, ga_col=ga_col, tm=tm, tj=tj)
        x2 = _ffn(x2, norm2_w[l][None, :], w_ffn_gate[l].astype(BF16), w_ffn_up[l].astype(BF16),
                  w_ffn_down[l].astype(BF16), tm=tm, tf=tf)

    return x2.reshape(batch, seq, d_model)
```

```python
import functools
import math

import numpy as np
import jax
import jax.numpy as jnp
from jax import lax
from jax.experimental import pallas as pl
from jax.experimental.pallas import tpu as pltpu

F32 = jnp.float32
BF16 = jnp.bfloat16

CHUNK = 64
MAX_DISTANCE = 128
RMS_EPS = 1e-6
SUBLN_EPS = 1e-5
LAM_RE_MAX = -1e-4
MASK_VALUE = -1e30

LANES = 128
V7X_MXU_DIM = 256
V7X_VMEM_BYTES = 64 * 1024 * 1024

SSM_T = 16
SSM_ROWS = 256


def _vmem_limit(block_bytes, extra_bytes):
    return int(min(2 * block_bytes + extra_bytes + (4 << 20), V7X_VMEM_BYTES - (6 << 20)))


def _params(semantics, block_bytes, extra_bytes):
    return pltpu.CompilerParams(dimension_semantics=semantics,
                                vmem_limit_bytes=_vmem_limit(block_bytes, extra_bytes))


def _rms_norm(x, w, eps):
    ms = jnp.mean(x * x, axis=-1, keepdims=True)
    return x * lax.rsqrt(ms + eps) * w


def _dot_nt(a, b, **kw):
    return lax.dot_general(a, b, (((1,), (1,)), ((), ())), preferred_element_type=F32, **kw)


def _in_proj_kernel(x_ref, nw_ref, w_ref, qw_ref, kw_ref, u_ref, o_ref, h_ref, *, head_dim):
    j = pl.program_id(1)

    @pl.when(j == 0)
    def _():
        h_ref[...] = _rms_norm(x_ref[...], nw_ref[...], RMS_EPS).astype(BF16)

    z = jnp.dot(h_ref[...], w_ref[...], preferred_element_type=F32)
    tn = z.shape[1]

    def seg_norm_store(w_row, scale):
        lo = lax.broadcasted_iota(jnp.int32, (1, LANES), 1) < head_dim
        for c in range(tn // LANES):
            zc = z[:, c * LANES:(c + 1) * LANES]
            sq = zc * zc
            s_lo = jnp.sum(jnp.where(lo, sq, 0.0), axis=-1, keepdims=True)
            s_hi = jnp.sum(jnp.where(lo, 0.0, sq), axis=-1, keepdims=True)
            ms = jnp.where(lo, s_lo, s_hi) * (1.0 / head_dim)
            y = zc * lax.rsqrt(ms + RMS_EPS) * w_row
            o_ref[:, c * LANES:(c + 1) * LANES] = (y * scale).astype(o_ref.dtype)

    @pl.when(j == 0)
    def _():
        for b in range(u_ref.shape[0]):
            u_ref[b] = z[:, b * LANES:(b + 1) * LANES].astype(u_ref.dtype)

    @pl.when(j == 3)
    def _():
        o_ref[...] = z.astype(o_ref.dtype)

    @pl.when(j == 1)
    def _():
        seg_norm_store(qw_ref[...], head_dim ** -0.5)

    @pl.when(j == 2)
    def _():
        seg_norm_store(kw_ref[...], 1.0)

    @pl.when(j >= 4)
    def _():
        o_ref[...] = jax.nn.sigmoid(z).astype(o_ref.dtype)


def _in_proj(x2, nw, w, qw2, kw2, *, head_dim, tm, tn):
    n, d = x2.shape
    width = w.shape[1]
    blocks = tm * d * 4 + d * 4 + d * tn * 2 + 2 * tm * tn * 2
    return pl.pallas_call(
        functools.partial(_in_proj_kernel, head_dim=head_dim),
        out_shape=(jax.ShapeDtypeStruct((tn // LANES, n, LANES), BF16),
                   jax.ShapeDtypeStruct((n, width - tn), BF16)),
        grid=(n // tm, width // tn),
        in_specs=[
            pl.BlockSpec((tm, d), lambda i, j: (i, 0)),
            pl.BlockSpec((1, d), lambda i, j: (0, 0)),
            pl.BlockSpec((d, tn), lambda i, j: (0, j)),
            pl.BlockSpec((1, LANES), lambda i, j: (0, 0)),
            pl.BlockSpec((1, LANES), lambda i, j: (0, 0)),
        ],
        out_specs=(pl.BlockSpec((tn // LANES, tm, LANES), lambda i, j: (0, i, 0)),
                   pl.BlockSpec((tm, tn), lambda i, j: (i, jnp.maximum(j - 1, 0)))),
        scratch_shapes=[pltpu.VMEM((tm, d), BF16)],
        compiler_params=_params(("parallel", "arbitrary"), blocks, tm * d * 2 + 3 * tm * tn * 4),
        name="in_proj",
    )(x2, nw, w, qw2, kw2)


def _ssm_prep_kernel(lr_ref, li_ref, ls_ref, bcat_ref, bswap_ref, ccat_ref, cswap_ref, d_ref,
                     lrl_ref, lil_ref, lsl_ref, m_ref, e_ref, ft_ref, p_ref, q_ref, ft0_scr,
                     *, state, group, nsteps):
    gl = LANES // group
    lane = lax.broadcasted_iota(jnp.int32, (1, LANES), 1)
    sign = jnp.where(lane < state, -1.0, 1.0).astype(F32)
    lr = jnp.minimum(lr_ref[0], LAM_RE_MAX)
    li = li_ref[0]
    dt = jnp.exp(ls_ref[0])
    mag = jnp.exp(lr * dt)
    ab_r = mag * jnp.cos(li * dt)
    ab_i = mag * jnp.sin(li * dt)
    den = lr * lr + li * li
    nr = ab_r - 1.0
    fr = (nr * lr + ab_i * li) / den
    fi = (ab_i * lr - nr * li) / den
    bcat = bcat_ref[0]
    bswap = bswap_ref[0]
    bb_cat = fr * bcat + sign * fi * bswap
    bb_swap = fr * bswap - sign * fi * bcat
    ccat = ccat_ref[0]
    cswap = cswap_ref[0]

    row_group = lax.broadcasted_iota(jnp.int32, (LANES, 1), 0) // group

    def store_block_diag(ref, block, value):
        for g in range(gl):
            ref[0, block * LANES:(block + 1) * LANES, g * LANES:(g + 1) * LANES] = (
                jnp.where(row_group == g, value, 0.0).astype(ref.dtype))

    pw_r = jnp.ones((LANES, LANES), F32)
    pw_i = jnp.zeros((LANES, LANES), F32)
    for n in range(SSM_T + 1):
        ft_blk = pw_r * ccat + pw_i * cswap
        if n < SSM_T:
            ft0_scr[n * LANES:(n + 1) * LANES, :] = ft_blk
            store_block_diag(e_ref, SSM_T - 1 - n, pw_r * bb_cat + sign * pw_i * bb_swap)
        if n >= 1:
            store_block_diag(ft_ref, n - 1, ft_blk)
        pw_r, pw_i = pw_r * ab_r - pw_i * ab_i, pw_r * ab_i + pw_i * ab_r

    width = SSM_T * LANES
    r = _dot_nt(bb_cat, ft0_scr[...], precision=lax.Precision.HIGHEST)
    col = lax.broadcasted_iota(jnp.int32, (LANES, width), 1)
    row = lax.broadcasted_iota(jnp.int32, (LANES, width), 0)
    r = jnp.where(row // group == (col % LANES) // group, r, 0.0)
    r = r + jnp.where(col == row, jnp.tile(d_ref[0], (1, SSM_T)), 0.0)
    for j in range(SSM_T):
        rows_j = slice(j * LANES, (j + 1) * LANES)
        if j:
            m_ref[0, rows_j, 0:j * LANES] = jnp.zeros((LANES, j * LANES), m_ref.dtype)
        m_ref[0, rows_j, j * LANES:width] = r[:, 0:width - j * LANES].astype(m_ref.dtype)

    lrl = jnp.minimum(lrl_ref[0], LAM_RE_MAX)
    dtl = jnp.exp(lsl_ref[0])
    lane_l = lax.broadcasted_iota(jnp.int32, lrl.shape, 1)
    sign_l = jnp.where(lane_l % LANES < state, -1.0, 1.0).astype(F32)
    steps = lax.broadcasted_iota(jnp.int32, (8, 1), 0)
    nd = (SSM_T * jnp.left_shift(1, jnp.minimum(steps, nsteps))).astype(F32)
    mag_l = jnp.exp(lrl * dtl * nd)
    ang_l = lil_ref[0] * dtl * nd
    p_ref[0] = mag_l * jnp.cos(ang_l)
    q_ref[0] = sign_l * mag_l * jnp.sin(ang_l)


def _ssm_prep(rows_in, lanes_in, *, state, group, nsteps):
    nb = rows_in[0].shape[0]
    gl = LANES // group
    width = SSM_T * LANES
    bspec = lambda *shape: pl.BlockSpec((1,) + shape, lambda i: (i, 0, 0))
    blocks = width * width * 2 + 2 * width * gl * LANES * 2 + 8 * LANES * LANES * 4
    return pl.pallas_call(
        functools.partial(_ssm_prep_kernel, state=state, group=group, nsteps=nsteps),
        out_shape=(
            jax.ShapeDtypeStruct((nb, width, width), BF16),
            jax.ShapeDtypeStruct((nb, width, gl * LANES), BF16),
            jax.ShapeDtypeStruct((nb, width, gl * LANES), BF16),
            jax.ShapeDtypeStruct((nb, 8, gl * LANES), F32),
            jax.ShapeDtypeStruct((nb, 8, gl * LANES), F32),
        ),
        grid=(nb,),
        in_specs=[bspec(LANES, LANES)] * 8 + [bspec(1, gl * LANES)] * 3,
        out_specs=(bspec(width, width), bspec(width, gl * LANES), bspec(width, gl * LANES),
                   bspec(8, gl * LANES), bspec(8, gl * LANES)),
        scratch_shapes=[pltpu.VMEM((width, LANES), F32)],
        compiler_params=_params(("parallel",), blocks, width * LANES * 4 + 6 * LANES * width * 4),
        name="ssm_prep",
    )(*rows_in, *lanes_in)


SSM_COL_SPLIT = 4


def _ssm_kernel(u_ref, m_ref, e_ref, ft_ref, p_ref, q_ref, o_ref, *, kchunks, nsteps, state):
    u = u_ref[0]
    rows, width = u.shape
    s = jnp.dot(u, e_ref[0], preferred_element_type=F32)
    kidx = lax.broadcasted_iota(jnp.int32, (rows, 1), 0) & (kchunks - 1)
    s_prev = []
    for g in range(s.shape[1] // LANES):
        lanes = slice(g * LANES, (g + 1) * LANES)
        sg = s[:, lanes]
        for i in range(nsteps):
            d = 1 << i
            x = jnp.where(kidx >= d, pltpu.roll(sg, d, axis=0), 0.0)
            sg = sg + p_ref[0, i:i + 1, lanes] * x + q_ref[0, i:i + 1, lanes] * pltpu.roll(x, state, axis=1)
        s_prev.append(jnp.where(kidx >= 1, pltpu.roll(sg, 1, axis=0), 0.0).astype(BF16))
    s_prev = jnp.concatenate(s_prev, axis=1)
    cw = width // SSM_COL_SPLIT
    for c in range(SSM_COL_SPLIT):
        cols = slice(c * cw, (c + 1) * cw)
        kk = (c + 1) * cw
        y = (jnp.dot(u[:, 0:kk], m_ref[0, 0:kk, cols], preferred_element_type=F32)
             + _dot_nt(s_prev, ft_ref[0, cols, :]))
        o_ref[0, :, cols] = jax.nn.gelu(y).astype(o_ref.dtype)


def _ssm(u4, m, e, ft, p, q, *, kchunks, nsteps, rt):
    nb, rows, width = u4.shape
    sl = e.shape[2]
    blocks = 2 * rt * width * 2 + width * width * 2 + 2 * width * sl * 2 + 2 * 8 * sl * 4
    return pl.pallas_call(
        functools.partial(_ssm_kernel, kchunks=kchunks, nsteps=nsteps, state=LANES // 2),
        out_shape=jax.ShapeDtypeStruct((nb, rows, width), BF16),
        grid=(nb, rows // rt),
        in_specs=[pl.BlockSpec((1, rt, width), lambda b, r: (b, r, 0)),
                  pl.BlockSpec((1, width, width), lambda b, r: (b, 0, 0)),
                  pl.BlockSpec((1, width, sl), lambda b, r: (b, 0, 0)),
                  pl.BlockSpec((1, width, sl), lambda b, r: (b, 0, 0)),
                  pl.BlockSpec((1, 8, sl), lambda b, r: (b, 0, 0)),
                  pl.BlockSpec((1, 8, sl), lambda b, r: (b, 0, 0))],
        out_specs=pl.BlockSpec((1, rt, width), lambda b, r: (b, r, 0)),
        compiler_params=_params(("parallel", "arbitrary"), blocks,
                                4 * rt * sl * 4 + 4 * rt * (width // SSM_COL_SPLIT) * 4),
        name="ssm",
    )(u4, m, e, ft, p, q)


def _bias_kernel(tab_ref, idx_ref, o_ref, *, n_buckets, far_bucket):
    h = pl.program_id(0)
    far = tab_ref[far_bucket, h]
    for t in range(idx_ref.shape[0]):
        idx = idx_ref[t]
        acc = jnp.full(idx.shape, MASK_VALUE, F32)
        for b in range(n_buckets):
            acc = jnp.where(idx == b, tab_ref[b, h] - far, acc)
        o_ref[0, t] = acc


def _t5_bucket_np(rel, n_buckets):
    nb = n_buckets // 2
    ret = np.where(rel > 0, nb, 0)
    n = np.abs(rel)
    max_exact = nb // 2
    nf = np.maximum(n, 1).astype(np.float32)
    large = max_exact + (np.log(nf / np.float32(max_exact)) / np.float32(math.log(MAX_DISTANCE / max_exact))
                         * np.float32(nb - max_exact)).astype(np.int32)
    large = np.minimum(large, nb - 1)
    return (ret + np.where(n < max_exact, n, large)).astype(np.int32)


def _bias_tiles(rel_bias, *, tq):
    n_buckets, n_heads = rel_bias.shape
    c = np.arange(tq)[:, None]
    r = np.arange(tq)[None, :]
    prev = _t5_bucket_np(c - r - tq, n_buckets)
    diag = np.where(c // CHUNK <= r // CHUNK, _t5_bucket_np(c - r, n_buckets), -1)
    idx = np.stack([prev, diag])
    far_bucket = int(_t5_bucket_np(np.array(-MAX_DISTANCE), n_buckets))
    assert tq >= MAX_DISTANCE and (prev[0, :] == far_bucket).all()
    return pl.pallas_call(
        functools.partial(_bias_kernel, n_buckets=n_buckets, far_bucket=far_bucket),
        out_shape=jax.ShapeDtypeStruct((n_heads, 2, tq, tq), F32),
        grid=(n_heads,),
        in_specs=[pl.BlockSpec(memory_space=pltpu.SMEM),
                  pl.BlockSpec((2, tq, tq), lambda h: (0, 0, 0))],
        out_specs=pl.BlockSpec((1, 2, tq, tq), lambda h: (h, 0, 0, 0)),
        compiler_params=pltpu.CompilerParams(dimension_semantics=("parallel",)),
        name="attn_bias",
    )(rel_bias, jnp.asarray(idx))


ONES_ROWS = 16


def _attn_kernel(q_ref, k_ref, v_ref, bias_ref, lq1_ref, lk1_ref, lq2_ref, lk2_ref, sw_ref,
                 o_ref, vt_scr, qm_scr, st_a, st_b, m_scr, acc_scr, *, tq, head_dim, lambda_init):
    i = pl.program_id(2)
    vd = v_ref.shape[1]

    @pl.when(i == 0)
    def _():
        for c in range(v_ref.shape[0] // tq):
            cols = slice(c * tq, (c + 1) * tq)
            vt_scr[0:vd, cols] = v_ref[cols, :].astype(F32).T.astype(BF16)
        vt_scr[vd:, :] = jnp.ones((ONES_ROWS, vt_scr.shape[1]), BF16)

    q = q_ref[...]
    lo = lax.broadcasted_iota(jnp.int32, (1, LANES), 1) < head_dim
    zero = jnp.zeros_like(q)
    qm_scr[0] = jnp.where(lo, q, zero)
    qm_scr[1] = jnp.where(lo, zero, q)
    m_scr[...] = jnp.full(m_scr.shape, MASK_VALUE, F32)
    acc_scr[...] = jnp.zeros(acc_scr.shape, F32)

    def keys_of(g):
        return pl.ds(pl.multiple_of(g * tq, tq), tq)

    def qk(g, st_ref):
        kb = k_ref[keys_of(g), :]
        for m in range(2):
            st_ref[m] = _dot_nt(kb, qm_scr[m])

    def softmax_pv(g, st_ref, bias):
        vt = vt_scr[:, keys_of(g)]
        for m in range(2):
            st = st_ref[m]
            if bias is not None:
                st = st + bias
            m_prev = m_scr[m]
            m_new = jnp.maximum(m_prev, jnp.max(st, axis=0, keepdims=True))
            p = jnp.exp(st - m_new).astype(BF16)
            alpha = jnp.exp(m_prev - m_new)
            acc_scr[m] = alpha * acc_scr[m] + jnp.dot(vt, p, preferred_element_type=F32)
            m_scr[m] = m_new

    prev_bias = lambda: bias_ref[0, 0]
    diag_bias = lambda: bias_ref[0, 1]
    far = jnp.maximum(i - 1, 0)
    qk(0, st_a)

    def far_pair(t, carry):
        g = 2 * t
        qk(g + 1, st_b)
        softmax_pv(g, st_a, None)
        qk(g + 2, st_a)
        softmax_pv(g + 1, st_b, None)
        return carry

    lax.fori_loop(0, far // 2, far_pair, 0)
    c = far & ~1

    @pl.when(far & 1 == 1)
    def _():
        qk(c + 1, st_b)
        softmax_pv(c, st_a, None)
        qk(c + 2, st_a)
        softmax_pv(c + 1, st_b, prev_bias())
        softmax_pv(c + 2, st_a, diag_bias())

    @pl.when(jnp.logical_and(far & 1 == 0, i >= 1))
    def _():
        qk(c + 1, st_b)
        softmax_pv(c, st_a, prev_bias())
        softmax_pv(c + 1, st_b, diag_bias())

    @pl.when(i == 0)
    def _():
        softmax_pv(0, st_a, diag_bias())

    s1 = jnp.sum(lq1_ref[...] * lk1_ref[...], axis=-1, keepdims=True)
    s2 = jnp.sum(lq2_ref[...] * lk2_ref[...], axis=-1, keepdims=True)
    lam = jnp.exp(s1) - jnp.exp(s2) + lambda_init
    a1 = acc_scr[0]
    a2 = acc_scr[1]
    ot = a1[0:vd] * (1.0 / a1[vd:vd + 1]) - lam * (a2[0:vd] * (1.0 / a2[vd:vd + 1]))
    y = _rms_norm(ot.T, sw_ref[...], SUBLN_EPS) * (1.0 - lambda_init)
    o_ref[...] = y.astype(o_ref.dtype)


def _attention(z, bias, lq1, lk1, lq2, lk2, sw, *, batch, seq, n_heads, head_dim, tq,
               q_col, k_col, v_col, lambda_init):
    n = z.shape[0]
    nq = seq // tq
    acc_rows = LANES + ONES_ROWS
    vec = pl.BlockSpec((1, head_dim), lambda b, h, i: (0, 0))
    blocks = tq * LANES * 2 * 2 + 2 * seq * LANES * 2 + 2 * tq * tq * 4 + tq * LANES * 2
    scratch = (acc_rows * seq * 2 + 2 * tq * LANES * 2 + 4 * tq * tq * 4 + 2 * tq * 4 * 8
               + 2 * acc_rows * tq * 4)
    return pl.pallas_call(
        functools.partial(_attn_kernel, tq=tq, head_dim=head_dim, lambda_init=lambda_init),
        out_shape=jax.ShapeDtypeStruct((n, n_heads * LANES), BF16),
        grid=(batch, n_heads, nq),
        in_specs=[
            pl.BlockSpec((tq, LANES), lambda b, h, i: (b * nq + i, q_col + h)),
            pl.BlockSpec((seq, LANES), lambda b, h, i: (b, k_col + h)),
            pl.BlockSpec((seq, LANES), lambda b, h, i: (b, v_col + h)),
            pl.BlockSpec((1, 2, tq, tq), lambda b, h, i: (h, 0, 0, 0)),
            vec, vec, vec, vec,
            pl.BlockSpec((1, LANES), lambda b, h, i: (0, 0)),
        ],
        out_specs=pl.BlockSpec((tq, LANES), lambda b, h, i: (b * nq + i, h)),
        scratch_shapes=[pltpu.VMEM((acc_rows, seq), BF16),
                        pltpu.VMEM((2, tq, LANES), BF16),
                        pltpu.VMEM((2, tq, tq), F32),
                        pltpu.VMEM((2, tq, tq), F32),
                        pltpu.VMEM((2, 1, tq), F32),
                        pltpu.VMEM((2, acc_rows, tq), F32)],
        compiler_params=_params(("parallel", "parallel", "arbitrary"), blocks,
                                scratch + 4 * tq * tq * 4),
        name="attention",
    )(z, z, z, bias, lq1, lk1, lq2, lk2, sw)


def _merge_kernel(yg_ref, wglu_ref, bglu_ref, ya_ref, gs_ref, ga_ref, ps_ref, pa_ref, wo_ref,
                  x_ref, o_ref, ys_ref):
    j = pl.program_id(1)

    @pl.when(j == 0)
    def _():
        yg = jnp.concatenate([yg_ref[b] for b in range(yg_ref.shape[0])], axis=1)
        g = jnp.dot(yg, wglu_ref[...], preferred_element_type=F32) + bglu_ref[...]
        ys_ref[...] = (yg.astype(F32) * jax.nn.sigmoid(g)).astype(BF16)
        o_ref[...] = x_ref[...]

    m = (gs_ref[...].astype(F32) * jnp.dot(ys_ref[...], ps_ref[...], preferred_element_type=F32)
         + ga_ref[...].astype(F32) * jnp.dot(ya_ref[...], pa_ref[...], preferred_element_type=F32))
    o_ref[...] += jnp.dot(m.astype(BF16), wo_ref[...], preferred_element_type=F32)


def _merge(yg, wglu, bglu, ya, z, ps, pa, wo, x2, *, gs_col, ga_col, tm, tj):
    n, d = x2.shape
    ws = yg.shape[0] * LANES
    wa = ya.shape[1]
    blocks = (tm * ws * 2 + ws * ws * 2 + ws * 4 + tm * wa * 2 + 2 * tm * tj * 2
              + ws * tj * 2 + wa * tj * 2 + tj * d * 2 + 2 * tm * d * 4)
    return pl.pallas_call(
        _merge_kernel,
        out_shape=jax.ShapeDtypeStruct((n, d), F32),
        grid=(n // tm, d // tj),
        in_specs=[
            pl.BlockSpec((ws // LANES, tm, LANES), lambda i, j: (0, i, 0)),
            pl.BlockSpec((ws, ws), lambda i, j: (0, 0)),
            pl.BlockSpec((1, ws), lambda i, j: (0, 0)),
            pl.BlockSpec((tm, wa), lambda i, j: (i, 0)),
            pl.BlockSpec((tm, tj), lambda i, j: (i, gs_col + j)),
            pl.BlockSpec((tm, tj), lambda i, j: (i, ga_col + j)),
            pl.BlockSpec((ws, tj), lambda i, j: (0, j)),
            pl.BlockSpec((wa, tj), lambda i, j: (0, j)),
            pl.BlockSpec((tj, d), lambda i, j: (j, 0)),
            pl.BlockSpec((tm, d), lambda i, j: (i, 0)),
        ],
        out_specs=pl.BlockSpec((tm, d), lambda i, j: (i, 0)),
        scratch_shapes=[pltpu.VMEM((tm, ws), BF16)],
        compiler_params=_params(("parallel", "arbitrary"), blocks,
                                tm * ws * 2 + 2 * tm * ws * 4 + 3 * tm * tj * 4 + tm * d * 4),
        name="merge",
    )(yg, wglu, bglu, ya, z, z, ps, pa, wo, x2)


def _ffn_kernel(x_ref, nw_ref, wg_ref, wu_ref, wd_ref, o_ref, h_ref):
    j = pl.program_id(1)

    @pl.when(j == 0)
    def _():
        x = x_ref[...]
        h_ref[...] = _rms_norm(x, nw_ref[...], RMS_EPS).astype(BF16)
        o_ref[...] = x

    h = h_ref[...]
    g = jnp.dot(h, wg_ref[...], preferred_element_type=F32)
    u = jnp.dot(h, wu_ref[...], preferred_element_type=F32)
    a = (jax.nn.silu(g) * u).astype(BF16)
    o_ref[...] += jnp.dot(a, wd_ref[...], preferred_element_type=F32)


def _ffn(x2, nw, wg, wu, wd, *, tm, tf):
    n, d = x2.shape
    dff = wg.shape[1]
    blocks = 2 * tm * d * 4 + d * 4 + 3 * d * tf * 2
    return pl.pallas_call(
        _ffn_kernel,
        out_shape=jax.ShapeDtypeStruct((n, d), F32),
        grid=(n // tm, dff // tf),
        in_specs=[
            pl.BlockSpec((tm, d), lambda i, j: (i, 0)),
            pl.BlockSpec((1, d), lambda i, j: (0, 0)),
            pl.BlockSpec((d, tf), lambda i, j: (0, j)),
            pl.BlockSpec((d, tf), lambda i, j: (0, j)),
            pl.BlockSpec((tf, d), lambda i, j: (j, 0)),
        ],
        out_specs=pl.BlockSpec((tm, d), lambda i, j: (i, 0)),
        scratch_shapes=[pltpu.VMEM((tm, d), BF16)],
        compiler_params=_params(("parallel", "arbitrary"), blocks,
                                tm * d * 2 + 4 * tm * tf * 4 + tm * d * 4),
        name="ffn",
    )(x2, nw, wg, wu, wd)


def _largest_tile(total, target, quantum):
    t = min(total, target)
    while total % t or t % quantum:
        t -= quantum
    return t


def kernel(x, norm1_w, w_in, lam_re, lam_im, log_step, ssm_b_re, ssm_b_im, ssm_c_re, ssm_c_im, ssm_d, w_glu, b_glu, q_norm_w, k_norm_w, lambda_q1, lambda_k1, lambda_q2, lambda_k2, subln_w, w_proj_ssm, w_proj_attn, w_out, rel_bias, norm2_w, w_ffn_gate, w_ffn_up, w_ffn_down):
    batch, seq, d_model = x.shape
    depth = w_in.shape[0]
    n = batch * seq
    _, groups, state = lam_re.shape
    group = ssm_b_re.shape[-1]
    ssm_width = groups * group
    head_dim = q_norm_w.shape[-1]
    v_dim = subln_w.shape[-1]
    attn_width = w_proj_attn.shape[1]
    n_heads = attn_width // v_dim
    d_ff = w_ffn_gate.shape[-1]

    tn = ssm_width
    assert ssm_width == attn_width and d_model % tn == 0
    assert w_in.shape[2] == ssm_width + 3 * attn_width + 2 * d_model
    assert 2 * head_dim == LANES and v_dim == LANES and 2 * state == LANES
    assert SSM_T * group == V7X_MXU_DIM
    kchunks = seq // SSM_T
    nsteps = kchunks.bit_length() - 1
    assert seq % SSM_T == 0 and kchunks == 1 << nsteps and nsteps <= 8

    tm = _largest_tile(n, 512, 8)
    tq = _largest_tile(seq, 512, MAX_DISTANCE)
    assert tq % CHUNK == 0
    tj = _largest_tile(d_model, 512, LANES)
    tf = _largest_tile(d_ff, 512, LANES)

    q_col = 0
    k_col = q_col + attn_width // LANES
    v_col = k_col + attn_width // LANES
    gs_col = 3 * attn_width // tj
    ga_col = gs_col + d_model // tj

    bias = _bias_tiles(rel_bias, tq=tq)
    x2 = x.reshape(n, d_model)
    rows = n // SSM_T
    nblk = ssm_width // LANES
    seqs_per_tile = math.gcd(batch, max(1, SSM_ROWS // kchunks))
    rt = kchunks * seqs_per_tile
    dup = lambda a: jnp.concatenate([a, a], axis=-1)
    per_row = lambda a: jnp.repeat(a, group, axis=0).reshape(nblk, LANES, LANES)
    per_lane = lambda a: a.reshape(nblk, 1, -1)
    as_rows = lambda a: a.reshape(nblk, LANES, LANES)

    for l in range(depth):
        lambda_init = 0.8 - 0.6 * math.exp(-0.3 * l)

        qw2 = jnp.tile(q_norm_w[l], LANES // head_dim)[None, :]
        kw2 = jnp.tile(k_norm_w[l], LANES // head_dim)[None, :]
        u3, z = _in_proj(x2, norm1_w[l][None, :], w_in[l].astype(BF16), qw2, kw2,
                         head_dim=head_dim, tm=tm, tn=tn)

        lam_re2, lam_im2 = dup(lam_re[l]), dup(lam_im[l])
        step2 = jnp.broadcast_to(log_step[l][:, None], (groups, 2 * state))
        b_re_t = jnp.swapaxes(ssm_b_re[l], 1, 2)
        b_im_t = jnp.swapaxes(ssm_b_im[l], 1, 2)
        c_re, c_im = ssm_c_re[l], ssm_c_im[l]
        d_rows = jnp.broadcast_to(ssm_d[l][:, None], (ssm_width, LANES)).reshape(nblk, LANES, LANES)
        m_op, e_op, ft_op, p_op, q_op = _ssm_prep(
            (per_row(lam_re2), per_row(lam_im2), per_row(step2),
             as_rows(jnp.concatenate([b_re_t, b_im_t], axis=-1)),
             as_rows(jnp.concatenate([b_im_t, b_re_t], axis=-1)),
             as_rows(jnp.concatenate([c_re, -c_im], axis=-1)),
             as_rows(jnp.concatenate([-c_im, -c_re], axis=-1)), d_rows),
            (per_lane(lam_re2), per_lane(lam_im2), per_lane(step2)),
            state=state, group=group, nsteps=nsteps)
        y4 = _ssm(u3.reshape(nblk, rows, SSM_T * LANES), m_op, e_op, ft_op, p_op, q_op,
                  kchunks=kchunks, nsteps=nsteps, rt=rt)
        yg = y4.reshape(nblk, n, LANES)

        ya = _attention(z, bias, lambda_q1[l][None, :], lambda_k1[l][None, :],
                        lambda_q2[l][None, :], lambda_k2[l][None, :], subln_w[l][None, :],
                        batch=batch, seq=seq, n_heads=n_heads, head_dim=head_dim, tq=tq,
                        q_col=q_col, k_col=k_col, v_col=v_col, lambda_init=lambda_init)

        x2 = _merge(yg, w_glu[l].astype(BF16), b_glu[l][None, :], ya, z,
                    w_proj_ssm[l].astype(BF16), w_proj_attn[l].astype(BF16),
                    w_out[l].astype(BF16), x2, gs_col=gs_col, ga_col=ga_col, tm=tm, tj=tj)
        x2 = _ffn(x2, norm2_w[l][None, :], w_ffn_gate[l].astype(BF16), w_ffn_up[l].astype(BF16),
                  w_ffn_down[l].astype(BF16), tm=tm, tf=tf)

    return x2.reshape(batch, seq, d_model)
```

```python
import functools
import math

import numpy as np
import jax
import jax.numpy as jnp
from jax import lax
from jax.experimental import pallas as pl
from jax.experimental.pallas import tpu as pltpu

F32 = jnp.float32
BF16 = jnp.bfloat16

CHUNK = 64
MAX_DISTANCE = 128
RMS_EPS = 1e-6
SUBLN_EPS = 1e-5
LAM_RE_MAX = -1e-4
MASK_VALUE = -1e30
LOG2_E = math.log2(math.e)

LANES = 128
V7X_MXU_DIM = 256
V7X_VMEM_BYTES = 64 * 1024 * 1024

PROJ_CHUNK = V7X_MXU_DIM
SSM_T = 16
SSM_ROWS = 256


def _vmem_limit(block_bytes, extra_bytes):
    return int(min(2 * block_bytes + extra_bytes + (4 << 20), V7X_VMEM_BYTES - (6 << 20)))


def _params(semantics, block_bytes, extra_bytes):
    return pltpu.CompilerParams(dimension_semantics=semantics,
                                vmem_limit_bytes=_vmem_limit(block_bytes, extra_bytes))


def _rms_norm(x, w, eps):
    ms = jnp.mean(x * x, axis=-1, keepdims=True)
    return x * lax.rsqrt(ms + eps) * w


def _dot_nt(a, b, **kw):
    return lax.dot_general(a, b, (((1,), (1,)), ((), ())), preferred_element_type=F32, **kw)


def _in_proj_kernel(x_ref, nw_ref, w_ref, qw_ref, kw_ref, u_ref, o_ref, h_ref, *, head_dim):
    j = pl.program_id(1)

    @pl.when(j == 0)
    def _():
        h_ref[...] = _rms_norm(x_ref[...], nw_ref[...], RMS_EPS).astype(BF16)

    tn = w_ref.shape[1]

    def project(epilogue):
        h = h_ref[...]
        for c in range(tn // PROJ_CHUNK):
            zc = jnp.dot(h, w_ref[:, c * PROJ_CHUNK:(c + 1) * PROJ_CHUNK], preferred_element_type=F32)
            for s in range(PROJ_CHUNK // LANES):
                epilogue(c * (PROJ_CHUNK // LANES) + s, zc[:, s * LANES:(s + 1) * LANES])

    def store_z(b, y):
        o_ref[:, b * LANES:(b + 1) * LANES] = y.astype(o_ref.dtype)

    def seg_norm(w_row, scale):
        lo = lax.broadcasted_iota(jnp.int32, (1, LANES), 1) < head_dim

        def epilogue(b, zb):
            sq = zb * zb
            s_lo = jnp.sum(jnp.where(lo, sq, 0.0), axis=-1, keepdims=True)
            s_hi = jnp.sum(jnp.where(lo, 0.0, sq), axis=-1, keepdims=True)
            ms = jnp.where(lo, s_lo, s_hi) * (1.0 / head_dim)
            store_z(b, zb * lax.rsqrt(ms + RMS_EPS) * w_row * scale)
        return epilogue

    def store_u(b, zb):
        u_ref[b] = zb.astype(u_ref.dtype)

    pl.when(j == 0)(lambda: project(store_u))
    pl.when(j == 1)(lambda: project(seg_norm(qw_ref[...], head_dim ** -0.5 * LOG2_E)))
    pl.when(j == 2)(lambda: project(seg_norm(kw_ref[...], 1.0)))
    pl.when(j == 3)(lambda: project(store_z))
    pl.when(j >= 4)(lambda: project(lambda b, zb: store_z(b, jax.nn.sigmoid(zb))))


def _in_proj(x2, nw, w, qw2, kw2, *, head_dim, tm, tn):
    n, d = x2.shape
    width = w.shape[1]
    blocks = tm * d * 4 + d * 4 + d * tn * 2 + 2 * tm * tn * 2
    return pl.pallas_call(
        functools.partial(_in_proj_kernel, head_dim=head_dim),
        out_shape=(jax.ShapeDtypeStruct((tn // LANES, n, LANES), BF16),
                   jax.ShapeDtypeStruct((n, width - tn), BF16)),
        grid=(n // tm, width // tn),
        in_specs=[
            pl.BlockSpec((tm, d), lambda i, j: (i, 0)),
            pl.BlockSpec((1, d), lambda i, j: (0, 0)),
            pl.BlockSpec((d, tn), lambda i, j: (0, j)),
            pl.BlockSpec((1, LANES), lambda i, j: (0, 0)),
            pl.BlockSpec((1, LANES), lambda i, j: (0, 0)),
        ],
        out_specs=(pl.BlockSpec((tn // LANES, tm, LANES), lambda i, j: (0, i, 0)),
                   pl.BlockSpec((tm, tn), lambda i, j: (i, jnp.maximum(j - 1, 0)))),
        scratch_shapes=[pltpu.VMEM((tm, d), BF16)],
        compiler_params=_params(("parallel", "arbitrary"), blocks, tm * d * 2 + 3 * tm * tn * 4),
        name="in_proj",
    )(x2, nw, w, qw2, kw2)


def _ssm_prep_kernel(lr_ref, li_ref, ls_ref, bcat_ref, bswap_ref, ccat_ref, cswap_ref, d_ref,
                     lrl_ref, lil_ref, lsl_ref, m_ref, e_ref, ft_ref, p_ref, q_ref, ft0_scr,
                     *, state, group, nsteps):
    gl = LANES // group
    lane = lax.broadcasted_iota(jnp.int32, (1, LANES), 1)
    sign = jnp.where(lane < state, -1.0, 1.0).astype(F32)
    lr = jnp.minimum(lr_ref[0], LAM_RE_MAX)
    li = li_ref[0]
    dt = jnp.exp(ls_ref[0])
    mag = jnp.exp(lr * dt)
    ab_r = mag * jnp.cos(li * dt)
    ab_i = mag * jnp.sin(li * dt)
    den = lr * lr + li * li
    nr = ab_r - 1.0
    fr = (nr * lr + ab_i * li) / den
    fi = (ab_i * lr - nr * li) / den
    bcat = bcat_ref[0]
    bswap = bswap_ref[0]
    bb_cat = fr * bcat + sign * fi * bswap
    bb_swap = fr * bswap - sign * fi * bcat
    ccat = ccat_ref[0]
    cswap = cswap_ref[0]

    row_group = lax.broadcasted_iota(jnp.int32, (LANES, 1), 0) // group

    def store_block_diag(ref, block, value):
        for g in range(gl):
            ref[0, block * LANES:(block + 1) * LANES, g * LANES:(g + 1) * LANES] = (
                jnp.where(row_group == g, value, 0.0).astype(ref.dtype))

    pw_r = jnp.ones((LANES, LANES), F32)
    pw_i = jnp.zeros((LANES, LANES), F32)
    for n in range(SSM_T + 1):
        ft_blk = pw_r * ccat + pw_i * cswap
        if n < SSM_T:
            ft0_scr[n * LANES:(n + 1) * LANES, :] = ft_blk
            store_block_diag(e_ref, SSM_T - 1 - n, pw_r * bb_cat + sign * pw_i * bb_swap)
        if n >= 1:
            store_block_diag(ft_ref, n - 1, ft_blk)
        pw_r, pw_i = pw_r * ab_r - pw_i * ab_i, pw_r * ab_i + pw_i * ab_r

    width = SSM_T * LANES
    r = _dot_nt(bb_cat, ft0_scr[...], precision=lax.Precision.HIGHEST)
    col = lax.broadcasted_iota(jnp.int32, (LANES, width), 1)
    row = lax.broadcasted_iota(jnp.int32, (LANES, width), 0)
    r = jnp.where(row // group == (col % LANES) // group, r, 0.0)
    r = r + jnp.where(col == row, jnp.tile(d_ref[0], (1, SSM_T)), 0.0)
    for j in range(SSM_T):
        rows_j = slice(j * LANES, (j + 1) * LANES)
        if j:
            m_ref[0, rows_j, 0:j * LANES] = jnp.zeros((LANES, j * LANES), m_ref.dtype)
        m_ref[0, rows_j, j * LANES:width] = r[:, 0:width - j * LANES].astype(m_ref.dtype)

    lrl = jnp.minimum(lrl_ref[0], LAM_RE_MAX)
    dtl = jnp.exp(lsl_ref[0])
    lane_l = lax.broadcasted_iota(jnp.int32, lrl.shape, 1)
    sign_l = jnp.where(lane_l % LANES < state, -1.0, 1.0).astype(F32)
    steps = lax.broadcasted_iota(jnp.int32, (8, 1), 0)
    nd = (SSM_T * jnp.left_shift(1, jnp.minimum(steps, nsteps))).astype(F32)
    mag_l = jnp.exp(lrl * dtl * nd)
    ang_l = lil_ref[0] * dtl * nd
    p_ref[0] = mag_l * jnp.cos(ang_l)
    q_ref[0] = sign_l * mag_l * jnp.sin(ang_l)


def _ssm_prep(rows_in, lanes_in, *, state, group, nsteps):
    nb = rows_in[0].shape[0]
    gl = LANES // group
    width = SSM_T * LANES
    bspec = lambda *shape: pl.BlockSpec((1,) + shape, lambda i: (i, 0, 0))
    blocks = width * width * 2 + 2 * width * gl * LANES * 2 + 8 * LANES * LANES * 4
    return pl.pallas_call(
        functools.partial(_ssm_prep_kernel, state=state, group=group, nsteps=nsteps),
        out_shape=(
            jax.ShapeDtypeStruct((nb, width, width), BF16),
            jax.ShapeDtypeStruct((nb, width, gl * LANES), BF16),
            jax.ShapeDtypeStruct((nb, width, gl * LANES), BF16),
            jax.ShapeDtypeStruct((nb, 8, gl * LANES), F32),
            jax.ShapeDtypeStruct((nb, 8, gl * LANES), F32),
        ),
        grid=(nb,),
        in_specs=[bspec(LANES, LANES)] * 8 + [bspec(1, gl * LANES)] * 3,
        out_specs=(bspec(width, width), bspec(width, gl * LANES), bspec(width, gl * LANES),
                   bspec(8, gl * LANES), bspec(8, gl * LANES)),
        scratch_shapes=[pltpu.VMEM((width, LANES), F32)],
        compiler_params=_params(("parallel",), blocks, width * LANES * 4 + 6 * LANES * width * 4),
        name="ssm_prep",
    )(*rows_in, *lanes_in)


SSM_COL_SPLIT = 4


def _ssm_kernel(u_ref, m_ref, e_ref, ft_ref, p_ref, q_ref, o_ref, *, kchunks, nsteps, state):
    u = u_ref[0]
    rows, width = u.shape
    s = jnp.dot(u, e_ref[0], preferred_element_type=F32)
    kidx = lax.broadcasted_iota(jnp.int32, (rows, 1), 0) & (kchunks - 1)
    s_prev = []
    for g in range(s.shape[1] // LANES):
        lanes = slice(g * LANES, (g + 1) * LANES)
        sg = s[:, lanes]
        for i in range(nsteps):
            d = 1 << i
            x = jnp.where(kidx >= d, pltpu.roll(sg, d, axis=0), 0.0)
            sg = sg + p_ref[0, i:i + 1, lanes] * x + q_ref[0, i:i + 1, lanes] * pltpu.roll(x, state, axis=1)
        s_prev.append(jnp.where(kidx >= 1, pltpu.roll(sg, 1, axis=0), 0.0).astype(BF16))
    s_prev = jnp.concatenate(s_prev, axis=1)
    cw = width // SSM_COL_SPLIT
    for c in range(SSM_COL_SPLIT):
        cols = slice(c * cw, (c + 1) * cw)
        kk = (c + 1) * cw
        y = (jnp.dot(u[:, 0:kk], m_ref[0, 0:kk, cols], preferred_element_type=F32)
             + _dot_nt(s_prev, ft_ref[0, cols, :]))
        o_ref[0, :, cols] = jax.nn.gelu(y).astype(o_ref.dtype)


def _ssm(u4, m, e, ft, p, q, *, kchunks, nsteps, rt):
    nb, rows, width = u4.shape
    sl = e.shape[2]
    blocks = 2 * rt * width * 2 + width * width * 2 + 2 * width * sl * 2 + 2 * 8 * sl * 4
    return pl.pallas_call(
        functools.partial(_ssm_kernel, kchunks=kchunks, nsteps=nsteps, state=LANES // 2),
        out_shape=jax.ShapeDtypeStruct((nb, rows, width), BF16),
        grid=(nb, rows // rt),
        in_specs=[pl.BlockSpec((1, rt, width), lambda b, r: (b, r, 0)),
                  pl.BlockSpec((1, width, width), lambda b, r: (b, 0, 0)),
                  pl.BlockSpec((1, width, sl), lambda b, r: (b, 0, 0)),
                  pl.BlockSpec((1, width, sl), lambda b, r: (b, 0, 0)),
                  pl.BlockSpec((1, 8, sl), lambda b, r: (b, 0, 0)),
                  pl.BlockSpec((1, 8, sl), lambda b, r: (b, 0, 0))],
        out_specs=pl.BlockSpec((1, rt, width), lambda b, r: (b, r, 0)),
        compiler_params=_params(("parallel", "arbitrary"), blocks,
                                4 * rt * sl * 4 + 4 * rt * (width // SSM_COL_SPLIT) * 4),
        name="ssm",
    )(u4, m, e, ft, p, q)


def _bias_kernel(tab_ref, idx_ref, o_ref, *, n_buckets, far_bucket):
    h = pl.program_id(0)
    far = tab_ref[far_bucket, h]
    for t in range(idx_ref.shape[0]):
        idx = idx_ref[t]
        acc = jnp.full(idx.shape, MASK_VALUE, F32)
        for b in range(n_buckets):
            acc = jnp.where(idx == b, (tab_ref[b, h] - far) * LOG2_E, acc)
        o_ref[0, t] = acc


def _t5_bucket_np(rel, n_buckets):
    nb = n_buckets // 2
    ret = np.where(rel > 0, nb, 0)
    n = np.abs(rel)
    max_exact = nb // 2
    nf = np.maximum(n, 1).astype(np.float32)
    large = max_exact + (np.log(nf / np.float32(max_exact)) / np.float32(math.log(MAX_DISTANCE / max_exact))
                         * np.float32(nb - max_exact)).astype(np.int32)
    large = np.minimum(large, nb - 1)
    return (ret + np.where(n < max_exact, n, large)).astype(np.int32)


def _bias_tiles(rel_bias, *, tq):
    n_buckets, n_heads = rel_bias.shape
    c = np.arange(tq)[:, None]
    r = np.arange(tq)[None, :]
    prev = _t5_bucket_np(c - r - tq, n_buckets)
    diag = np.where(c // CHUNK <= r // CHUNK, _t5_bucket_np(c - r, n_buckets), -1)
    idx = np.stack([prev, diag])
    far_bucket = int(_t5_bucket_np(np.array(-MAX_DISTANCE), n_buckets))
    assert tq >= MAX_DISTANCE and (prev[0, :] == far_bucket).all()
    return pl.pallas_call(
        functools.partial(_bias_kernel, n_buckets=n_buckets, far_bucket=far_bucket),
        out_shape=jax.ShapeDtypeStruct((n_heads, 2, tq, tq), F32),
        grid=(n_heads,),
        in_specs=[pl.BlockSpec(memory_space=pltpu.SMEM),
                  pl.BlockSpec((2, tq, tq), lambda h: (0, 0, 0))],
        out_specs=pl.BlockSpec((1, 2, tq, tq), lambda h: (h, 0, 0, 0)),
        compiler_params=pltpu.CompilerParams(dimension_semantics=("parallel",)),
        name="attn_bias",
    )(rel_bias, jnp.asarray(idx))


ONES_ROWS = 16


def _attn_kernel(q_ref, k_ref, v_ref, bias_ref, lq1_ref, lk1_ref, lq2_ref, lk2_ref, sw_ref,
                 o_ref, vt_scr, qm_scr, st_a, st_b, m_scr, acc_scr, *, tq, head_dim, lambda_init):
    seq, vd = v_ref.shape
    nq = seq // tq
    lo = lax.broadcasted_iota(jnp.int32, (1, LANES), 1) < head_dim
    for c in range(nq):
        rows = slice(c * tq, (c + 1) * tq)
        vt_scr[0:vd, rows] = v_ref[rows, :].astype(F32).T.astype(BF16)
        q = q_ref[rows, :]
        zero = jnp.zeros_like(q)
        qm_scr[0, rows, :] = jnp.where(lo, q, zero)
        qm_scr[1, rows, :] = jnp.where(lo, zero, q)
    vt_scr[vd:, :] = jnp.ones((ONES_ROWS, seq), BF16)
    m_scr[...] = jnp.full(m_scr.shape, MASK_VALUE, F32)
    acc_scr[...] = jnp.zeros(acc_scr.shape, F32)

    s1 = jnp.sum(lq1_ref[...] * lk1_ref[...], axis=-1, keepdims=True)
    s2 = jnp.sum(lq2_ref[...] * lk2_ref[...], axis=-1, keepdims=True)
    lam = jnp.exp(s1) - jnp.exp(s2) + lambda_init

    def block(j):
        return slice(j * tq, (j + 1) * tq)

    def qk(i, g, st_ref):
        kb = k_ref[block(g), :]
        for m in range(2):
            st_ref[m] = _dot_nt(kb, qm_scr[m, block(i), :])

    def softmax_pv(i, g, st_ref, bias):
        vt = vt_scr[:, block(g)]
        for m in range(2):
            st = st_ref[m]
            if bias is not None:
                st = st + bias
            m_prev = m_scr[i, m]
            m_new = jnp.maximum(m_prev, jnp.max(st, axis=0, keepdims=True))
            p = jnp.exp2(st - m_new).astype(BF16)
            alpha = jnp.exp2(m_prev - m_new)
            acc_scr[i, m] = alpha * acc_scr[i, m] + jnp.dot(vt, p, preferred_element_type=F32)
            m_scr[i, m] = m_new

    def finalize(i):
        a1 = acc_scr[i, 0]
        a2 = acc_scr[i, 1]
        ot = a1[0:vd] * (1.0 / a1[vd:vd + 1]) - lam * (a2[0:vd] * (1.0 / a2[vd:vd + 1]))
        y = _rms_norm(ot.T, sw_ref[...], SUBLN_EPS) * (1.0 - lambda_init)
        o_ref[block(i), :] = y.astype(o_ref.dtype)

    work = [(i, g) for i in range(nq) for g in range(i + 1)]
    bufs = (st_a, st_b)
    qk(*work[0], bufs[0])
    for t, (i, g) in enumerate(work):
        if t + 1 < len(work):
            qk(*work[t + 1], bufs[(t + 1) % 2])
        bias = bias_ref[0, 1] if g == i else bias_ref[0, 0] if g == i - 1 else None
        softmax_pv(i, g, bufs[t % 2], bias)
        if g == i:
            finalize(i)


def _attention(z, bias, lq1, lk1, lq2, lk2, sw, *, batch, seq, n_heads, head_dim, tq,
               q_col, k_col, v_col, lambda_init):
    n = z.shape[0]
    nq = seq // tq
    acc_rows = LANES + ONES_ROWS
    vec = pl.BlockSpec((1, head_dim), lambda b, h: (0, 0))
    blocks = 4 * seq * LANES * 2 + 2 * tq * tq * 4
    scratch = (acc_rows * seq * 2 + 2 * seq * LANES * 2 + 4 * tq * tq * 4
               + nq * 2 * (acc_rows + 8) * tq * 4)
    return pl.pallas_call(
        functools.partial(_attn_kernel, tq=tq, head_dim=head_dim, lambda_init=lambda_init),
        out_shape=jax.ShapeDtypeStruct((n, n_heads * LANES), BF16),
        grid=(batch, n_heads),
        in_specs=[
            pl.BlockSpec((seq, LANES), lambda b, h: (b, q_col + h)),
            pl.BlockSpec((seq, LANES), lambda b, h: (b, k_col + h)),
            pl.BlockSpec((seq, LANES), lambda b, h: (b, v_col + h)),
            pl.BlockSpec((1, 2, tq, tq), lambda b, h: (h, 0, 0, 0)),
            vec, vec, vec, vec,
            pl.BlockSpec((1, LANES), lambda b, h: (0, 0)),
        ],
        out_specs=pl.BlockSpec((seq, LANES), lambda b, h: (b, h)),
        scratch_shapes=[pltpu.VMEM((acc_rows, seq), BF16),
                        pltpu.VMEM((2, seq, LANES), BF16),
                        pltpu.VMEM((2, tq, tq), F32),
                        pltpu.VMEM((2, tq, tq), F32),
                        pltpu.VMEM((nq, 2, 1, tq), F32),
                        pltpu.VMEM((nq, 2, acc_rows, tq), F32)],
        compiler_params=_params(("parallel", "parallel"), blocks, scratch + 4 * tq * tq * 4),
        name="attention",
    )(z, z, z, bias, lq1, lk1, lq2, lk2, sw)


def _merge_kernel(yg_ref, wglu_ref, bglu_ref, ya_ref, gs_ref, ga_ref, ps_ref, pa_ref, wo_ref,
                  x_ref, o_ref, ys_ref, m_ref):
    j = pl.program_id(1)
    nblk = yg_ref.shape[0]
    per_chunk = PROJ_CHUNK // LANES

    @pl.when(j == 0)
    def _():
        yg = jnp.concatenate([yg_ref[b] for b in range(nblk)], axis=1)
        for c in range(nblk // per_chunk):
            cols = slice(c * PROJ_CHUNK, (c + 1) * PROJ_CHUNK)
            g = jnp.dot(yg, wglu_ref[:, cols], preferred_element_type=F32) + bglu_ref[:, cols]
            ys_ref[:, cols] = (yg[:, cols].astype(F32) * jax.nn.sigmoid(g)).astype(BF16)

    ys = ys_ref[...]
    ya = ya_ref[...]
    for c in range(m_ref.shape[1] // PROJ_CHUNK):
        cols = slice(c * PROJ_CHUNK, (c + 1) * PROJ_CHUNK)
        m = (gs_ref[:, cols].astype(F32) * jnp.dot(ys, ps_ref[:, cols], preferred_element_type=F32)
             + ga_ref[:, cols].astype(F32) * jnp.dot(ya, pa_ref[:, cols], preferred_element_type=F32))
        m_ref[:, cols] = m.astype(BF16)
    upd = jnp.dot(m_ref[...], wo_ref[...], preferred_element_type=F32)

    @pl.when(j == 0)
    def _():
        o_ref[...] = x_ref[...] + upd

    @pl.when(j > 0)
    def _():
        o_ref[...] += upd


def _merge(yg, wglu, bglu, ya, z, ps, pa, wo, x2, *, gs_col, ga_col, tm, tj):
    n, d = x2.shape
    ws = yg.shape[0] * LANES
    wa = ya.shape[1]
    blocks = (tm * ws * 2 + ws * ws * 2 + ws * 4 + tm * wa * 2 + 2 * tm * tj * 2
              + ws * tj * 2 + wa * tj * 2 + tj * d * 2 + 2 * tm * d * 4)
    return pl.pallas_call(
        _merge_kernel,
        out_shape=jax.ShapeDtypeStruct((n, d), F32),
        grid=(n // tm, d // tj),
        in_specs=[
            pl.BlockSpec((ws // LANES, tm, LANES), lambda i, j: (0, i, 0)),
            pl.BlockSpec((ws, ws), lambda i, j: (0, 0)),
            pl.BlockSpec((1, ws), lambda i, j: (0, 0)),
            pl.BlockSpec((tm, wa), lambda i, j: (i, 0)),
            pl.BlockSpec((tm, tj), lambda i, j: (i, gs_col + j)),
            pl.BlockSpec((tm, tj), lambda i, j: (i, ga_col + j)),
            pl.BlockSpec((ws, tj), lambda i, j: (0, j)),
            pl.BlockSpec((wa, tj), lambda i, j: (0, j)),
            pl.BlockSpec((tj, d), lambda i, j: (j, 0)),
            pl.BlockSpec((tm, d), lambda i, j: (i, 0)),
        ],
        out_specs=pl.BlockSpec((tm, d), lambda i, j: (i, 0)),
        scratch_shapes=[pltpu.VMEM((tm, ws), BF16), pltpu.VMEM((tm, tj), BF16)],
        compiler_params=_params(("parallel", "arbitrary"), blocks,
                                tm * ws * 2 + tm * tj * 2 + 6 * tm * PROJ_CHUNK * 4 + 2 * tm * d * 4),
        name="merge",
    )(yg, wglu, bglu, ya, z, z, ps, pa, wo, x2)


def _ffn_kernel(x_ref, nw_ref, wg_ref, wu_ref, wd_ref, o_ref, h_ref, a_ref):
    j = pl.program_id(1)

    @pl.when(j == 0)
    def _():
        x = x_ref[...]
        h_ref[...] = _rms_norm(x, nw_ref[...], RMS_EPS).astype(BF16)
        o_ref[...] = x

    h = h_ref[...]
    for c in range(a_ref.shape[1] // PROJ_CHUNK):
        cols = slice(c * PROJ_CHUNK, (c + 1) * PROJ_CHUNK)
        g = jnp.dot(h, wg_ref[:, cols], preferred_element_type=F32)
        u = jnp.dot(h, wu_ref[:, cols], preferred_element_type=F32)
        a_ref[:, cols] = (jax.nn.silu(g) * u).astype(BF16)
    o_ref[...] += jnp.dot(a_ref[...], wd_ref[...], preferred_element_type=F32)


def _ffn(x2, nw, wg, wu, wd, *, tm, tf):
    n, d = x2.shape
    dff = wg.shape[1]
    blocks = 2 * tm * d * 4 + d * 4 + 3 * d * tf * 2
    return pl.pallas_call(
        _ffn_kernel,
        out_shape=jax.ShapeDtypeStruct((n, d), F32),
        grid=(n // tm, dff // tf),
        in_specs=[
            pl.BlockSpec((tm, d), lambda i, j: (i, 0)),
            pl.BlockSpec((1, d), lambda i, j: (0, 0)),
            pl.BlockSpec((d, tf), lambda i, j: (0, j)),
            pl.BlockSpec((d, tf), lambda i, j: (0, j)),
            pl.BlockSpec((tf, d), lambda i, j: (j, 0)),
        ],
        out_specs=pl.BlockSpec((tm, d), lambda i, j: (i, 0)),
        scratch_shapes=[pltpu.VMEM((tm, d), BF16), pltpu.VMEM((tm, tf), BF16)],
        compiler_params=_params(("parallel", "arbitrary"), blocks,
                                tm * d * 2 + tm * tf * 2 + 6 * tm * PROJ_CHUNK * 4 + 2 * tm * d * 4),
        name="ffn",
    )(x2, nw, wg, wu, wd)


def _largest_tile(total, target, quantum):
    t = min(total, target)
    while total % t or t % quantum:
        t -= quantum
    return t


def kernel(x, norm1_w, w_in, lam_re, lam_im, log_step, ssm_b_re, ssm_b_im, ssm_c_re, ssm_c_im, ssm_d, w_glu, b_glu, q_norm_w, k_norm_w, lambda_q1, lambda_k1, lambda_q2, lambda_k2, subln_w, w_proj_ssm, w_proj_attn, w_out, rel_bias, norm2_w, w_ffn_gate, w_ffn_up, w_ffn_down):
    batch, seq, d_model = x.shape
    depth = w_in.shape[0]
    n = batch * seq
    _, groups, state = lam_re.shape
    group = ssm_b_re.shape[-1]
    ssm_width = groups * group
    head_dim = q_norm_w.shape[-1]
    v_dim = subln_w.shape[-1]
    attn_width = w_proj_attn.shape[1]
    n_heads = attn_width // v_dim
    d_ff = w_ffn_gate.shape[-1]

    tn = ssm_width
    assert ssm_width == attn_width and d_model % tn == 0
    assert w_in.shape[2] == ssm_width + 3 * attn_width + 2 * d_model
    assert 2 * head_dim == LANES and v_dim == LANES and 2 * state == LANES
    assert SSM_T * group == V7X_MXU_DIM
    kchunks = seq // SSM_T
    nsteps = kchunks.bit_length() - 1
    assert seq % SSM_T == 0 and kchunks == 1 << nsteps and nsteps <= 8

    tm = _largest_tile(n, 512, 8)
    tq = _largest_tile(seq, 512, MAX_DISTANCE)
    assert tq % CHUNK == 0
    tj = _largest_tile(d_model, 1024, PROJ_CHUNK)
    tf = _largest_tile(d_ff, 512, LANES)

    q_col = 0
    k_col = q_col + attn_width // LANES
    v_col = k_col + attn_width // LANES
    gs_col = 3 * attn_width // tj
    ga_col = gs_col + d_model // tj

    bias = _bias_tiles(rel_bias, tq=tq)
    x2 = x.reshape(n, d_model)
    rows = n // SSM_T
    nblk = ssm_width // LANES
    seqs_per_tile = math.gcd(batch, max(1, SSM_ROWS // kchunks))
    rt = kchunks * seqs_per_tile
    dup = lambda a: jnp.concatenate([a, a], axis=-1)
    per_row = lambda a: jnp.repeat(a, group, axis=0).reshape(nblk, LANES, LANES)
    per_lane = lambda a: a.reshape(nblk, 1, -1)
    as_rows = lambda a: a.reshape(nblk, LANES, LANES)

    for l in range(depth):
        lambda_init = 0.8 - 0.6 * math.exp(-0.3 * l)

        qw2 = jnp.tile(q_norm_w[l], LANES // head_dim)[None, :]
        kw2 = jnp.tile(k_norm_w[l], LANES // head_dim)[None, :]
        u3, z = _in_proj(x2, norm1_w[l][None, :], w_in[l].astype(BF16), qw2, kw2,
                         head_dim=head_dim, tm=tm, tn=tn)

        lam_re2, lam_im2 = dup(lam_re[l]), dup(lam_im[l])
        step2 = jnp.broadcast_to(log_step[l][:, None], (groups, 2 * state))
        b_re_t = jnp.swapaxes(ssm_b_re[l], 1, 2)
        b_im_t = jnp.swapaxes(ssm_b_im[l], 1, 2)
        c_re, c_im = ssm_c_re[l], ssm_c_im[l]
        d_rows = jnp.broadcast_to(ssm_d[l][:, None], (ssm_width, LANES)).reshape(nblk, LANES, LANES)
        m_op, e_op, ft_op, p_op, q_op = _ssm_prep(
            (per_row(lam_re2), per_row(lam_im2), per_row(step2),
             as_rows(jnp.concatenate([b_re_t, b_im_t], axis=-1)),
             as_rows(jnp.concatenate([b_im_t, b_re_t], axis=-1)),
             as_rows(jnp.concatenate([c_re, -c_im], axis=-1)),
             as_rows(jnp.concatenate([-c_im, -c_re], axis=-1)), d_rows),
            (per_lane(lam_re2), per_lane(lam_im2), per_lane(step2)),
            state=state, group=group, nsteps=nsteps)
        y4 = _ssm(u3.reshape(nblk, rows, SSM_T * LANES), m_op, e_op, ft_op, p_op, q_op,
                  kchunks=kchunks, nsteps=nsteps, rt=rt)
        yg = y4.reshape(nblk, n, LANES)

        ya = _attention(z, bias, lambda_q1[l][None, :], lambda_k1[l][None, :],
                        lambda_q2[l][None, :], lambda_k2[l][None, :], subln_w[l][None, :],
                        batch=batch, seq=seq, n_heads=n_heads, head_dim=head_dim, tq=tq,
                        q_col=q_col, k_col=k_col, v_col=v_col, lambda_init=lambda_init)

        x2 = _merge(yg, w_glu[l].astype(BF16), b_glu[l][None, :], ya, z,
                    w_proj_ssm[l].astype(BF16), w_proj_attn[l].astype(BF16),
                    w_out[l].astype(BF16), x2, gs_col=gs_col, ga_col=ga_col, tm=tm, tj=tj)
        x2 = _ffn(x2, norm2_w[l][None, :], w_ffn_gate[l].astype(BF16), w_ffn_up[l].astype(BF16),
                  w_ffn_down[l].astype(BF16), tm=tm, tf=tf)

    return x2.reshape(batch, seq, d_model)
```

```python
import functools
import math

import numpy as np
import jax
import jax.numpy as jnp
from jax import lax
from jax.experimental import pallas as pl
from jax.experimental.pallas import tpu as pltpu

F32 = jnp.float32
BF16 = jnp.bfloat16

CHUNK = 64
MAX_DISTANCE = 128
RMS_EPS = 1e-6
SUBLN_EPS = 1e-5
LAM_RE_MAX = -1e-4
MASK_VALUE = -1e30
LOG2_E = math.log2(math.e)

LANES = 128
V7X_MXU_DIM = 256
V7X_VMEM_BYTES = 64 * 1024 * 1024

PROJ_CHUNK = V7X_MXU_DIM
DOWN_SPLIT = 4
SSM_T = 16
SSM_ROWS = 256


def _vmem_limit(block_bytes, extra_bytes):
    return int(min(2 * block_bytes + extra_bytes + (4 << 20), V7X_VMEM_BYTES - (6 << 20)))


def _params(semantics, block_bytes, extra_bytes):
    return pltpu.CompilerParams(dimension_semantics=semantics,
                                vmem_limit_bytes=_vmem_limit(block_bytes, extra_bytes))


def _rms_norm(x, w, eps):
    ms = jnp.mean(x * x, axis=-1, keepdims=True)
    return x * lax.rsqrt(ms + eps) * w


def _dot_nt(a, b, **kw):
    return lax.dot_general(a, b, (((1,), (1,)), ((), ())), preferred_element_type=F32, **kw)


def _in_proj_kernel(x_ref, nw_ref, w_ref, qw_ref, kw_ref, u_ref, o_ref, h_ref, us_ref, *, head_dim):
    j = pl.program_id(1)

    @pl.when(j == 0)
    def _():
        h_ref[...] = _rms_norm(x_ref[...], nw_ref[...], RMS_EPS).astype(BF16)

    tn = w_ref.shape[1]

    def project(epilogue):
        h = h_ref[...]
        for c in range(tn // PROJ_CHUNK):
            zc = jnp.dot(h, w_ref[:, c * PROJ_CHUNK:(c + 1) * PROJ_CHUNK], preferred_element_type=F32)
            for s in range(PROJ_CHUNK // LANES):
                epilogue(c * (PROJ_CHUNK // LANES) + s, zc[:, s * LANES:(s + 1) * LANES])

    def store_z(b, y):
        o_ref[:, b * LANES:(b + 1) * LANES] = y.astype(o_ref.dtype)

    def seg_norm(w_row, scale):
        lo = lax.broadcasted_iota(jnp.int32, (1, LANES), 1) < head_dim

        def epilogue(b, zb):
            sq = zb * zb
            s_lo = jnp.sum(jnp.where(lo, sq, 0.0), axis=-1, keepdims=True)
            s_hi = jnp.sum(jnp.where(lo, 0.0, sq), axis=-1, keepdims=True)
            ms = jnp.where(lo, s_lo, s_hi) * (1.0 / head_dim)
            store_z(b, zb * lax.rsqrt(ms + RMS_EPS) * w_row * scale)
        return epilogue

    def store_u(b, zb):
        us_ref[b] = zb
        chunks = zb.shape[0] // SSM_T
        for t in range(SSM_T):
            u_ref[b, :, t * LANES:(t + 1) * LANES] = (
                us_ref[b, pl.ds(t, chunks, stride=SSM_T), :].astype(u_ref.dtype))

    pl.when(j == 0)(lambda: project(store_u))
    pl.when(j == 1)(lambda: project(seg_norm(qw_ref[...], head_dim ** -0.5 * LOG2_E)))
    pl.when(j == 2)(lambda: project(seg_norm(kw_ref[...], 1.0)))
    pl.when(j == 3)(lambda: project(store_z))
    pl.when(j >= 4)(lambda: project(lambda b, zb: store_z(b, jax.nn.sigmoid(zb))))


def _in_proj(x2, nw, w, qw2, kw2, *, head_dim, tm, tn):
    n, d = x2.shape
    width = w.shape[1]
    blocks = tm * d * 4 + d * 4 + d * tn * 2 + 2 * tm * tn * 2
    return pl.pallas_call(
        functools.partial(_in_proj_kernel, head_dim=head_dim),
        out_shape=(jax.ShapeDtypeStruct((tn // LANES, n // SSM_T, SSM_T * LANES), BF16),
                   jax.ShapeDtypeStruct((n, width - tn), BF16)),
        grid=(n // tm, width // tn),
        in_specs=[
            pl.BlockSpec((tm, d), lambda i, j: (i, 0)),
            pl.BlockSpec((1, d), lambda i, j: (0, 0)),
            pl.BlockSpec((d, tn), lambda i, j: (0, j)),
            pl.BlockSpec((1, LANES), lambda i, j: (0, 0)),
            pl.BlockSpec((1, LANES), lambda i, j: (0, 0)),
        ],
        out_specs=(pl.BlockSpec((tn // LANES, tm // SSM_T, SSM_T * LANES), lambda i, j: (0, i, 0)),
                   pl.BlockSpec((tm, tn), lambda i, j: (i, jnp.maximum(j - 1, 0)))),
        scratch_shapes=[pltpu.VMEM((tm, d), BF16), pltpu.VMEM((tn // LANES, tm, LANES), F32)],
        compiler_params=_params(("parallel", "arbitrary"), blocks,
                                tm * d * 2 + tm * tn * 4 + 6 * tm * PROJ_CHUNK * 4),
        name="in_proj",
    )(x2, nw, w, qw2, kw2)


def _ssm_prep_kernel(lr_ref, li_ref, ls_ref, bcat_ref, bswap_ref, ccat_ref, cswap_ref, d_ref,
                     lrl_ref, lil_ref, lsl_ref, m_ref, e_ref, ft_ref, p_ref, q_ref, ft0_scr,
                     *, state, group, nsteps):
    gl = LANES // group
    lane = lax.broadcasted_iota(jnp.int32, (1, LANES), 1)
    sign = jnp.where(lane < state, -1.0, 1.0).astype(F32)
    lr = jnp.minimum(lr_ref[0], LAM_RE_MAX)
    li = li_ref[0]
    dt = jnp.exp(ls_ref[0])
    mag = jnp.exp(lr * dt)
    ab_r = mag * jnp.cos(li * dt)
    ab_i = mag * jnp.sin(li * dt)
    den = lr * lr + li * li
    nr = ab_r - 1.0
    fr = (nr * lr + ab_i * li) / den
    fi = (ab_i * lr - nr * li) / den
    bcat = bcat_ref[0]
    bswap = bswap_ref[0]
    bb_cat = fr * bcat + sign * fi * bswap
    bb_swap = fr * bswap - sign * fi * bcat
    ccat = ccat_ref[0]
    cswap = cswap_ref[0]

    row_group = lax.broadcasted_iota(jnp.int32, (LANES, 1), 0) // group

    def store_block_diag(ref, block, value):
        for g in range(gl):
            ref[0, block * LANES:(block + 1) * LANES, g * LANES:(g + 1) * LANES] = (
                jnp.where(row_group == g, value, 0.0).astype(ref.dtype))

    pw_r = jnp.ones((LANES, LANES), F32)
    pw_i = jnp.zeros((LANES, LANES), F32)
    for n in range(SSM_T + 1):
        ft_blk = pw_r * ccat + pw_i * cswap
        if n < SSM_T:
            ft0_scr[n * LANES:(n + 1) * LANES, :] = ft_blk
            store_block_diag(e_ref, SSM_T - 1 - n, pw_r * bb_cat + sign * pw_i * bb_swap)
        if n >= 1:
            store_block_diag(ft_ref, n - 1, ft_blk)
        pw_r, pw_i = pw_r * ab_r - pw_i * ab_i, pw_r * ab_i + pw_i * ab_r

    width = SSM_T * LANES
    r = _dot_nt(bb_cat, ft0_scr[...], precision=lax.Precision.HIGHEST)
    col = lax.broadcasted_iota(jnp.int32, (LANES, width), 1)
    row = lax.broadcasted_iota(jnp.int32, (LANES, width), 0)
    r = jnp.where(row // group == (col % LANES) // group, r, 0.0)
    r = r + jnp.where(col == row, jnp.tile(d_ref[0], (1, SSM_T)), 0.0)
    for j in range(SSM_T):
        rows_j = slice(j * LANES, (j + 1) * LANES)
        if j:
            m_ref[0, rows_j, 0:j * LANES] = jnp.zeros((LANES, j * LANES), m_ref.dtype)
        m_ref[0, rows_j, j * LANES:width] = r[:, 0:width - j * LANES].astype(m_ref.dtype)

    lrl = jnp.minimum(lrl_ref[0], LAM_RE_MAX)
    dtl = jnp.exp(lsl_ref[0])
    lane_l = lax.broadcasted_iota(jnp.int32, lrl.shape, 1)
    sign_l = jnp.where(lane_l % LANES < state, -1.0, 1.0).astype(F32)
    steps = lax.broadcasted_iota(jnp.int32, (8, 1), 0)
    nd = (SSM_T * jnp.left_shift(1, jnp.minimum(steps, nsteps))).astype(F32)
    mag_l = jnp.exp(lrl * dtl * nd)
    ang_l = lil_ref[0] * dtl * nd
    p_ref[0] = mag_l * jnp.cos(ang_l)
    q_ref[0] = sign_l * mag_l * jnp.sin(ang_l)


def _ssm_prep(rows_in, lanes_in, *, state, group, nsteps):
    nb = rows_in[0].shape[0]
    gl = LANES // group
    width = SSM_T * LANES
    bspec = lambda *shape: pl.BlockSpec((1,) + shape, lambda i: (i, 0, 0))
    blocks = width * width * 2 + 2 * width * gl * LANES * 2 + 8 * LANES * LANES * 4
    return pl.pallas_call(
        functools.partial(_ssm_prep_kernel, state=state, group=group, nsteps=nsteps),
        out_shape=(
            jax.ShapeDtypeStruct((nb, width, width), BF16),
            jax.ShapeDtypeStruct((nb, width, gl * LANES), BF16),
            jax.ShapeDtypeStruct((nb, width, gl * LANES), BF16),
            jax.ShapeDtypeStruct((nb, 8, gl * LANES), F32),
            jax.ShapeDtypeStruct((nb, 8, gl * LANES), F32),
        ),
        grid=(nb,),
        in_specs=[bspec(LANES, LANES)] * 8 + [bspec(1, gl * LANES)] * 3,
        out_specs=(bspec(width, width), bspec(width, gl * LANES), bspec(width, gl * LANES),
                   bspec(8, gl * LANES), bspec(8, gl * LANES)),
        scratch_shapes=[pltpu.VMEM((width, LANES), F32)],
        compiler_params=_params(("parallel",), blocks, width * LANES * 4 + 6 * LANES * width * 4),
        name="ssm_prep",
    )(*rows_in, *lanes_in)


SSM_COL_SPLIT = 4


def _ssm_kernel(u_ref, m_ref, e_ref, ft_ref, p_ref, q_ref, o_ref, ys_ref, *, kchunks, nsteps, state):
    u = u_ref[0]
    rows, width = u.shape
    s = jnp.dot(u, e_ref[0], preferred_element_type=F32)
    kidx = lax.broadcasted_iota(jnp.int32, (rows, 1), 0) & (kchunks - 1)
    s_prev = []
    for g in range(s.shape[1] // LANES):
        lanes = slice(g * LANES, (g + 1) * LANES)
        sg = s[:, lanes]
        for i in range(nsteps):
            d = 1 << i
            x = jnp.where(kidx >= d, pltpu.roll(sg, d, axis=0), 0.0)
            sg = sg + p_ref[0, i:i + 1, lanes] * x + q_ref[0, i:i + 1, lanes] * pltpu.roll(x, state, axis=1)
        s_prev.append(jnp.where(kidx >= 1, pltpu.roll(sg, 1, axis=0), 0.0).astype(BF16))
    s_prev = jnp.concatenate(s_prev, axis=1)
    cw = width // SSM_COL_SPLIT
    for c in range(SSM_COL_SPLIT):
        cols = slice(c * cw, (c + 1) * cw)
        kk = (c + 1) * cw
        y = jax.nn.gelu(jnp.dot(u[:, 0:kk], m_ref[0, 0:kk, cols], preferred_element_type=F32)
                        + _dot_nt(s_prev, ft_ref[0, cols, :]))
        for tl in range(cw // LANES):
            t = c * (cw // LANES) + tl
            ys_ref[pl.ds(t, rows, stride=SSM_T), :] = y[:, tl * LANES:(tl + 1) * LANES]
    o_ref[0] = ys_ref[...].astype(o_ref.dtype)


def _ssm(u4, m, e, ft, p, q, *, kchunks, nsteps, rt):
    nb, rows, width = u4.shape
    sl = e.shape[2]
    blocks = 2 * rt * width * 2 + width * width * 2 + 2 * width * sl * 2 + 2 * 8 * sl * 4
    return pl.pallas_call(
        functools.partial(_ssm_kernel, kchunks=kchunks, nsteps=nsteps, state=LANES // 2),
        out_shape=jax.ShapeDtypeStruct((nb, rows * SSM_T, LANES), BF16),
        grid=(nb, rows // rt),
        in_specs=[pl.BlockSpec((1, rt, width), lambda b, r: (b, r, 0)),
                  pl.BlockSpec((1, width, width), lambda b, r: (b, 0, 0)),
                  pl.BlockSpec((1, width, sl), lambda b, r: (b, 0, 0)),
                  pl.BlockSpec((1, width, sl), lambda b, r: (b, 0, 0)),
                  pl.BlockSpec((1, 8, sl), lambda b, r: (b, 0, 0)),
                  pl.BlockSpec((1, 8, sl), lambda b, r: (b, 0, 0))],
        out_specs=pl.BlockSpec((1, rt * SSM_T, LANES), lambda b, r: (b, r, 0)),
        scratch_shapes=[pltpu.VMEM((rt * SSM_T, LANES), F32)],
        compiler_params=_params(("parallel", "arbitrary"), blocks,
                                rt * width * 4 + 4 * rt * sl * 4 + 4 * rt * (width // SSM_COL_SPLIT) * 4),
        name="ssm",
    )(u4, m, e, ft, p, q)


def _bias_kernel(tab_ref, idx_ref, o_ref, *, n_buckets, far_bucket):
    h = pl.program_id(0)
    far = tab_ref[far_bucket, h]
    for t in range(idx_ref.shape[0]):
        idx = idx_ref[t]
        acc = jnp.full(idx.shape, MASK_VALUE, F32)
        for b in range(n_buckets):
            acc = jnp.where(idx == b, (tab_ref[b, h] - far) * LOG2_E, acc)
        o_ref[0, t] = acc


def _t5_bucket_np(rel, n_buckets):
    nb = n_buckets // 2
    ret = np.where(rel > 0, nb, 0)
    n = np.abs(rel)
    max_exact = nb // 2
    nf = np.maximum(n, 1).astype(np.float32)
    large = max_exact + (np.log(nf / np.float32(max_exact)) / np.float32(math.log(MAX_DISTANCE / max_exact))
                         * np.float32(nb - max_exact)).astype(np.int32)
    large = np.minimum(large, nb - 1)
    return (ret + np.where(n < max_exact, n, large)).astype(np.int32)


def _bias_tiles(rel_bias, *, tq):
    n_buckets, n_heads = rel_bias.shape
    c = np.arange(tq)[:, None]
    r = np.arange(tq)[None, :]
    prev = _t5_bucket_np(c - r - tq, n_buckets)
    diag = np.where(c // CHUNK <= r // CHUNK, _t5_bucket_np(c - r, n_buckets), -1)
    idx = np.stack([prev, diag])
    far_bucket = int(_t5_bucket_np(np.array(-MAX_DISTANCE), n_buckets))
    assert tq >= MAX_DISTANCE and (prev[0, :] == far_bucket).all()
    return pl.pallas_call(
        functools.partial(_bias_kernel, n_buckets=n_buckets, far_bucket=far_bucket),
        out_shape=jax.ShapeDtypeStruct((n_heads, 2, tq, tq), F32),
        grid=(n_heads,),
        in_specs=[pl.BlockSpec(memory_space=pltpu.SMEM),
                  pl.BlockSpec((2, tq, tq), lambda h: (0, 0, 0))],
        out_specs=pl.BlockSpec((1, 2, tq, tq), lambda h: (h, 0, 0, 0)),
        compiler_params=pltpu.CompilerParams(dimension_semantics=("parallel",)),
        name="attn_bias",
    )(rel_bias, jnp.asarray(idx))


ONES_ROWS = 16


def _attn_kernel(q_ref, k_ref, v_ref, bias_ref, lq1_ref, lk1_ref, lq2_ref, lk2_ref, sw_ref,
                 o_ref, vt_scr, qm_scr, st_a, st_b, m_scr, acc_scr, *, tq, head_dim, lambda_init):
    seq, vd = v_ref.shape
    nq = seq // tq
    lo = lax.broadcasted_iota(jnp.int32, (1, LANES), 1) < head_dim
    for c in range(nq):
        rows = slice(c * tq, (c + 1) * tq)
        vt_scr[0:vd, rows] = v_ref[rows, :].astype(F32).T.astype(BF16)
        q = q_ref[rows, :]
        zero = jnp.zeros_like(q)
        qm_scr[0, rows, :] = jnp.where(lo, q, zero)
        qm_scr[1, rows, :] = jnp.where(lo, zero, q)
    vt_scr[vd:, :] = jnp.ones((ONES_ROWS, seq), BF16)
    m_scr[...] = jnp.full(m_scr.shape, MASK_VALUE, F32)
    acc_scr[...] = jnp.zeros(acc_scr.shape, F32)

    s1 = jnp.sum(lq1_ref[...] * lk1_ref[...], axis=-1, keepdims=True)
    s2 = jnp.sum(lq2_ref[...] * lk2_ref[...], axis=-1, keepdims=True)
    lam = jnp.exp(s1) - jnp.exp(s2) + lambda_init

    def block(j):
        return slice(j * tq, (j + 1) * tq)

    def qk(i, g, st_ref):
        kb = k_ref[block(g), :]
        for m in range(2):
            st_ref[m] = _dot_nt(kb, qm_scr[m, block(i), :])

    def softmax_pv(i, g, st_ref, bias):
        vt = vt_scr[:, block(g)]
        for m in range(2):
            st = st_ref[m]
            if bias is not None:
                st = st + bias
            m_prev = m_scr[i, m]
            m_new = jnp.maximum(m_prev, jnp.max(st, axis=0, keepdims=True))
            p = jnp.exp2(st - m_new).astype(BF16)
            alpha = jnp.exp2(m_prev - m_new)
            acc_scr[i, m] = alpha * acc_scr[i, m] + jnp.dot(vt, p, preferred_element_type=F32)
            m_scr[i, m] = m_new

    def finalize(i):
        a1 = acc_scr[i, 0]
        a2 = acc_scr[i, 1]
        ot = a1[0:vd] * (1.0 / a1[vd:vd + 1]) - lam * (a2[0:vd] * (1.0 / a2[vd:vd + 1]))
        y = _rms_norm(ot.T, sw_ref[...], SUBLN_EPS) * (1.0 - lambda_init)
        o_ref[block(i), :] = y.astype(o_ref.dtype)

    work = [(i, g) for i in range(nq) for g in range(i + 1)]
    bufs = (st_a, st_b)
    qk(*work[0], bufs[0])
    for t, (i, g) in enumerate(work):
        if t + 1 < len(work):
            qk(*work[t + 1], bufs[(t + 1) % 2])
        bias = bias_ref[0, 1] if g == i else bias_ref[0, 0] if g == i - 1 else None
        softmax_pv(i, g, bufs[t % 2], bias)
        if g == i:
            finalize(i)


def _attention(z, bias, lq1, lk1, lq2, lk2, sw, *, batch, seq, n_heads, head_dim, tq,
               q_col, k_col, v_col, lambda_init):
    n = z.shape[0]
    nq = seq // tq
    acc_rows = LANES + ONES_ROWS
    vec = pl.BlockSpec((1, head_dim), lambda b, h: (0, 0))
    blocks = 4 * seq * LANES * 2 + 2 * tq * tq * 4
    scratch = (acc_rows * seq * 2 + 2 * seq * LANES * 2 + 4 * tq * tq * 4
               + nq * 2 * (acc_rows + 8) * tq * 4)
    return pl.pallas_call(
        functools.partial(_attn_kernel, tq=tq, head_dim=head_dim, lambda_init=lambda_init),
        out_shape=jax.ShapeDtypeStruct((n, n_heads * LANES), BF16),
        grid=(batch, n_heads),
        in_specs=[
            pl.BlockSpec((seq, LANES), lambda b, h: (b, q_col + h)),
            pl.BlockSpec((seq, LANES), lambda b, h: (b, k_col + h)),
            pl.BlockSpec((seq, LANES), lambda b, h: (b, v_col + h)),
            pl.BlockSpec((1, 2, tq, tq), lambda b, h: (h, 0, 0, 0)),
            vec, vec, vec, vec,
            pl.BlockSpec((1, LANES), lambda b, h: (0, 0)),
        ],
        out_specs=pl.BlockSpec((seq, LANES), lambda b, h: (b, h)),
        scratch_shapes=[pltpu.VMEM((acc_rows, seq), BF16),
                        pltpu.VMEM((2, seq, LANES), BF16),
                        pltpu.VMEM((2, tq, tq), F32),
                        pltpu.VMEM((2, tq, tq), F32),
                        pltpu.VMEM((nq, 2, 1, tq), F32),
                        pltpu.VMEM((nq, 2, acc_rows, tq), F32)],
        compiler_params=_params(("parallel", "parallel"), blocks, scratch + 4 * tq * tq * 4),
        name="attention",
    )(z, z, z, bias, lq1, lk1, lq2, lk2, sw)


def _merge_kernel(yg_ref, wglu_ref, bglu_ref, ya_ref, gs_ref, ga_ref, ps_ref, pa_ref, wo_ref,
                  x_ref, o_ref, ys_ref, m_ref):
    j = pl.program_id(1)
    nblk = yg_ref.shape[0]
    per_chunk = PROJ_CHUNK // LANES

    @pl.when(j == 0)
    def _():
        yg = jnp.concatenate([yg_ref[b] for b in range(nblk)], axis=1)
        for c in range(nblk // per_chunk):
            cols = slice(c * PROJ_CHUNK, (c + 1) * PROJ_CHUNK)
            g = jnp.dot(yg, wglu_ref[:, cols], preferred_element_type=F32) + bglu_ref[:, cols]
            ys_ref[:, cols] = (yg[:, cols].astype(F32) * jax.nn.sigmoid(g)).astype(BF16)

    ys = ys_ref[...]
    ya = ya_ref[...]
    for c in range(m_ref.shape[1] // PROJ_CHUNK):
        cols = slice(c * PROJ_CHUNK, (c + 1) * PROJ_CHUNK)
        m = (gs_ref[:, cols].astype(F32) * jnp.dot(ys, ps_ref[:, cols], preferred_element_type=F32)
             + ga_ref[:, cols].astype(F32) * jnp.dot(ya, pa_ref[:, cols], preferred_element_type=F32))
        m_ref[:, cols] = m.astype(BF16)
    upd = jnp.dot(m_ref[...], wo_ref[...], preferred_element_type=F32)

    @pl.when(j == 0)
    def _():
        o_ref[...] = x_ref[...] + upd

    @pl.when(j > 0)
    def _():
        o_ref[...] += upd


def _merge(yg, wglu, bglu, ya, z, ps, pa, wo, x2, *, gs_col, ga_col, tm, tj):
    n, d = x2.shape
    ws = yg.shape[0] * LANES
    wa = ya.shape[1]
    blocks = (tm * ws * 2 + ws * ws * 2 + ws * 4 + tm * wa * 2 + 2 * tm * tj * 2
              + ws * tj * 2 + wa * tj * 2 + tj * d * 2 + 2 * tm * d * 4)
    return pl.pallas_call(
        _merge_kernel,
        out_shape=jax.ShapeDtypeStruct((n, d), F32),
        grid=(n // tm, d // tj),
        in_specs=[
            pl.BlockSpec((ws // LANES, tm, LANES), lambda i, j: (0, i, 0)),
            pl.BlockSpec((ws, ws), lambda i, j: (0, 0)),
            pl.BlockSpec((1, ws), lambda i, j: (0, 0)),
            pl.BlockSpec((tm, wa), lambda i, j: (i, 0)),
            pl.BlockSpec((tm, tj), lambda i, j: (i, gs_col + j)),
            pl.BlockSpec((tm, tj), lambda i, j: (i, ga_col + j)),
            pl.BlockSpec((ws, tj), lambda i, j: (0, j)),
            pl.BlockSpec((wa, tj), lambda i, j: (0, j)),
            pl.BlockSpec((tj, d), lambda i, j: (j, 0)),
            pl.BlockSpec((tm, d), lambda i, j: (i, 0)),
        ],
        out_specs=pl.BlockSpec((tm, d), lambda i, j: (i, 0)),
        scratch_shapes=[pltpu.VMEM((tm, ws), BF16), pltpu.VMEM((tm, tj), BF16)],
        compiler_params=_params(("parallel", "arbitrary"), blocks,
                                tm * ws * 2 + tm * tj * 2 + 6 * tm * PROJ_CHUNK * 4 + 2 * tm * d * 4),
        name="merge",
    )(yg, wglu, bglu, ya, z, z, ps, pa, wo, x2)


def _ffn_kernel(x_ref, nw_ref, wg_ref, wu_ref, wd_ref, o_ref, h_ref, a_ref):
    j = pl.program_id(1)

    @pl.when(j == 0)
    def _():
        x = x_ref[...]
        h_ref[...] = _rms_norm(x, nw_ref[...], RMS_EPS).astype(BF16)
        o_ref[...] = x

    h = h_ref[...]
    for c in range(a_ref.shape[1] // PROJ_CHUNK):
        cols = slice(c * PROJ_CHUNK, (c + 1) * PROJ_CHUNK)
        g = jnp.dot(h, wg_ref[:, cols], preferred_element_type=F32)
        u = jnp.dot(h, wu_ref[:, cols], preferred_element_type=F32)
        a_ref[:, cols] = (jax.nn.silu(g) * u).astype(BF16)
    a = a_ref[...]
    cw = o_ref.shape[1] // DOWN_SPLIT
    for c in range(DOWN_SPLIT):
        cols = slice(c * cw, (c + 1) * cw)
        o_ref[:, cols] += jnp.dot(a, wd_ref[:, cols], preferred_element_type=F32)


def _ffn(x2, nw, wg, wu, wd, *, tm, tf):
    n, d = x2.shape
    dff = wg.shape[1]
    blocks = tm * d * 4 + tm * d * 2 + d * 4 + 3 * d * tf * 2
    return pl.pallas_call(
        _ffn_kernel,
        out_shape=jax.ShapeDtypeStruct((n, d), F32),
        grid=(n // tm, dff // tf),
        in_specs=[
            pl.BlockSpec((tm, d), lambda i, j: (i, 0), pipeline_mode=pl.Buffered(1)),
            pl.BlockSpec((1, d), lambda i, j: (0, 0)),
            pl.BlockSpec((d, tf), lambda i, j: (0, j)),
            pl.BlockSpec((d, tf), lambda i, j: (0, j)),
            pl.BlockSpec((tf, d), lambda i, j: (j, 0)),
        ],
        out_specs=pl.BlockSpec((tm, d), lambda i, j: (i, 0)),
        scratch_shapes=[pltpu.VMEM((tm, d), BF16), pltpu.VMEM((tm, tf), BF16)],
        compiler_params=_params(("parallel", "arbitrary"), blocks,
                                tm * d * 2 + tm * tf * 2 + 6 * tm * PROJ_CHUNK * 4),
        name="ffn",
    )(x2, nw, wg, wu, wd)


def _largest_tile(total, target, quantum):
    t = min(total, target)
    while total % t or t % quantum:
        t -= quantum
    return t


def kernel(x, norm1_w, w_in, lam_re, lam_im, log_step, ssm_b_re, ssm_b_im, ssm_c_re, ssm_c_im, ssm_d, w_glu, b_glu, q_norm_w, k_norm_w, lambda_q1, lambda_k1, lambda_q2, lambda_k2, subln_w, w_proj_ssm, w_proj_attn, w_out, rel_bias, norm2_w, w_ffn_gate, w_ffn_up, w_ffn_down):
    batch, seq, d_model = x.shape
    depth = w_in.shape[0]
    n = batch * seq
    _, groups, state = lam_re.shape
    group = ssm_b_re.shape[-1]
    ssm_width = groups * group
    head_dim = q_norm_w.shape[-1]
    v_dim = subln_w.shape[-1]
    attn_width = w_proj_attn.shape[1]
    n_heads = attn_width // v_dim
    d_ff = w_ffn_gate.shape[-1]

    tn = ssm_width
    assert ssm_width == attn_width and d_model % tn == 0
    assert w_in.shape[2] == ssm_width + 3 * attn_width + 2 * d_model
    assert 2 * head_dim == LANES and v_dim == LANES and 2 * state == LANES
    assert SSM_T * group == V7X_MXU_DIM
    kchunks = seq // SSM_T
    nsteps = kchunks.bit_length() - 1
    assert seq % SSM_T == 0 and kchunks == 1 << nsteps and nsteps <= 8

    tm_proj = _largest_tile(n, 1024, 16 * SSM_T)
    tm_merge = _largest_tile(n, 512, 16)
    tm_ffn = _largest_tile(n, 1024, 8)
    tq = _largest_tile(seq, 512, MAX_DISTANCE)
    assert tq % CHUNK == 0
    tj = _largest_tile(d_model, 1024, PROJ_CHUNK)
    tf = _largest_tile(d_ff, 512, LANES)

    q_col = 0
    k_col = q_col + attn_width // LANES
    v_col = k_col + attn_width // LANES
    gs_col = 3 * attn_width // tj
    ga_col = gs_col + d_model // tj

    bias = _bias_tiles(rel_bias, tq=tq)
    x2 = x.reshape(n, d_model)
    nblk = ssm_width // LANES
    seqs_per_tile = math.gcd(batch, max(1, SSM_ROWS // kchunks))
    rt = kchunks * seqs_per_tile
    dup = lambda a: jnp.concatenate([a, a], axis=-1)
    per_row = lambda a: jnp.repeat(a, group, axis=0).reshape(nblk, LANES, LANES)
    per_lane = lambda a: a.reshape(nblk, 1, -1)
    as_rows = lambda a: a.reshape(nblk, LANES, LANES)

    for l in range(depth):
        lambda_init = 0.8 - 0.6 * math.exp(-0.3 * l)

        qw2 = jnp.tile(q_norm_w[l], LANES // head_dim)[None, :]
        kw2 = jnp.tile(k_norm_w[l], LANES // head_dim)[None, :]
        u3, z = _in_proj(x2, norm1_w[l][None, :], w_in[l].astype(BF16), qw2, kw2,
                         head_dim=head_dim, tm=tm_proj, tn=tn)

        lam_re2, lam_im2 = dup(lam_re[l]), dup(lam_im[l])
        step2 = jnp.broadcast_to(log_step[l][:, None], (groups, 2 * state))
        b_re_t = jnp.swapaxes(ssm_b_re[l], 1, 2)
        b_im_t = jnp.swapaxes(ssm_b_im[l], 1, 2)
        c_re, c_im = ssm_c_re[l], ssm_c_im[l]
        d_rows = jnp.broadcast_to(ssm_d[l][:, None], (ssm_width, LANES)).reshape(nblk, LANES, LANES)
        m_op, e_op, ft_op, p_op, q_op = _ssm_prep(
            (per_row(lam_re2), per_row(lam_im2), per_row(step2),
             as_rows(jnp.concatenate([b_re_t, b_im_t], axis=-1)),
             as_rows(jnp.concatenate([b_im_t, b_re_t], axis=-1)),
             as_rows(jnp.concatenate([c_re, -c_im], axis=-1)),
             as_rows(jnp.concatenate([-c_im, -c_re], axis=-1)), d_rows),
            (per_lane(lam_re2), per_lane(lam_im2), per_lane(step2)),
            state=state, group=group, nsteps=nsteps)
        yg = _ssm(u3, m_op, e_op, ft_op, p_op, q_op, kchunks=kchunks, nsteps=nsteps, rt=rt)

        ya = _attention(z, bias, lambda_q1[l][None, :], lambda_k1[l][None, :],
                        lambda_q2[l][None, :], lambda_k2[l][None, :], subln_w[l][None, :],
                        batch=batch, seq=seq, n_heads=n_heads, head_dim=head_dim, tq=tq,
                        q_col=q_col, k_col=k_col, v_col=v_col, lambda_init=lambda_init)

        x2 = _merge(yg, w_glu[l].astype(BF16), b_glu[l][None, :], ya, z,
                    w_proj_ssm[l].astype(BF16), w_proj_attn[l].astype(BF16),
                    w_out[l].astype(BF16), x2, gs_col=gs_col, ga_col=ga_col, tm=tm_merge, tj=tj)
        x2 = _ffn(x2, norm2_w[l][None, :], w_ffn_gate[l].astype(BF16), w_ffn_up[l].astype(BF16),
                  w_ffn_down[l].astype(BF16), tm=tm_ffn, tf=tf)

    return x2.reshape(batch, seq, d_model)
```

```python
import functools
import math

import numpy as np
import jax
import jax.numpy as jnp
from jax import lax
from jax.experimental import pallas as pl
from jax.experimental.pallas import tpu as pltpu

F32 = jnp.float32
BF16 = jnp.bfloat16

CHUNK = 64
MAX_DISTANCE = 128
RMS_EPS = 1e-6
SUBLN_EPS = 1e-5
LAM_RE_MAX = -1e-4
MASK_VALUE = -1e30
LOG2_E = math.log2(math.e)

LANES = 128
V7X_MXU_DIM = 256
V7X_VMEM_BYTES = 64 * 1024 * 1024

PROJ_CHUNK = V7X_MXU_DIM
DOWN_SPLIT = 4
SSM_T = 16
SSM_ROWS = 256


def _vmem_limit(block_bytes, extra_bytes):
    return int(min(2 * block_bytes + extra_bytes + (4 << 20), V7X_VMEM_BYTES - (6 << 20)))


def _params(semantics, block_bytes, extra_bytes):
    return pltpu.CompilerParams(dimension_semantics=semantics,
                                vmem_limit_bytes=_vmem_limit(block_bytes, extra_bytes))


def _rms_norm(x, w, eps):
    ms = jnp.mean(x * x, axis=-1, keepdims=True)
    return x * lax.rsqrt(ms + eps) * w


def _dot_nt(a, b, **kw):
    return lax.dot_general(a, b, (((1,), (1,)), ((), ())), preferred_element_type=F32, **kw)


def _in_proj_kernel(x_ref, nw_ref, w_ref, qw_ref, kw_ref, u_ref, o_ref, h_ref, us_ref, *, head_dim):
    j = pl.program_id(1)

    @pl.when(j == 0)
    def _():
        h_ref[...] = _rms_norm(x_ref[...], nw_ref[...], RMS_EPS).astype(BF16)

    tn = w_ref.shape[1]

    def project(epilogue):
        h = h_ref[...]
        for c in range(tn // PROJ_CHUNK):
            zc = jnp.dot(h, w_ref[:, c * PROJ_CHUNK:(c + 1) * PROJ_CHUNK], preferred_element_type=F32)
            for s in range(PROJ_CHUNK // LANES):
                epilogue(c * (PROJ_CHUNK // LANES) + s, zc[:, s * LANES:(s + 1) * LANES])

    def store_z(b, y):
        o_ref[:, b * LANES:(b + 1) * LANES] = y.astype(o_ref.dtype)

    def seg_norm(w_row, scale):
        lo = lax.broadcasted_iota(jnp.int32, (1, LANES), 1) < head_dim

        def epilogue(b, zb):
            sq = zb * zb
            s_lo = jnp.sum(jnp.where(lo, sq, 0.0), axis=-1, keepdims=True)
            s_hi = jnp.sum(jnp.where(lo, 0.0, sq), axis=-1, keepdims=True)
            ms = jnp.where(lo, s_lo, s_hi) * (1.0 / head_dim)
            store_z(b, zb * lax.rsqrt(ms + RMS_EPS) * w_row * scale)
        return epilogue

    def store_u(b, zb):
        us_ref[b] = zb
        chunks = zb.shape[0] // SSM_T
        for t in range(SSM_T):
            u_ref[b, :, t * LANES:(t + 1) * LANES] = (
                us_ref[b, pl.ds(t, chunks, stride=SSM_T), :].astype(u_ref.dtype))

    pl.when(j == 0)(lambda: project(store_u))
    pl.when(j == 1)(lambda: project(seg_norm(qw_ref[...], head_dim ** -0.5 * LOG2_E)))
    pl.when(j == 2)(lambda: project(seg_norm(kw_ref[...], 1.0)))
    pl.when(j == 3)(lambda: project(store_z))
    pl.when(j >= 4)(lambda: project(lambda b, zb: store_z(b, jax.nn.sigmoid(zb))))


def _in_proj(x2, nw, w, qw2, kw2, *, layer, head_dim, tm, tn):
    n, d = x2.shape
    width = w.shape[2]
    blocks = tm * d * 4 + d * 4 + d * tn * 2 + 2 * tm * tn * 2
    return pl.pallas_call(
        functools.partial(_in_proj_kernel, head_dim=head_dim),
        out_shape=(jax.ShapeDtypeStruct((tn // LANES, n // SSM_T, SSM_T * LANES), BF16),
                   jax.ShapeDtypeStruct((n, width - tn), BF16)),
        grid=(n // tm, width // tn),
        in_specs=[
            pl.BlockSpec((tm, d), lambda i, j: (i, 0)),
            pl.BlockSpec((1, d), lambda i, j: (0, 0)),
            pl.BlockSpec((None, d, tn), lambda i, j: (layer, 0, j)),
            pl.BlockSpec((1, LANES), lambda i, j: (0, 0)),
            pl.BlockSpec((1, LANES), lambda i, j: (0, 0)),
        ],
        out_specs=(pl.BlockSpec((tn // LANES, tm // SSM_T, SSM_T * LANES), lambda i, j: (0, i, 0)),
                   pl.BlockSpec((tm, tn), lambda i, j: (i, jnp.maximum(j - 1, 0)))),
        scratch_shapes=[pltpu.VMEM((tm, d), BF16), pltpu.VMEM((tn // LANES, tm, LANES), F32)],
        compiler_params=_params(("parallel", "arbitrary"), blocks,
                                tm * d * 2 + tm * tn * 4 + 6 * tm * PROJ_CHUNK * 4),
        name="in_proj",
    )(x2, nw, w, qw2, kw2)


def _ssm_prep_kernel(lr_ref, li_ref, ls_ref, bcat_ref, bswap_ref, ccat_ref, cswap_ref, d_ref,
                     lrl_ref, lil_ref, lsl_ref, m_ref, e_ref, ft_ref, p_ref, q_ref, ft0_scr,
                     *, state, group, nsteps):
    gl = LANES // group
    lane = lax.broadcasted_iota(jnp.int32, (1, LANES), 1)
    sign = jnp.where(lane < state, -1.0, 1.0).astype(F32)
    lr = jnp.minimum(lr_ref[0], LAM_RE_MAX)
    li = li_ref[0]
    dt = jnp.exp(ls_ref[0])
    mag = jnp.exp(lr * dt)
    ab_r = mag * jnp.cos(li * dt)
    ab_i = mag * jnp.sin(li * dt)
    den = lr * lr + li * li
    nr = ab_r - 1.0
    fr = (nr * lr + ab_i * li) / den
    fi = (ab_i * lr - nr * li) / den
    bcat = bcat_ref[0]
    bswap = bswap_ref[0]
    bb_cat = fr * bcat + sign * fi * bswap
    bb_swap = fr * bswap - sign * fi * bcat
    ccat = ccat_ref[0]
    cswap = cswap_ref[0]

    row_group = lax.broadcasted_iota(jnp.int32, (LANES, 1), 0) // group

    def store_block_diag(ref, block, value):
        for g in range(gl):
            ref[0, block * LANES:(block + 1) * LANES, g * LANES:(g + 1) * LANES] = (
                jnp.where(row_group == g, value, 0.0).astype(ref.dtype))

    pw_r = jnp.ones((LANES, LANES), F32)
    pw_i = jnp.zeros((LANES, LANES), F32)
    for n in range(SSM_T + 1):
        ft_blk = pw_r * ccat + pw_i * cswap
        if n < SSM_T:
            ft0_scr[n * LANES:(n + 1) * LANES, :] = ft_blk
            store_block_diag(e_ref, SSM_T - 1 - n, pw_r * bb_cat + sign * pw_i * bb_swap)
        if n >= 1:
            store_block_diag(ft_ref, n - 1, ft_blk)
        pw_r, pw_i = pw_r * ab_r - pw_i * ab_i, pw_r * ab_i + pw_i * ab_r

    width = SSM_T * LANES
    r = _dot_nt(bb_cat, ft0_scr[...], precision=lax.Precision.HIGHEST)
    col = lax.broadcasted_iota(jnp.int32, (LANES, width), 1)
    row = lax.broadcasted_iota(jnp.int32, (LANES, width), 0)
    r = jnp.where(row // group == (col % LANES) // group, r, 0.0)
    r = r + jnp.where(col == row, jnp.tile(d_ref[0], (1, SSM_T)), 0.0)
    for j in range(SSM_T):
        rows_j = slice(j * LANES, (j + 1) * LANES)
        if j:
            m_ref[0, rows_j, 0:j * LANES] = jnp.zeros((LANES, j * LANES), m_ref.dtype)
        m_ref[0, rows_j, j * LANES:width] = r[:, 0:width - j * LANES].astype(m_ref.dtype)

    lrl = jnp.minimum(lrl_ref[0], LAM_RE_MAX)
    dtl = jnp.exp(lsl_ref[0])
    lane_l = lax.broadcasted_iota(jnp.int32, lrl.shape, 1)
    sign_l = jnp.where(lane_l % LANES < state, -1.0, 1.0).astype(F32)
    steps = lax.broadcasted_iota(jnp.int32, (8, 1), 0)
    nd = (SSM_T * jnp.left_shift(1, jnp.minimum(steps, nsteps))).astype(F32)
    mag_l = jnp.exp(lrl * dtl * nd)
    ang_l = lil_ref[0] * dtl * nd
    p_ref[0] = mag_l * jnp.cos(ang_l)
    q_ref[0] = sign_l * mag_l * jnp.sin(ang_l)


def _ssm_prep(rows_in, lanes_in, *, state, group, nsteps):
    nb = rows_in[0].shape[0]
    gl = LANES // group
    width = SSM_T * LANES
    bspec = lambda *shape: pl.BlockSpec((1,) + shape, lambda i: (i, 0, 0))
    blocks = width * width * 2 + 2 * width * gl * LANES * 2 + 8 * LANES * LANES * 4
    return pl.pallas_call(
        functools.partial(_ssm_prep_kernel, state=state, group=group, nsteps=nsteps),
        out_shape=(
            jax.ShapeDtypeStruct((nb, width, width), BF16),
            jax.ShapeDtypeStruct((nb, width, gl * LANES), BF16),
            jax.ShapeDtypeStruct((nb, width, gl * LANES), BF16),
            jax.ShapeDtypeStruct((nb, 8, gl * LANES), F32),
            jax.ShapeDtypeStruct((nb, 8, gl * LANES), F32),
        ),
        grid=(nb,),
        in_specs=[bspec(LANES, LANES)] * 8 + [bspec(1, gl * LANES)] * 3,
        out_specs=(bspec(width, width), bspec(width, gl * LANES), bspec(width, gl * LANES),
                   bspec(8, gl * LANES), bspec(8, gl * LANES)),
        scratch_shapes=[pltpu.VMEM((width, LANES), F32)],
        compiler_params=_params(("parallel",), blocks, width * LANES * 4 + 6 * LANES * width * 4),
        name="ssm_prep",
    )(*rows_in, *lanes_in)


SSM_COL_SPLIT = 4


def _ssm_kernel(u_ref, m_ref, e_ref, ft_ref, p_ref, q_ref, o_ref, ys_ref, *, kchunks, nsteps, state):
    u = u_ref[0]
    rows, width = u.shape
    s = jnp.dot(u, e_ref[0], preferred_element_type=F32)
    kidx = lax.broadcasted_iota(jnp.int32, (rows, 1), 0) & (kchunks - 1)
    s_prev = []
    for g in range(s.shape[1] // LANES):
        lanes = slice(g * LANES, (g + 1) * LANES)
        sg = s[:, lanes]
        for i in range(nsteps):
            d = 1 << i
            x = jnp.where(kidx >= d, pltpu.roll(sg, d, axis=0), 0.0)
            sg = sg + p_ref[0, i:i + 1, lanes] * x + q_ref[0, i:i + 1, lanes] * pltpu.roll(x, state, axis=1)
        s_prev.append(jnp.where(kidx >= 1, pltpu.roll(sg, 1, axis=0), 0.0).astype(BF16))
    s_prev = jnp.concatenate(s_prev, axis=1)
    cw = width // SSM_COL_SPLIT
    for c in range(SSM_COL_SPLIT):
        cols = slice(c * cw, (c + 1) * cw)
        kk = (c + 1) * cw
        y = jax.nn.gelu(jnp.dot(u[:, 0:kk], m_ref[0, 0:kk, cols], preferred_element_type=F32)
                        + _dot_nt(s_prev, ft_ref[0, cols, :]))
        for tl in range(cw // LANES):
            t = c * (cw // LANES) + tl
            ys_ref[pl.ds(t, rows, stride=SSM_T), :] = y[:, tl * LANES:(tl + 1) * LANES]
    o_ref[0] = ys_ref[...].astype(o_ref.dtype)


def _ssm(u4, m, e, ft, p, q, *, kchunks, nsteps, rt):
    nb, rows, width = u4.shape
    sl = e.shape[2]
    blocks = 2 * rt * width * 2 + width * width * 2 + 2 * width * sl * 2 + 2 * 8 * sl * 4
    return pl.pallas_call(
        functools.partial(_ssm_kernel, kchunks=kchunks, nsteps=nsteps, state=LANES // 2),
        out_shape=jax.ShapeDtypeStruct((nb, rows * SSM_T, LANES), BF16),
        grid=(nb, rows // rt),
        in_specs=[pl.BlockSpec((1, rt, width), lambda b, r: (b, r, 0)),
                  pl.BlockSpec((1, width, width), lambda b, r: (b, 0, 0)),
                  pl.BlockSpec((1, width, sl), lambda b, r: (b, 0, 0)),
                  pl.BlockSpec((1, width, sl), lambda b, r: (b, 0, 0)),
                  pl.BlockSpec((1, 8, sl), lambda b, r: (b, 0, 0)),
                  pl.BlockSpec((1, 8, sl), lambda b, r: (b, 0, 0))],
        out_specs=pl.BlockSpec((1, rt * SSM_T, LANES), lambda b, r: (b, r, 0)),
        scratch_shapes=[pltpu.VMEM((rt * SSM_T, LANES), F32)],
        compiler_params=_params(("parallel", "arbitrary"), blocks,
                                rt * width * 4 + 4 * rt * sl * 4 + 4 * rt * (width // SSM_COL_SPLIT) * 4),
        name="ssm",
    )(u4, m, e, ft, p, q)


def _bias_kernel(tab_ref, idx_ref, o_ref, *, n_buckets, far_bucket):
    h = pl.program_id(0)
    far = tab_ref[far_bucket, h]
    for t in range(idx_ref.shape[0]):
        idx = idx_ref[t]
        acc = jnp.full(idx.shape, MASK_VALUE, F32)
        for b in range(n_buckets):
            acc = jnp.where(idx == b, (tab_ref[b, h] - far) * LOG2_E, acc)
        o_ref[0, t] = acc


def _t5_bucket_np(rel, n_buckets):
    nb = n_buckets // 2
    ret = np.where(rel > 0, nb, 0)
    n = np.abs(rel)
    max_exact = nb // 2
    nf = np.maximum(n, 1).astype(np.float32)
    large = max_exact + (np.log(nf / np.float32(max_exact)) / np.float32(math.log(MAX_DISTANCE / max_exact))
                         * np.float32(nb - max_exact)).astype(np.int32)
    large = np.minimum(large, nb - 1)
    return (ret + np.where(n < max_exact, n, large)).astype(np.int32)


def _bias_tiles(rel_bias, *, tq):
    n_buckets, n_heads = rel_bias.shape
    c = np.arange(tq)[:, None]
    r = np.arange(tq)[None, :]
    prev = _t5_bucket_np(c - r - tq, n_buckets)
    diag = np.where(c // CHUNK <= r // CHUNK, _t5_bucket_np(c - r, n_buckets), -1)
    idx = np.stack([prev, diag])
    far_bucket = int(_t5_bucket_np(np.array(-MAX_DISTANCE), n_buckets))
    assert tq >= MAX_DISTANCE and (prev[0, :] == far_bucket).all()
    return pl.pallas_call(
        functools.partial(_bias_kernel, n_buckets=n_buckets, far_bucket=far_bucket),
        out_shape=jax.ShapeDtypeStruct((n_heads, 2, tq, tq), F32),
        grid=(n_heads,),
        in_specs=[pl.BlockSpec(memory_space=pltpu.SMEM),
                  pl.BlockSpec((2, tq, tq), lambda h: (0, 0, 0))],
        out_specs=pl.BlockSpec((1, 2, tq, tq), lambda h: (h, 0, 0, 0)),
        compiler_params=pltpu.CompilerParams(dimension_semantics=("parallel",)),
        name="attn_bias",
    )(rel_bias, jnp.asarray(idx))


ONES_ROWS = 16
FIXED_SHIFT_MAX = 60.0
BOUND_SLACK = 1.02


def _attn_kernel(q_ref, k_ref, v_ref, bias_ref, lq1_ref, lk1_ref, lq2_ref, lk2_ref, sw_ref,
                 o_ref, vt_scr, qm_scr, shift_scr, st_a, st_b, m_scr, acc_scr, *, tq, head_dim, lambda_init):
    seq, vd = v_ref.shape
    nq = seq // tq

    def block(j):
        return slice(j * tq, (j + 1) * tq)

    lo = lax.broadcasted_iota(jnp.int32, (1, LANES), 1) < head_dim
    halves = (lax.broadcasted_iota(jnp.int32, (8, LANES), 0)
              == (lax.broadcasted_iota(jnp.int32, (8, LANES), 1) >= head_dim).astype(jnp.int32)
              ).astype(BF16)
    qsq, ksq = [], []
    for c in range(nq):
        vt_scr[0:vd, block(c)] = v_ref[block(c), :].astype(F32).T.astype(BF16)
        q = q_ref[block(c), :]
        zero = jnp.zeros_like(q)
        qm_scr[0, block(c), :] = jnp.where(lo, q, zero)
        qm_scr[1, block(c), :] = jnp.where(lo, zero, q)
        k = k_ref[block(c), :]
        qsq.append(_dot_nt(halves, q * q))
        ksq.append(_dot_nt(halves, k * k))
    vt_scr[vd:, :] = jnp.ones((ONES_ROWS, seq), BF16)

    k_max = jnp.max(jnp.concatenate(ksq, axis=1), axis=1, keepdims=True)
    bias_max = jnp.maximum(jnp.max(bias_ref[0]), 0.0)
    bound = jnp.sqrt(jnp.concatenate(qsq, axis=1) * k_max) * BOUND_SLACK + bias_max
    shift_scr[0] = bound[0:1]
    shift_scr[1] = bound[1:2]
    fixed_shift_ok = jnp.max(bound[0:2]) <= FIXED_SHIFT_MAX

    s1 = jnp.sum(lq1_ref[...] * lk1_ref[...], axis=-1, keepdims=True)
    s2 = jnp.sum(lq2_ref[...] * lk2_ref[...], axis=-1, keepdims=True)
    lam = jnp.exp(s1) - jnp.exp(s2) + lambda_init

    def finalize(rows, a1, a2):
        ot = a1[0:vd] * (1.0 / a1[vd:vd + 1]) - lam * (a2[0:vd] * (1.0 / a2[vd:vd + 1]))
        y = _rms_norm(ot.T, sw_ref[...], SUBLN_EPS) * (1.0 - lambda_init)
        o_ref[rows, :] = y.astype(o_ref.dtype)


    @pl.when(fixed_shift_ok)
    def _():
        def scores(i, g, st_ref):
            kb = k_ref[block(g), :]
            for m in range(2):
                st_ref[m] = _dot_nt(kb, qm_scr[m, block(i), :])

        def accumulate(i, g, st_ref):
            vt = vt_scr[:, block(g)]
            for m in range(2):
                st = st_ref[m]
                if g >= i - 1:
                    st = st + bias_ref[0, 1 if g == i else 0]
                p = jnp.exp2(st - shift_scr[m, :, block(i)]).astype(BF16)
                pv = jnp.dot(vt, p, preferred_element_type=F32)
                if g == 0:
                    acc_scr[i, m] = pv
                else:
                    acc_scr[i, m] += pv

        work = [(i, g) for i in range(nq) for g in range(i + 1)]
        bufs = (st_a, st_b)
        scores(*work[0], bufs[0])
        for t, (i, g) in enumerate(work):
            if t + 1 < len(work):
                scores(*work[t + 1], bufs[(t + 1) % 2])
            accumulate(i, g, bufs[t % 2])
            if g == i:
                finalize(block(i), acc_scr[i, 0], acc_scr[i, 1])

    @pl.when(jnp.logical_not(fixed_shift_ok))
    def _():
        def update(i, g, bias):
            kb = k_ref[pl.ds(pl.multiple_of(g * tq, tq), tq), :]
            vt = vt_scr[:, pl.ds(pl.multiple_of(g * tq, tq), tq)]
            for m in range(2):
                st = _dot_nt(kb, qm_scr[m, pl.ds(pl.multiple_of(i * tq, tq), tq), :])
                if bias is not None:
                    st = st + bias
                m_prev = m_scr[m]
                m_new = jnp.maximum(m_prev, jnp.max(st, axis=0, keepdims=True))
                p = jnp.exp2(st - m_new).astype(BF16)
                acc_scr[0, m] = (jnp.exp2(m_prev - m_new) * acc_scr[0, m]
                                 + jnp.dot(vt, p, preferred_element_type=F32))
                m_scr[m] = m_new

        def query_block(i, carry):
            m_scr[...] = jnp.full(m_scr.shape, MASK_VALUE, F32)
            acc_scr[0] = jnp.zeros(acc_scr.shape[1:], F32)
            update(i, i, bias_ref[0, 1])
            pl.when(i >= 1)(lambda: update(i, i - 1, bias_ref[0, 0]))
            lax.fori_loop(0, jnp.maximum(i - 1, 0), lambda g, c: (update(i, g, None), c)[1], 0)
            finalize(pl.ds(pl.multiple_of(i * tq, tq), tq), acc_scr[0, 0], acc_scr[0, 1])
            return carry

        lax.fori_loop(0, nq, query_block, 0)


def _attention(z, bias, lq1, lk1, lq2, lk2, sw, *, batch, seq, n_heads, head_dim, tq,
               q_col, k_col, v_col, lambda_init):
    n = z.shape[0]
    nq = seq // tq
    acc_rows = LANES + ONES_ROWS
    vec = pl.BlockSpec((1, head_dim), lambda b, h: (0, 0))
    blocks = 4 * seq * LANES * 2 + 2 * tq * tq * 4
    scratch = (acc_rows * seq * 2 + 2 * seq * LANES * 2 + 2 * 8 * seq * 4 + 2 * 8 * tq * 4
               + nq * 2 * acc_rows * tq * 4)
    return pl.pallas_call(
        functools.partial(_attn_kernel, tq=tq, head_dim=head_dim, lambda_init=lambda_init),
        out_shape=jax.ShapeDtypeStruct((n, n_heads * LANES), BF16),
        grid=(batch, n_heads),
        in_specs=[
            pl.BlockSpec((seq, LANES), lambda b, h: (b, q_col + h)),
            pl.BlockSpec((seq, LANES), lambda b, h: (b, k_col + h)),
            pl.BlockSpec((seq, LANES), lambda b, h: (b, v_col + h)),
            pl.BlockSpec((1, 2, tq, tq), lambda b, h: (h, 0, 0, 0)),
            vec, vec, vec, vec,
            pl.BlockSpec((1, LANES), lambda b, h: (0, 0)),
        ],
        out_specs=pl.BlockSpec((seq, LANES), lambda b, h: (b, h)),
        scratch_shapes=[pltpu.VMEM((acc_rows, seq), BF16),
                        pltpu.VMEM((2, seq, LANES), BF16),
                        pltpu.VMEM((2, 1, seq), F32),
                        pltpu.VMEM((2, tq, tq), F32),
                        pltpu.VMEM((2, tq, tq), F32),
                        pltpu.VMEM((2, 1, tq), F32),
                        pltpu.VMEM((nq, 2, acc_rows, tq), F32)],
        compiler_params=_params(("parallel", "parallel"), blocks, scratch + 8 * tq * tq * 4),
        name="attention",
    )(z, z, z, bias, lq1, lk1, lq2, lk2, sw)


def _merge_kernel(yg_ref, wglu_ref, bglu_ref, ya_ref, gs_ref, ga_ref, ps_ref, pa_ref, wo_ref,
                  x_ref, o_ref, ys_ref, m_ref):
    j = pl.program_id(1)
    nblk = yg_ref.shape[0]
    per_chunk = PROJ_CHUNK // LANES

    @pl.when(j == 0)
    def _():
        yg = jnp.concatenate([yg_ref[b] for b in range(nblk)], axis=1)
        for c in range(nblk // per_chunk):
            cols = slice(c * PROJ_CHUNK, (c + 1) * PROJ_CHUNK)
            g = jnp.dot(yg, wglu_ref[:, cols], preferred_element_type=F32) + bglu_ref[:, cols]
            ys_ref[:, cols] = (yg[:, cols].astype(F32) * jax.nn.sigmoid(g)).astype(BF16)

    ys = ys_ref[...]
    ya = ya_ref[...]
    for c in range(m_ref.shape[1] // PROJ_CHUNK):
        cols = slice(c * PROJ_CHUNK, (c + 1) * PROJ_CHUNK)
        m = (gs_ref[:, cols].astype(F32) * jnp.dot(ys, ps_ref[:, cols], preferred_element_type=F32)
             + ga_ref[:, cols].astype(F32) * jnp.dot(ya, pa_ref[:, cols], preferred_element_type=F32))
        m_ref[:, cols] = m.astype(BF16)
    upd = jnp.dot(m_ref[...], wo_ref[...], preferred_element_type=F32)

    @pl.when(j == 0)
    def _():
        o_ref[...] = x_ref[...] + upd

    @pl.when(j > 0)
    def _():
        o_ref[...] += upd


def _merge(yg, wglu, bglu, ya, z, ps, pa, wo, x2, *, layer, gs_col, ga_col, tm, tj):
    n, d = x2.shape
    ws = yg.shape[0] * LANES
    wa = ya.shape[1]
    blocks = (tm * ws * 2 + ws * ws * 2 + ws * 4 + tm * wa * 2 + 2 * tm * tj * 2
              + ws * tj * 2 + wa * tj * 2 + tj * d * 2 + 2 * tm * d * 4)
    return pl.pallas_call(
        _merge_kernel,
        out_shape=jax.ShapeDtypeStruct((n, d), F32),
        grid=(n // tm, d // tj),
        in_specs=[
            pl.BlockSpec((ws // LANES, tm, LANES), lambda i, j: (0, i, 0)),
            pl.BlockSpec((None, ws, ws), lambda i, j: (layer, 0, 0)),
            pl.BlockSpec((1, ws), lambda i, j: (0, 0)),
            pl.BlockSpec((tm, wa), lambda i, j: (i, 0)),
            pl.BlockSpec((tm, tj), lambda i, j: (i, gs_col + j)),
            pl.BlockSpec((tm, tj), lambda i, j: (i, ga_col + j)),
            pl.BlockSpec((None, ws, tj), lambda i, j: (layer, 0, j)),
            pl.BlockSpec((None, wa, tj), lambda i, j: (layer, 0, j)),
            pl.BlockSpec((None, tj, d), lambda i, j: (layer, j, 0)),
            pl.BlockSpec((tm, d), lambda i, j: (i, 0)),
        ],
        out_specs=pl.BlockSpec((tm, d), lambda i, j: (i, 0)),
        scratch_shapes=[pltpu.VMEM((tm, ws), BF16), pltpu.VMEM((tm, tj), BF16)],
        compiler_params=_params(("parallel", "arbitrary"), blocks,
                                tm * ws * 2 + tm * tj * 2 + 6 * tm * PROJ_CHUNK * 4 + 2 * tm * d * 4),
        name="merge",
    )(yg, wglu, bglu, ya, z, z, ps, pa, wo, x2)


def _ffn_kernel(x_ref, nw_ref, wg_ref, wu_ref, wd_ref, o_ref, h_ref, a_ref):
    j = pl.program_id(1)

    @pl.when(j == 0)
    def _():
        x = x_ref[...]
        h_ref[...] = _rms_norm(x, nw_ref[...], RMS_EPS).astype(BF16)
        o_ref[...] = x

    h = h_ref[...]
    for c in range(a_ref.shape[1] // PROJ_CHUNK):
        cols = slice(c * PROJ_CHUNK, (c + 1) * PROJ_CHUNK)
        g = jnp.dot(h, wg_ref[:, cols], preferred_element_type=F32)
        u = jnp.dot(h, wu_ref[:, cols], preferred_element_type=F32)
        a_ref[:, cols] = (jax.nn.silu(g) * u).astype(BF16)
    a = a_ref[...]
    cw = o_ref.shape[1] // DOWN_SPLIT
    for c in range(DOWN_SPLIT):
        cols = slice(c * cw, (c + 1) * cw)
        o_ref[:, cols] += jnp.dot(a, wd_ref[:, cols], preferred_element_type=F32)


def _ffn(x2, nw, wg, wu, wd, *, layer, tm, tf):
    n, d = x2.shape
    dff = wg.shape[2]
    blocks = 2 * tm * d * 4 + d * 4 + 3 * d * tf * 2
    return pl.pallas_call(
        _ffn_kernel,
        out_shape=jax.ShapeDtypeStruct((n, d), F32),
        grid=(n // tm, dff // tf),
        in_specs=[
            pl.BlockSpec((tm, d), lambda i, j: (i, 0)),
            pl.BlockSpec((1, d), lambda i, j: (0, 0)),
            pl.BlockSpec((None, d, tf), lambda i, j: (layer, 0, j)),
            pl.BlockSpec((None, d, tf), lambda i, j: (layer, 0, j)),
            pl.BlockSpec((None, tf, d), lambda i, j: (layer, j, 0)),
        ],
        out_specs=pl.BlockSpec((tm, d), lambda i, j: (i, 0)),
        scratch_shapes=[pltpu.VMEM((tm, d), BF16), pltpu.VMEM((tm, tf), BF16)],
        compiler_params=_params(("parallel", "arbitrary"), blocks,
                                tm * d * 2 + tm * tf * 2 + 6 * tm * PROJ_CHUNK * 4),
        name="ffn",
    )(x2, nw, wg, wu, wd)


def _largest_tile(total, target, quantum):
    t = min(total, target)
    while total % t or t % quantum:
        t -= quantum
    return t


def kernel(x, norm1_w, w_in, lam_re, lam_im, log_step, ssm_b_re, ssm_b_im, ssm_c_re, ssm_c_im, ssm_d, w_glu, b_glu, q_norm_w, k_norm_w, lambda_q1, lambda_k1, lambda_q2, lambda_k2, subln_w, w_proj_ssm, w_proj_attn, w_out, rel_bias, norm2_w, w_ffn_gate, w_ffn_up, w_ffn_down):
    batch, seq, d_model = x.shape
    depth = w_in.shape[0]
    n = batch * seq
    _, groups, state = lam_re.shape
    group = ssm_b_re.shape[-1]
    ssm_width = groups * group
    head_dim = q_norm_w.shape[-1]
    v_dim = subln_w.shape[-1]
    attn_width = w_proj_attn.shape[1]
    n_heads = attn_width // v_dim
    d_ff = w_ffn_gate.shape[-1]

    tn = ssm_width
    assert ssm_width == attn_width and d_model % tn == 0
    assert w_in.shape[2] == ssm_width + 3 * attn_width + 2 * d_model
    assert 2 * head_dim == LANES and v_dim == LANES and 2 * state == LANES
    assert SSM_T * group == V7X_MXU_DIM
    kchunks = seq // SSM_T
    nsteps = kchunks.bit_length() - 1
    assert seq % SSM_T == 0 and kchunks == 1 << nsteps and nsteps <= 8

    tm_proj = _largest_tile(n, 1024, 16 * SSM_T)
    tm_merge = _largest_tile(n, 512, 16)
    tm_ffn = _largest_tile(n, 512, 8)
    tq = _largest_tile(seq, 512, MAX_DISTANCE)
    assert tq % CHUNK == 0
    tj = _largest_tile(d_model, 1024, PROJ_CHUNK)
    tf = _largest_tile(d_ff, 512, LANES)

    q_col = 0
    k_col = q_col + attn_width // LANES
    v_col = k_col + attn_width // LANES
    gs_col = 3 * attn_width // tj
    ga_col = gs_col + d_model // tj

    bias = _bias_tiles(rel_bias, tq=tq)
    x2 = x.reshape(n, d_model)
    nblk = ssm_width // LANES
    seqs_per_tile = math.gcd(batch, max(1, SSM_ROWS // kchunks))
    rt = kchunks * seqs_per_tile
    dup = lambda a: jnp.concatenate([a, a], axis=-1)
    per_row = lambda a: jnp.repeat(a, group, axis=0).reshape(nblk, LANES, LANES)
    per_lane = lambda a: a.reshape(nblk, 1, -1)
    as_rows = lambda a: a.reshape(nblk, LANES, LANES)

    w_in, w_glu, w_proj_ssm, w_proj_attn, w_out, w_ffn_gate, w_ffn_up, w_ffn_down = (
        w.astype(BF16) for w in (w_in, w_glu, w_proj_ssm, w_proj_attn, w_out,
                                 w_ffn_gate, w_ffn_up, w_ffn_down))

    for l in range(depth):
        lambda_init = 0.8 - 0.6 * math.exp(-0.3 * l)

        qw2 = jnp.tile(q_norm_w[l], LANES // head_dim)[None, :]
        kw2 = jnp.tile(k_norm_w[l], LANES // head_dim)[None, :]
        u3, z = _in_proj(x2, norm1_w[l][None, :], w_in, qw2, kw2,
                         layer=l, head_dim=head_dim, tm=tm_proj, tn=tn)

        lam_re2, lam_im2 = dup(lam_re[l]), dup(lam_im[l])
        step2 = jnp.broadcast_to(log_step[l][:, None], (groups, 2 * state))
        b_re_t = jnp.swapaxes(ssm_b_re[l], 1, 2)
        b_im_t = jnp.swapaxes(ssm_b_im[l], 1, 2)
        c_re, c_im = ssm_c_re[l], ssm_c_im[l]
        d_rows = jnp.broadcast_to(ssm_d[l][:, None], (ssm_width, LANES)).reshape(nblk, LANES, LANES)
        m_op, e_op, ft_op, p_op, q_op = _ssm_prep(
            (per_row(lam_re2), per_row(lam_im2), per_row(step2),
             as_rows(jnp.concatenate([b_re_t, b_im_t], axis=-1)),
             as_rows(jnp.concatenate([b_im_t, b_re_t], axis=-1)),
             as_rows(jnp.concatenate([c_re, -c_im], axis=-1)),
             as_rows(jnp.concatenate([-c_im, -c_re], axis=-1)), d_rows),
            (per_lane(lam_re2), per_lane(lam_im2), per_lane(step2)),
            state=state, group=group, nsteps=nsteps)
        yg = _ssm(u3, m_op, e_op, ft_op, p_op, q_op, kchunks=kchunks, nsteps=nsteps, rt=rt)

        ya = _attention(z, bias, lambda_q1[l][None, :], lambda_k1[l][None, :],
                        lambda_q2[l][None, :], lambda_k2[l][None, :], subln_w[l][None, :],
                        batch=batch, seq=seq, n_heads=n_heads, head_dim=head_dim, tq=tq,
                        q_col=q_col, k_col=k_col, v_col=v_col, lambda_init=lambda_init)

        x2 = _merge(yg, w_glu, b_glu[l][None, :], ya, z, w_proj_ssm, w_proj_attn, w_out, x2,
                    layer=l, gs_col=gs_col, ga_col=ga_col, tm=tm_merge, tj=tj)
        x2 = _ffn(x2, norm2_w[l][None, :], w_ffn_gate, w_ffn_up, w_ffn_down,
                  layer=l, tm=tm_ffn, tf=tf)

    return x2.reshape(batch, seq, d_model)
```

```python
import functools
import math

import numpy as np
import jax
import jax.numpy as jnp
from jax import lax
from jax.experimental import pallas as pl
from jax.experimental.pallas import tpu as pltpu

F32 = jnp.float32
BF16 = jnp.bfloat16

CHUNK = 64
MAX_DISTANCE = 128
RMS_EPS = 1e-6
SUBLN_EPS = 1e-5
LAM_RE_MAX = -1e-4
MASK_VALUE = -1e30
LOG2_E = math.log2(math.e)

LANES = 128
V7X_MXU_DIM = 256
V7X_VMEM_BYTES = 64 * 1024 * 1024

PROJ_CHUNK = V7X_MXU_DIM
DOWN_SPLIT = 4
SSM_T = 16
SSM_ROWS = 256


def _vmem_limit(block_bytes, extra_bytes):
    return int(min(2 * block_bytes + extra_bytes + (4 << 20), V7X_VMEM_BYTES - (6 << 20)))


def _params(semantics, block_bytes, extra_bytes):
    return pltpu.CompilerParams(dimension_semantics=semantics,
                                vmem_limit_bytes=_vmem_limit(block_bytes, extra_bytes))


def _rms_norm(x, w, eps):
    ms = jnp.mean(x * x, axis=-1, keepdims=True)
    return x * lax.rsqrt(ms + eps) * w


def _dot_nt(a, b, **kw):
    return lax.dot_general(a, b, (((1,), (1,)), ((), ())), preferred_element_type=F32, **kw)


def _in_proj_kernel(x_ref, nw_ref, w_ref, qw_ref, kw_ref, u_ref, o_ref, h_ref, us_ref, *, head_dim):
    j = pl.program_id(1)

    @pl.when(j == 0)
    def _():
        h_ref[...] = _rms_norm(x_ref[...], nw_ref[...], RMS_EPS).astype(BF16)

    tn = w_ref.shape[1]

    def project(epilogue):
        h = h_ref[...]
        for c in range(tn // PROJ_CHUNK):
            zc = jnp.dot(h, w_ref[:, c * PROJ_CHUNK:(c + 1) * PROJ_CHUNK], preferred_element_type=F32)
            for s in range(PROJ_CHUNK // LANES):
                epilogue(c * (PROJ_CHUNK // LANES) + s, zc[:, s * LANES:(s + 1) * LANES])

    def store_z(b, y):
        o_ref[:, b * LANES:(b + 1) * LANES] = y.astype(o_ref.dtype)

    def seg_norm(w_row, scale):
        lo = lax.broadcasted_iota(jnp.int32, (1, LANES), 1) < head_dim

        def epilogue(b, zb):
            sq = zb * zb
            s_lo = jnp.sum(jnp.where(lo, sq, 0.0), axis=-1, keepdims=True)
            s_hi = jnp.sum(jnp.where(lo, 0.0, sq), axis=-1, keepdims=True)
            ms = jnp.where(lo, s_lo, s_hi) * (1.0 / head_dim)
            store_z(b, zb * lax.rsqrt(ms + RMS_EPS) * w_row * scale)
        return epilogue

    def store_u(b, zb):
        us_ref[b] = zb
        chunks = zb.shape[0] // SSM_T
        for t in range(SSM_T):
            u_ref[b, :, t * LANES:(t + 1) * LANES] = (
                us_ref[b, pl.ds(t, chunks, stride=SSM_T), :].astype(u_ref.dtype))

    pl.when(j == 0)(lambda: project(store_u))
    pl.when(j == 1)(lambda: project(seg_norm(qw_ref[...], head_dim ** -0.5 * LOG2_E)))
    pl.when(j == 2)(lambda: project(seg_norm(kw_ref[...], 1.0)))
    pl.when(j == 3)(lambda: project(store_z))
    pl.when(j >= 4)(lambda: project(lambda b, zb: store_z(b, jax.nn.sigmoid(zb))))


def _in_proj(x2, nw, w, qw2, kw2, *, layer, head_dim, tm, tn):
    n, d = x2.shape
    width = w.shape[2]
    blocks = tm * d * 4 + d * 4 + d * tn * 2 + 2 * tm * tn * 2
    return pl.pallas_call(
        functools.partial(_in_proj_kernel, head_dim=head_dim),
        out_shape=(jax.ShapeDtypeStruct((tn // LANES, n // SSM_T, SSM_T * LANES), BF16),
                   jax.ShapeDtypeStruct((n, width - tn), BF16)),
        grid=(n // tm, width // tn),
        in_specs=[
            pl.BlockSpec((tm, d), lambda i, j: (i, 0)),
            pl.BlockSpec((1, d), lambda i, j: (0, 0)),
            pl.BlockSpec((None, d, tn), lambda i, j: (layer, 0, j)),
            pl.BlockSpec((1, LANES), lambda i, j: (0, 0)),
            pl.BlockSpec((1, LANES), lambda i, j: (0, 0)),
        ],
        out_specs=(pl.BlockSpec((tn // LANES, tm // SSM_T, SSM_T * LANES), lambda i, j: (0, i, 0)),
                   pl.BlockSpec((tm, tn), lambda i, j: (i, jnp.maximum(j - 1, 0)))),
        scratch_shapes=[pltpu.VMEM((tm, d), BF16), pltpu.VMEM((tn // LANES, tm, LANES), F32)],
        compiler_params=_params(("parallel", "arbitrary"), blocks,
                                tm * d * 2 + tm * tn * 4 + 6 * tm * PROJ_CHUNK * 4),
        name="in_proj",
    )(x2, nw, w, qw2, kw2)


def _ssm_prep_kernel(lr_ref, li_ref, ls_ref, bcat_ref, bswap_ref, ccat_ref, cswap_ref, d_ref,
                     lrl_ref, lil_ref, lsl_ref, m_ref, e_ref, ft_ref, p_ref, q_ref, ft0_scr,
                     *, state, group, nsteps):
    gl = LANES // group
    lane = lax.broadcasted_iota(jnp.int32, (1, LANES), 1)
    sign = jnp.where(lane < state, -1.0, 1.0).astype(F32)
    lr = jnp.minimum(lr_ref[0], LAM_RE_MAX)
    li = li_ref[0]
    dt = jnp.exp(ls_ref[0])
    mag = jnp.exp(lr * dt)
    ab_r = mag * jnp.cos(li * dt)
    ab_i = mag * jnp.sin(li * dt)
    den = lr * lr + li * li
    nr = ab_r - 1.0
    fr = (nr * lr + ab_i * li) / den
    fi = (ab_i * lr - nr * li) / den
    bcat = bcat_ref[0]
    bswap = bswap_ref[0]
    bb_cat = fr * bcat + sign * fi * bswap
    bb_swap = fr * bswap - sign * fi * bcat
    ccat = ccat_ref[0]
    cswap = cswap_ref[0]

    row_group = lax.broadcasted_iota(jnp.int32, (LANES, 1), 0) // group

    def store_block_diag(ref, block, value):
        for g in range(gl):
            ref[0, block * LANES:(block + 1) * LANES, g * LANES:(g + 1) * LANES] = (
                jnp.where(row_group == g, value, 0.0).astype(ref.dtype))

    pw_r = jnp.ones((LANES, LANES), F32)
    pw_i = jnp.zeros((LANES, LANES), F32)
    for n in range(SSM_T + 1):
        ft_blk = pw_r * ccat + pw_i * cswap
        if n < SSM_T:
            ft0_scr[n * LANES:(n + 1) * LANES, :] = ft_blk
            store_block_diag(e_ref, SSM_T - 1 - n, pw_r * bb_cat + sign * pw_i * bb_swap)
        if n >= 1:
            store_block_diag(ft_ref, n - 1, ft_blk)
        pw_r, pw_i = pw_r * ab_r - pw_i * ab_i, pw_r * ab_i + pw_i * ab_r

    width = SSM_T * LANES
    r = _dot_nt(bb_cat, ft0_scr[...], precision=lax.Precision.HIGHEST)
    col = lax.broadcasted_iota(jnp.int32, (LANES, width), 1)
    row = lax.broadcasted_iota(jnp.int32, (LANES, width), 0)
    r = jnp.where(row // group == (col % LANES) // group, r, 0.0)
    r = r + jnp.where(col == row, jnp.tile(d_ref[0], (1, SSM_T)), 0.0)
    for j in range(SSM_T):
        rows_j = slice(j * LANES, (j + 1) * LANES)
        if j:
            m_ref[0, rows_j, 0:j * LANES] = jnp.zeros((LANES, j * LANES), m_ref.dtype)
        m_ref[0, rows_j, j * LANES:width] = r[:, 0:width - j * LANES].astype(m_ref.dtype)

    lrl = jnp.minimum(lrl_ref[0], LAM_RE_MAX)
    dtl = jnp.exp(lsl_ref[0])
    lane_l = lax.broadcasted_iota(jnp.int32, lrl.shape, 1)
    sign_l = jnp.where(lane_l % LANES < state, -1.0, 1.0).astype(F32)
    steps = lax.broadcasted_iota(jnp.int32, (8, 1), 0)
    nd = (SSM_T * jnp.left_shift(1, jnp.minimum(steps, nsteps))).astype(F32)
    mag_l = jnp.exp(lrl * dtl * nd)
    ang_l = lil_ref[0] * dtl * nd
    p_ref[0] = mag_l * jnp.cos(ang_l)
    q_ref[0] = sign_l * mag_l * jnp.sin(ang_l)


def _ssm_prep(rows_in, lanes_in, *, state, group, nsteps):
    nb = rows_in[0].shape[0]
    gl = LANES // group
    width = SSM_T * LANES
    bspec = lambda *shape: pl.BlockSpec((1,) + shape, lambda i: (i, 0, 0))
    blocks = width * width * 2 + 2 * width * gl * LANES * 2 + 8 * LANES * LANES * 4
    return pl.pallas_call(
        functools.partial(_ssm_prep_kernel, state=state, group=group, nsteps=nsteps),
        out_shape=(
            jax.ShapeDtypeStruct((nb, width, width), BF16),
            jax.ShapeDtypeStruct((nb, width, gl * LANES), BF16),
            jax.ShapeDtypeStruct((nb, width, gl * LANES), BF16),
            jax.ShapeDtypeStruct((nb, 8, gl * LANES), F32),
            jax.ShapeDtypeStruct((nb, 8, gl * LANES), F32),
        ),
        grid=(nb,),
        in_specs=[bspec(LANES, LANES)] * 8 + [bspec(1, gl * LANES)] * 3,
        out_specs=(bspec(width, width), bspec(width, gl * LANES), bspec(width, gl * LANES),
                   bspec(8, gl * LANES), bspec(8, gl * LANES)),
        scratch_shapes=[pltpu.VMEM((width, LANES), F32)],
        compiler_params=_params(("parallel",), blocks, width * LANES * 4 + 6 * LANES * width * 4),
        name="ssm_prep",
    )(*rows_in, *lanes_in)


SSM_COL_SPLIT = 4


def _ssm_kernel(u_ref, m_ref, e_ref, ft_ref, p_ref, q_ref, o_ref, ys_ref, ym_ref,
                *, kchunks, nsteps, state):
    u = u_ref[0]
    rows, width = u.shape
    cw = width // SSM_COL_SPLIT
    s = jnp.dot(u, e_ref[0], preferred_element_type=F32)
    for c in range(SSM_COL_SPLIT):
        kk = (c + 1) * cw
        ym_ref[:, c * cw:kk] = jnp.dot(u[:, 0:kk], m_ref[0, 0:kk, c * cw:kk],
                                       preferred_element_type=F32)
    kidx = lax.broadcasted_iota(jnp.int32, (rows, 1), 0) & (kchunks - 1)
    s_prev = []
    for g in range(s.shape[1] // LANES):
        lanes = slice(g * LANES, (g + 1) * LANES)
        sg = s[:, lanes]
        for i in range(nsteps):
            d = 1 << i
            x = jnp.where(kidx >= d, pltpu.roll(sg, d, axis=0), 0.0)
            sg = sg + p_ref[0, i:i + 1, lanes] * x + q_ref[0, i:i + 1, lanes] * pltpu.roll(x, state, axis=1)
        s_prev.append(jnp.where(kidx >= 1, pltpu.roll(sg, 1, axis=0), 0.0).astype(BF16))
    s_prev = jnp.concatenate(s_prev, axis=1)
    for c in range(SSM_COL_SPLIT):
        cols = slice(c * cw, (c + 1) * cw)
        y = jax.nn.gelu(ym_ref[:, cols] + _dot_nt(s_prev, ft_ref[0, cols, :]))
        for tl in range(cw // LANES):
            t = c * (cw // LANES) + tl
            ys_ref[pl.ds(t, rows, stride=SSM_T), :] = y[:, tl * LANES:(tl + 1) * LANES]
    o_ref[0] = ys_ref[...].astype(o_ref.dtype)


def _ssm(u4, m, e, ft, p, q, *, kchunks, nsteps, rt):
    nb, rows, width = u4.shape
    sl = e.shape[2]
    blocks = 2 * rt * width * 2 + width * width * 2 + 2 * width * sl * 2 + 2 * 8 * sl * 4
    return pl.pallas_call(
        functools.partial(_ssm_kernel, kchunks=kchunks, nsteps=nsteps, state=LANES // 2),
        out_shape=jax.ShapeDtypeStruct((nb, rows * SSM_T, LANES), BF16),
        grid=(nb, rows // rt),
        in_specs=[pl.BlockSpec((1, rt, width), lambda b, r: (b, r, 0)),
                  pl.BlockSpec((1, width, width), lambda b, r: (b, 0, 0)),
                  pl.BlockSpec((1, width, sl), lambda b, r: (b, 0, 0)),
                  pl.BlockSpec((1, width, sl), lambda b, r: (b, 0, 0)),
                  pl.BlockSpec((1, 8, sl), lambda b, r: (b, 0, 0)),
                  pl.BlockSpec((1, 8, sl), lambda b, r: (b, 0, 0))],
        out_specs=pl.BlockSpec((1, rt * SSM_T, LANES), lambda b, r: (b, r, 0)),
        scratch_shapes=[pltpu.VMEM((rt * SSM_T, LANES), F32), pltpu.VMEM((rt, width), F32)],
        compiler_params=_params(("parallel", "arbitrary"), blocks,
                                2 * rt * width * 4 + 4 * rt * sl * 4
                                + 4 * rt * (width // SSM_COL_SPLIT) * 4),
        name="ssm",
    )(u4, m, e, ft, p, q)


def _bias_kernel(tab_ref, idx_ref, o_ref, *, n_buckets, far_bucket):
    h = pl.program_id(0)
    far = tab_ref[far_bucket, h]
    blocks = []
    for t in range(2):
        idx = idx_ref[t]
        acc = jnp.full(idx.shape, MASK_VALUE, F32)
        for b in range(n_buckets):
            acc = jnp.where(idx == b, (tab_ref[b, h] - far) * LOG2_E, acc)
        blocks.append(acc)
    prev_blk, near_blk = blocks
    zero_blk = jnp.zeros_like(near_blk)
    mask_blk = jnp.full_like(near_blk, MASK_VALUE)
    nb = o_ref.shape[2] // LANES
    for r in range(nb):
        for c in range(nb):
            where = (slice(r * LANES, (r + 1) * LANES), slice(c * LANES, (c + 1) * LANES))
            o_ref[(0, 0) + where] = prev_blk if (r, c) == (nb - 1, 0) else zero_blk
            o_ref[(0, 1) + where] = (near_blk if r == c else prev_blk if r == c - 1
                                     else mask_blk if r > c else zero_blk)


def _t5_bucket_np(rel, n_buckets):
    nb = n_buckets // 2
    ret = np.where(rel > 0, nb, 0)
    n = np.abs(rel)
    max_exact = nb // 2
    nf = np.maximum(n, 1).astype(np.float32)
    large = max_exact + (np.log(nf / np.float32(max_exact)) / np.float32(math.log(MAX_DISTANCE / max_exact))
                         * np.float32(nb - max_exact)).astype(np.int32)
    large = np.minimum(large, nb - 1)
    return (ret + np.where(n < max_exact, n, large)).astype(np.int32)


def _bias_tiles(rel_bias, *, tq):
    n_buckets, n_heads = rel_bias.shape
    c = np.arange(LANES)[:, None]
    r = np.arange(LANES)[None, :]
    prev = _t5_bucket_np(c - r - LANES, n_buckets)
    near = np.where(c // CHUNK <= r // CHUNK, _t5_bucket_np(c - r, n_buckets), -1)
    idx = np.stack([prev, near])
    far_bucket = int(_t5_bucket_np(np.array(-MAX_DISTANCE), n_buckets))
    assert LANES >= MAX_DISTANCE and LANES % CHUNK == 0 and tq % LANES == 0
    return pl.pallas_call(
        functools.partial(_bias_kernel, n_buckets=n_buckets, far_bucket=far_bucket),
        out_shape=jax.ShapeDtypeStruct((n_heads, 2, tq, tq), F32),
        grid=(n_heads,),
        in_specs=[pl.BlockSpec(memory_space=pltpu.SMEM),
                  pl.BlockSpec((2, LANES, LANES), lambda h: (0, 0, 0))],
        out_specs=pl.BlockSpec((1, 2, tq, tq), lambda h: (h, 0, 0, 0)),
        compiler_params=pltpu.CompilerParams(dimension_semantics=("parallel",)),
        name="attn_bias",
    )(rel_bias, jnp.asarray(idx))


ONES_ROWS = 16
FIXED_SHIFT_MAX = 60.0
BOUND_SLACK = 1.02


def _attn_kernel(q_ref, k_ref, v_ref, bias_ref, lq1_ref, lk1_ref, lq2_ref, lk2_ref, sw_ref,
                 o_ref, vt_scr, qm_scr, shift_scr, st_a, st_b, m_scr, acc_scr, *, tq, head_dim, lambda_init):
    seq, vd = v_ref.shape
    nq = seq // tq

    def block(j):
        return slice(j * tq, (j + 1) * tq)

    lo = lax.broadcasted_iota(jnp.int32, (1, LANES), 1) < head_dim
    halves = (lax.broadcasted_iota(jnp.int32, (8, LANES), 0)
              == (lax.broadcasted_iota(jnp.int32, (8, LANES), 1) >= head_dim).astype(jnp.int32)
              ).astype(BF16)
    qsq, ksq = [], []
    for c in range(nq):
        vt_scr[0:vd, block(c)] = v_ref[block(c), :].astype(F32).T.astype(BF16)
        q = q_ref[block(c), :]
        zero = jnp.zeros_like(q)
        qm_scr[0, block(c), :] = jnp.where(lo, q, zero)
        qm_scr[1, block(c), :] = jnp.where(lo, zero, q)
        k = k_ref[block(c), :]
        qsq.append(_dot_nt(halves, q * q))
        ksq.append(_dot_nt(halves, k * k))
    vt_scr[vd:, :] = jnp.ones((ONES_ROWS, seq), BF16)

    k_max = jnp.max(jnp.concatenate(ksq, axis=1), axis=1, keepdims=True)
    bias_max = jnp.maximum(jnp.max(bias_ref[0]), 0.0)
    bound = jnp.sqrt(jnp.concatenate(qsq, axis=1) * k_max) * BOUND_SLACK + bias_max
    shift_scr[0] = bound[0:1]
    shift_scr[1] = bound[1:2]
    fixed_shift_ok = jnp.max(bound[0:2]) <= FIXED_SHIFT_MAX

    s1 = jnp.sum(lq1_ref[...] * lk1_ref[...], axis=-1, keepdims=True)
    s2 = jnp.sum(lq2_ref[...] * lk2_ref[...], axis=-1, keepdims=True)
    lam = jnp.exp(s1) - jnp.exp(s2) + lambda_init

    def finalize(rows, a1, a2):
        ot = a1[0:vd] * (1.0 / a1[vd:vd + 1]) - lam * (a2[0:vd] * (1.0 / a2[vd:vd + 1]))
        y = _rms_norm(ot.T, sw_ref[...], SUBLN_EPS) * (1.0 - lambda_init)
        o_ref[rows, :] = y.astype(o_ref.dtype)


    @pl.when(fixed_shift_ok)
    def _():
        def scores(item, st_ref):
            i, g, k0, nk, q0, nqr = item
            kb = k_ref[g * tq + k0:g * tq + k0 + nk, :]
            for m in range(2):
                st_ref[m, 0:nk, 0:nqr] = _dot_nt(kb, qm_scr[m, i * tq + q0:i * tq + q0 + nqr, :])

        def accumulate(item, st_ref):
            i, g, k0, nk, q0, nqr = item
            vt = vt_scr[:, g * tq + k0:g * tq + k0 + nk]
            for m in range(2):
                st = st_ref[m, 0:nk, 0:nqr]
                if g >= i - 1:
                    st = st + bias_ref[0, 1 if g == i else 0, k0:k0 + nk, q0:q0 + nqr]
                p = jnp.exp2(st - shift_scr[m, :, i * tq + q0:i * tq + q0 + nqr]).astype(BF16)
                pv = jnp.dot(vt, p, preferred_element_type=F32)
                if g == 0 and k0 == 0:
                    acc_scr[i, m] = pv
                else:
                    acc_scr[i, m, :, q0:q0 + nqr] += pv

        half = tq // 2
        assert half % CHUNK == 0
        work = []
        for i in range(nq):
            work += [(i, g, 0, tq, 0, tq) for g in range(i)]
            work += [(i, i, 0, half, 0, tq), (i, i, half, half, half, half)]
        bufs = (st_a, st_b)
        scores(work[0], bufs[0])
        for t, item in enumerate(work):
            if t + 1 < len(work):
                scores(work[t + 1], bufs[(t + 1) % 2])
            accumulate(item, bufs[t % 2])
            if item[1] == item[0] and item[2] > 0:
                finalize(block(item[0]), acc_scr[item[0], 0], acc_scr[item[0], 1])

    @pl.when(jnp.logical_not(fixed_shift_ok))
    def _():
        def update(i, g, bias):
            kb = k_ref[pl.ds(pl.multiple_of(g * tq, tq), tq), :]
            vt = vt_scr[:, pl.ds(pl.multiple_of(g * tq, tq), tq)]
            for m in range(2):
                st = _dot_nt(kb, qm_scr[m, pl.ds(pl.multiple_of(i * tq, tq), tq), :])
                if bias is not None:
                    st = st + bias
                m_prev = m_scr[m]
                m_new = jnp.maximum(m_prev, jnp.max(st, axis=0, keepdims=True))
                p = jnp.exp2(st - m_new).astype(BF16)
                acc_scr[0, m] = (jnp.exp2(m_prev - m_new) * acc_scr[0, m]
                                 + jnp.dot(vt, p, preferred_element_type=F32))
                m_scr[m] = m_new

        def query_block(i, carry):
            m_scr[...] = jnp.full(m_scr.shape, MASK_VALUE, F32)
            acc_scr[0] = jnp.zeros(acc_scr.shape[1:], F32)
            update(i, i, bias_ref[0, 1])
            pl.when(i >= 1)(lambda: update(i, i - 1, bias_ref[0, 0]))
            lax.fori_loop(0, jnp.maximum(i - 1, 0), lambda g, c: (update(i, g, None), c)[1], 0)
            finalize(pl.ds(pl.multiple_of(i * tq, tq), tq), acc_scr[0, 0], acc_scr[0, 1])
            return carry

        lax.fori_loop(0, nq, query_block, 0)


def _attention(z, bias, lq1, lk1, lq2, lk2, sw, *, batch, seq, n_heads, head_dim, tq,
               q_col, k_col, v_col, lambda_init):
    n = z.shape[0]
    nq = seq // tq
    acc_rows = LANES + ONES_ROWS
    vec = pl.BlockSpec((1, head_dim), lambda b, h: (0, 0))
    blocks = 4 * seq * LANES * 2 + 2 * tq * tq * 4
    scratch = (acc_rows * seq * 2 + 2 * seq * LANES * 2 + 2 * 8 * seq * 4 + 2 * 8 * tq * 4
               + nq * 2 * acc_rows * tq * 4)
    return pl.pallas_call(
        functools.partial(_attn_kernel, tq=tq, head_dim=head_dim, lambda_init=lambda_init),
        out_shape=jax.ShapeDtypeStruct((n, n_heads * LANES), BF16),
        grid=(batch, n_heads),
        in_specs=[
            pl.BlockSpec((seq, LANES), lambda b, h: (b, q_col + h)),
            pl.BlockSpec((seq, LANES), lambda b, h: (b, k_col + h)),
            pl.BlockSpec((seq, LANES), lambda b, h: (b, v_col + h)),
            pl.BlockSpec((1, 2, tq, tq), lambda b, h: (h, 0, 0, 0)),
            vec, vec, vec, vec,
            pl.BlockSpec((1, LANES), lambda b, h: (0, 0)),
        ],
        out_specs=pl.BlockSpec((seq, LANES), lambda b, h: (b, h)),
        scratch_shapes=[pltpu.VMEM((acc_rows, seq), BF16),
                        pltpu.VMEM((2, seq, LANES), BF16),
                        pltpu.VMEM((2, 1, seq), F32),
                        pltpu.VMEM((2, tq, tq), F32),
                        pltpu.VMEM((2, tq, tq), F32),
                        pltpu.VMEM((2, 1, tq), F32),
                        pltpu.VMEM((nq, 2, acc_rows, tq), F32)],
        compiler_params=_params(("parallel", "parallel"), blocks, scratch + 8 * tq * tq * 4),
        name="attention",
    )(z, z, z, bias, lq1, lk1, lq2, lk2, sw)


def _merge_kernel(yg_ref, wglu_ref, bglu_ref, ya_ref, gs_ref, ga_ref, ps_ref, pa_ref, wo_ref,
                  x_ref, o_ref, ys_ref, m_ref):
    j = pl.program_id(1)
    nblk = yg_ref.shape[0]
    per_chunk = PROJ_CHUNK // LANES

    @pl.when(j == 0)
    def _():
        yg = jnp.concatenate([yg_ref[b] for b in range(nblk)], axis=1)
        for c in range(nblk // per_chunk):
            cols = slice(c * PROJ_CHUNK, (c + 1) * PROJ_CHUNK)
            g = jnp.dot(yg, wglu_ref[:, cols], preferred_element_type=F32) + bglu_ref[:, cols]
            ys_ref[:, cols] = (yg[:, cols].astype(F32) * jax.nn.sigmoid(g)).astype(BF16)

    ys = ys_ref[...]
    ya = ya_ref[...]
    for c in range(m_ref.shape[1] // PROJ_CHUNK):
        cols = slice(c * PROJ_CHUNK, (c + 1) * PROJ_CHUNK)
        m = (gs_ref[:, cols].astype(F32) * jnp.dot(ys, ps_ref[:, cols], preferred_element_type=F32)
             + ga_ref[:, cols].astype(F32) * jnp.dot(ya, pa_ref[:, cols], preferred_element_type=F32))
        m_ref[:, cols] = m.astype(BF16)
    upd = jnp.dot(m_ref[...], wo_ref[...], preferred_element_type=F32)

    @pl.when(j == 0)
    def _():
        o_ref[...] = x_ref[...] + upd

    @pl.when(j > 0)
    def _():
        o_ref[...] += upd


def _merge(yg, wglu, bglu, ya, z, ps, pa, wo, x2, *, layer, gs_col, ga_col, tm, tj):
    n, d = x2.shape
    ws = yg.shape[0] * LANES
    wa = ya.shape[1]
    blocks = (tm * ws * 2 + ws * ws * 2 + ws * 4 + tm * wa * 2 + 2 * tm * tj * 2
              + ws * tj * 2 + wa * tj * 2 + tj * d * 2 + 2 * tm * d * 4)
    return pl.pallas_call(
        _merge_kernel,
        out_shape=jax.ShapeDtypeStruct((n, d), F32),
        grid=(n // tm, d // tj),
        in_specs=[
            pl.BlockSpec((ws // LANES, tm, LANES), lambda i, j: (0, i, 0)),
            pl.BlockSpec((None, ws, ws), lambda i, j: (layer, 0, 0)),
            pl.BlockSpec((1, ws), lambda i, j: (0, 0)),
            pl.BlockSpec((tm, wa), lambda i, j: (i, 0)),
            pl.BlockSpec((tm, tj), lambda i, j: (i, gs_col + j)),
            pl.BlockSpec((tm, tj), lambda i, j: (i, ga_col + j)),
            pl.BlockSpec((None, ws, tj), lambda i, j: (layer, 0, j)),
            pl.BlockSpec((None, wa, tj), lambda i, j: (layer, 0, j)),
            pl.BlockSpec((None, tj, d), lambda i, j: (layer, j, 0)),
            pl.BlockSpec((tm, d), lambda i, j: (i, 0)),
        ],
        out_specs=pl.BlockSpec((tm, d), lambda i, j: (i, 0)),
        scratch_shapes=[pltpu.VMEM((tm, ws), BF16), pltpu.VMEM((tm, tj), BF16)],
        compiler_params=_params(("parallel", "arbitrary"), blocks,
                                tm * ws * 2 + tm * tj * 2 + 6 * tm * PROJ_CHUNK * 4 + 2 * tm * d * 4),
        name="merge",
    )(yg, wglu, bglu, ya, z, z, ps, pa, wo, x2)


def _ffn_kernel(x_ref, nw_ref, wg_ref, wu_ref, wd_ref, o_ref, h_ref, a_ref):
    j = pl.program_id(1)

    @pl.when(j == 0)
    def _():
        x = x_ref[...]
        h_ref[...] = _rms_norm(x, nw_ref[...], RMS_EPS).astype(BF16)
        o_ref[...] = x

    h = h_ref[...]
    for c in range(a_ref.shape[1] // PROJ_CHUNK):
        cols = slice(c * PROJ_CHUNK, (c + 1) * PROJ_CHUNK)
        g = jnp.dot(h, wg_ref[:, cols], preferred_element_type=F32)
        u = jnp.dot(h, wu_ref[:, cols], preferred_element_type=F32)
        a_ref[:, cols] = (jax.nn.silu(g) * u).astype(BF16)
    a = a_ref[...]
    cw = o_ref.shape[1] // DOWN_SPLIT
    for c in range(DOWN_SPLIT):
        cols = slice(c * cw, (c + 1) * cw)
        o_ref[:, cols] += jnp.dot(a, wd_ref[:, cols], preferred_element_type=F32)


def _ffn(x2, nw, wg, wu, wd, *, layer, tm, tf):
    n, d = x2.shape
    dff = wg.shape[2]
    blocks = 2 * tm * d * 4 + d * 4 + 3 * d * tf * 2
    return pl.pallas_call(
        _ffn_kernel,
        out_shape=jax.ShapeDtypeStruct((n, d), F32),
        grid=(n // tm, dff // tf),
        in_specs=[
            pl.BlockSpec((tm, d), lambda i, j: (i, 0)),
            pl.BlockSpec((1, d), lambda i, j: (0, 0)),
            pl.BlockSpec((None, d, tf), lambda i, j: (layer, 0, j)),
            pl.BlockSpec((None, d, tf), lambda i, j: (layer, 0, j)),
            pl.BlockSpec((None, tf, d), lambda i, j: (layer, j, 0)),
        ],
        out_specs=pl.BlockSpec((tm, d), lambda i, j: (i, 0)),
        scratch_shapes=[pltpu.VMEM((tm, d), BF16), pltpu.VMEM((tm, tf), BF16)],
        compiler_params=_params(("parallel", "arbitrary"), blocks,
                                tm * d * 2 + tm * tf * 2 + 6 * tm * PROJ_CHUNK * 4),
        name="ffn",
    )(x2, nw, wg, wu, wd)


def _largest_tile(total, target, quantum):
    t = min(total, target)
    while total % t or t % quantum:
        t -= quantum
    return t


def kernel(x, norm1_w, w_in, lam_re, lam_im, log_step, ssm_b_re, ssm_b_im, ssm_c_re, ssm_c_im, ssm_d, w_glu, b_glu, q_norm_w, k_norm_w, lambda_q1, lambda_k1, lambda_q2, lambda_k2, subln_w, w_proj_ssm, w_proj_attn, w_out, rel_bias, norm2_w, w_ffn_gate, w_ffn_up, w_ffn_down):
    batch, seq, d_model = x.shape
    depth = w_in.shape[0]
    n = batch * seq
    _, groups, state = lam_re.shape
    group = ssm_b_re.shape[-1]
    ssm_width = groups * group
    head_dim = q_norm_w.shape[-1]
    v_dim = subln_w.shape[-1]
    attn_width = w_proj_attn.shape[1]
    n_heads = attn_width // v_dim
    d_ff = w_ffn_gate.shape[-1]

    tn = ssm_width
    assert ssm_width == attn_width and d_model % tn == 0
    assert w_in.shape[2] == ssm_width + 3 * attn_width + 2 * d_model
    assert 2 * head_dim == LANES and v_dim == LANES and 2 * state == LANES
    assert SSM_T * group == V7X_MXU_DIM
    kchunks = seq // SSM_T
    nsteps = kchunks.bit_length() - 1
    assert seq % SSM_T == 0 and kchunks == 1 << nsteps and nsteps <= 8

    tm_proj = _largest_tile(n, 1024, 16 * SSM_T)
    tm_merge = _largest_tile(n, 512, 16)
    tm_ffn = _largest_tile(n, 512, 8)
    tq = _largest_tile(seq, 512, MAX_DISTANCE)
    assert tq % CHUNK == 0
    tj = _largest_tile(d_model, 1024, PROJ_CHUNK)
    tf = _largest_tile(d_ff, 512, LANES)

    q_col = 0
    k_col = q_col + attn_width // LANES
    v_col = k_col + attn_width // LANES
    gs_col = 3 * attn_width // tj
    ga_col = gs_col + d_model // tj

    bias = _bias_tiles(rel_bias, tq=tq)
    x2 = x.reshape(n, d_model)
    nblk = ssm_width // LANES
    seqs_per_tile = math.gcd(batch, max(1, SSM_ROWS // kchunks))
    rt = kchunks * seqs_per_tile
    dup = lambda a: jnp.concatenate([a, a], axis=-1)
    per_row = lambda a: jnp.repeat(a, group, axis=0).reshape(nblk, LANES, LANES)
    per_lane = lambda a: a.reshape(nblk, 1, -1)
    as_rows = lambda a: a.reshape(nblk, LANES, LANES)

    w_in, w_glu, w_proj_ssm, w_proj_attn, w_out, w_ffn_gate, w_ffn_up, w_ffn_down = (
        w.astype(BF16) for w in (w_in, w_glu, w_proj_ssm, w_proj_attn, w_out,
                                 w_ffn_gate, w_ffn_up, w_ffn_down))

    for l in range(depth):
        lambda_init = 0.8 - 0.6 * math.exp(-0.3 * l)

        qw2 = jnp.tile(q_norm_w[l], LANES // head_dim)[None, :]
        kw2 = jnp.tile(k_norm_w[l], LANES // head_dim)[None, :]
        u3, z = _in_proj(x2, norm1_w[l][None, :], w_in, qw2, kw2,
                         layer=l, head_dim=head_dim, tm=tm_proj, tn=tn)

        lam_re2, lam_im2 = dup(lam_re[l]), dup(lam_im[l])
        step2 = jnp.broadcast_to(log_step[l][:, None], (groups, 2 * state))
        b_re_t = jnp.swapaxes(ssm_b_re[l], 1, 2)
        b_im_t = jnp.swapaxes(ssm_b_im[l], 1, 2)
        c_re, c_im = ssm_c_re[l], ssm_c_im[l]
        d_rows = jnp.broadcast_to(ssm_d[l][:, None], (ssm_width, LANES)).reshape(nblk, LANES, LANES)
        m_op, e_op, ft_op, p_op, q_op = _ssm_prep(
            (per_row(lam_re2), per_row(lam_im2), per_row(step2),
             as_rows(jnp.concatenate([b_re_t, b_im_t], axis=-1)),
             as_rows(jnp.concatenate([b_im_t, b_re_t], axis=-1)),
             as_rows(jnp.concatenate([c_re, -c_im], axis=-1)),
             as_rows(jnp.concatenate([-c_im, -c_re], axis=-1)), d_rows),
            (per_lane(lam_re2), per_lane(lam_im2), per_lane(step2)),
            state=state, group=group, nsteps=nsteps)
        yg = _ssm(u3, m_op, e_op, ft_op, p_op, q_op, kchunks=kchunks, nsteps=nsteps, rt=rt)

        ya = _attention(z, bias, lambda_q1[l][None, :], lambda_k1[l][None, :],
                        lambda_q2[l][None, :], lambda_k2[l][None, :], subln_w[l][None, :],
                        batch=batch, seq=seq, n_heads=n_heads, head_dim=head_dim, tq=tq,
                        q_col=q_col, k_col=k_col, v_col=v_col, lambda_init=lambda_init)

        x2 = _merge(yg, w_glu, b_glu[l][None, :], ya, z, w_proj_ssm, w_proj_attn, w_out, x2,
                    layer=l, gs_col=gs_col, ga_col=ga_col, tm=tm_merge, tj=tj)
        x2 = _ffn(x2, norm2_w[l][None, :], w_ffn_gate, w_ffn_up, w_ffn_down,
                  layer=l, tm=tm_ffn, tf=tf)

    return x2.reshape(batch, seq, d_model)
```

```python
import functools
import math

import numpy as np
import jax
import jax.numpy as jnp
from jax import lax
from jax.experimental import pallas as pl
from jax.experimental.pallas import tpu as pltpu

F32 = jnp.float32
BF16 = jnp.bfloat16

CHUNK = 64
MAX_DISTANCE = 128
RMS_EPS = 1e-6
SUBLN_EPS = 1e-5
LAM_RE_MAX = -1e-4
MASK_VALUE = -1e30
LOG2_E = math.log2(math.e)

LANES = 128
V7X_MXU_DIM = 256
V7X_VMEM_BYTES = 64 * 1024 * 1024

PROJ_CHUNK = V7X_MXU_DIM
DOWN_SPLIT = 4
SSM_T = 16
SSM_ROWS = 256


def _vmem_limit(block_bytes, extra_bytes):
    return int(min(2 * block_bytes + extra_bytes + (4 << 20), V7X_VMEM_BYTES - (6 << 20)))


def _params(semantics, block_bytes, extra_bytes):
    return pltpu.CompilerParams(dimension_semantics=semantics,
                                vmem_limit_bytes=_vmem_limit(block_bytes, extra_bytes))


def _rms_norm(x, w, eps):
    ms = jnp.mean(x * x, axis=-1, keepdims=True)
    return x * lax.rsqrt(ms + eps) * w


def _dot_nt(a, b, **kw):
    return lax.dot_general(a, b, (((1,), (1,)), ((), ())), preferred_element_type=F32, **kw)


def _in_proj_kernel(x_ref, nw_ref, w_ref, qw_ref, kw_ref, u_ref, o_ref, h_ref, us_ref, *, head_dim):
    j = pl.program_id(1)

    @pl.when(j == 0)
    def _():
        h_ref[...] = _rms_norm(x_ref[...], nw_ref[...], RMS_EPS).astype(BF16)

    tn = w_ref.shape[1]

    def project(epilogue):
        h = h_ref[...]
        for c in range(tn // PROJ_CHUNK):
            zc = jnp.dot(h, w_ref[:, c * PROJ_CHUNK:(c + 1) * PROJ_CHUNK], preferred_element_type=F32)
            for s in range(PROJ_CHUNK // LANES):
                epilogue(c * (PROJ_CHUNK // LANES) + s, zc[:, s * LANES:(s + 1) * LANES])

    def store_z(b, y):
        o_ref[:, b * LANES:(b + 1) * LANES] = y.astype(o_ref.dtype)

    def seg_norm(w_row, scale):
        lo = lax.broadcasted_iota(jnp.int32, (1, LANES), 1) < head_dim

        def epilogue(b, zb):
            sq = zb * zb
            s_lo = jnp.sum(jnp.where(lo, sq, 0.0), axis=-1, keepdims=True)
            s_hi = jnp.sum(jnp.where(lo, 0.0, sq), axis=-1, keepdims=True)
            ms = jnp.where(lo, s_lo, s_hi) * (1.0 / head_dim)
            store_z(b, zb * lax.rsqrt(ms + RMS_EPS) * w_row * scale)
        return epilogue

    def store_u(b, zb):
        us_ref[b] = zb
        chunks = zb.shape[0] // SSM_T
        for t in range(SSM_T):
            u_ref[b, :, t * LANES:(t + 1) * LANES] = (
                us_ref[b, pl.ds(t, chunks, stride=SSM_T), :].astype(u_ref.dtype))

    pl.when(j == 0)(lambda: project(store_u))
    pl.when(j == 1)(lambda: project(seg_norm(qw_ref[...], head_dim ** -0.5 * LOG2_E)))
    pl.when(j == 2)(lambda: project(seg_norm(kw_ref[...], 1.0)))
    pl.when(j == 3)(lambda: project(store_z))
    pl.when(j >= 4)(lambda: project(lambda b, zb: store_z(b, jax.nn.sigmoid(zb))))


def _in_proj(x2, nw, w, qw2, kw2, *, layer, head_dim, tm, tn):
    n, d = x2.shape
    width = w.shape[2]
    blocks = tm * d * 4 + d * 4 + d * tn * 2 + 2 * tm * tn * 2
    return pl.pallas_call(
        functools.partial(_in_proj_kernel, head_dim=head_dim),
        out_shape=(jax.ShapeDtypeStruct((tn // LANES, n // SSM_T, SSM_T * LANES), BF16),
                   jax.ShapeDtypeStruct((n, width - tn), BF16)),
        grid=(n // tm, width // tn),
        in_specs=[
            pl.BlockSpec((tm, d), lambda i, j: (i, 0)),
            pl.BlockSpec((1, d), lambda i, j: (0, 0)),
            pl.BlockSpec((None, d, tn), lambda i, j: (layer, 0, j)),
            pl.BlockSpec((1, LANES), lambda i, j: (0, 0)),
            pl.BlockSpec((1, LANES), lambda i, j: (0, 0)),
        ],
        out_specs=(pl.BlockSpec((tn // LANES, tm // SSM_T, SSM_T * LANES), lambda i, j: (0, i, 0)),
                   pl.BlockSpec((tm, tn), lambda i, j: (i, jnp.maximum(j - 1, 0)))),
        scratch_shapes=[pltpu.VMEM((tm, d), BF16), pltpu.VMEM((tn // LANES, tm, LANES), F32)],
        compiler_params=_params(("parallel", "arbitrary"), blocks,
                                tm * d * 2 + tm * tn * 4 + 6 * tm * PROJ_CHUNK * 4),
        name="in_proj",
    )(x2, nw, w, qw2, kw2)


def _ssm_prep_kernel(lr_ref, li_ref, ls_ref, bcat_ref, bswap_ref, ccat_ref, cswap_ref, d_ref,
                     lrl_ref, lil_ref, lsl_ref, m_ref, e_ref, ft_ref, p_ref, q_ref, ft0_scr,
                     *, state, group, nsteps):
    gl = LANES // group
    lane = lax.broadcasted_iota(jnp.int32, (1, LANES), 1)
    sign = jnp.where(lane < state, -1.0, 1.0).astype(F32)
    lr = jnp.minimum(lr_ref[0], LAM_RE_MAX)
    li = li_ref[0]
    dt = jnp.exp(ls_ref[0])
    mag = jnp.exp(lr * dt)
    ab_r = mag * jnp.cos(li * dt)
    ab_i = mag * jnp.sin(li * dt)
    den = lr * lr + li * li
    nr = ab_r - 1.0
    fr = (nr * lr + ab_i * li) / den
    fi = (ab_i * lr - nr * li) / den
    bcat = bcat_ref[0]
    bswap = bswap_ref[0]
    bb_cat = fr * bcat + sign * fi * bswap
    bb_swap = fr * bswap - sign * fi * bcat
    ccat = ccat_ref[0]
    cswap = cswap_ref[0]

    row_group = lax.broadcasted_iota(jnp.int32, (LANES, 1), 0) // group

    def store_block_diag(ref, block, value):
        for g in range(gl):
            ref[0, block * LANES:(block + 1) * LANES, g * LANES:(g + 1) * LANES] = (
                jnp.where(row_group == g, value, 0.0).astype(ref.dtype))

    pw_r = jnp.ones((LANES, LANES), F32)
    pw_i = jnp.zeros((LANES, LANES), F32)
    for n in range(SSM_T + 1):
        ft_blk = pw_r * ccat + pw_i * cswap
        if n < SSM_T:
            ft0_scr[n * LANES:(n + 1) * LANES, :] = ft_blk
            store_block_diag(e_ref, SSM_T - 1 - n, pw_r * bb_cat + sign * pw_i * bb_swap)
        if n >= 1:
            store_block_diag(ft_ref, n - 1, ft_blk)
        pw_r, pw_i = pw_r * ab_r - pw_i * ab_i, pw_r * ab_i + pw_i * ab_r

    width = SSM_T * LANES
    r = _dot_nt(bb_cat, ft0_scr[...], precision=lax.Precision.HIGHEST)
    col = lax.broadcasted_iota(jnp.int32, (LANES, width), 1)
    row = lax.broadcasted_iota(jnp.int32, (LANES, width), 0)
    r = jnp.where(row // group == (col % LANES) // group, r, 0.0)
    r = r + jnp.where(col == row, jnp.tile(d_ref[0], (1, SSM_T)), 0.0)
    for j in range(SSM_T):
        rows_j = slice(j * LANES, (j + 1) * LANES)
        if j:
            m_ref[0, rows_j, 0:j * LANES] = jnp.zeros((LANES, j * LANES), m_ref.dtype)
        m_ref[0, rows_j, j * LANES:width] = r[:, 0:width - j * LANES].astype(m_ref.dtype)

    lrl = jnp.minimum(lrl_ref[0], LAM_RE_MAX)
    dtl = jnp.exp(lsl_ref[0])
    lane_l = lax.broadcasted_iota(jnp.int32, lrl.shape, 1)
    sign_l = jnp.where(lane_l % LANES < state, -1.0, 1.0).astype(F32)
    steps = lax.broadcasted_iota(jnp.int32, (8, 1), 0)
    nd = (SSM_T * jnp.left_shift(1, jnp.minimum(steps, nsteps))).astype(F32)
    mag_l = jnp.exp(lrl * dtl * nd)
    ang_l = lil_ref[0] * dtl * nd
    p_ref[0] = mag_l * jnp.cos(ang_l)
    q_ref[0] = sign_l * mag_l * jnp.sin(ang_l)


def _ssm_prep(rows_in, lanes_in, *, state, group, nsteps):
    nb = rows_in[0].shape[0]
    gl = LANES // group
    width = SSM_T * LANES
    bspec = lambda *shape: pl.BlockSpec((1,) + shape, lambda i: (i, 0, 0))
    blocks = width * width * 2 + 2 * width * gl * LANES * 2 + 8 * LANES * LANES * 4
    return pl.pallas_call(
        functools.partial(_ssm_prep_kernel, state=state, group=group, nsteps=nsteps),
        out_shape=(
            jax.ShapeDtypeStruct((nb, width, width), BF16),
            jax.ShapeDtypeStruct((nb, width, gl * LANES), BF16),
            jax.ShapeDtypeStruct((nb, width, gl * LANES), BF16),
            jax.ShapeDtypeStruct((nb, 8, gl * LANES), F32),
            jax.ShapeDtypeStruct((nb, 8, gl * LANES), F32),
        ),
        grid=(nb,),
        in_specs=[bspec(LANES, LANES)] * 8 + [bspec(1, gl * LANES)] * 3,
        out_specs=(bspec(width, width), bspec(width, gl * LANES), bspec(width, gl * LANES),
                   bspec(8, gl * LANES), bspec(8, gl * LANES)),
        scratch_shapes=[pltpu.VMEM((width, LANES), F32)],
        compiler_params=_params(("parallel",), blocks, width * LANES * 4 + 6 * LANES * width * 4),
        name="ssm_prep",
    )(*rows_in, *lanes_in)


SSM_COL_SPLIT = 4


def _ssm_kernel(u_ref, m_ref, e_ref, ft_ref, p_ref, q_ref, o_ref, ys_ref, ym_ref,
                *, kchunks, nsteps, state):
    u = u_ref[0]
    rows, width = u.shape
    cw = width // SSM_COL_SPLIT
    s = jnp.dot(u, e_ref[0], preferred_element_type=F32)
    for c in range(SSM_COL_SPLIT):
        kk = (c + 1) * cw
        ym_ref[:, c * cw:kk] = jnp.dot(u[:, 0:kk], m_ref[0, 0:kk, c * cw:kk],
                                       preferred_element_type=F32)
    kidx = lax.broadcasted_iota(jnp.int32, (rows, 1), 0) & (kchunks - 1)
    s_prev = []
    for g in range(s.shape[1] // LANES):
        lanes = slice(g * LANES, (g + 1) * LANES)
        sg = s[:, lanes]
        for i in range(nsteps):
            d = 1 << i
            x = jnp.where(kidx >= d, pltpu.roll(sg, d, axis=0), 0.0)
            sg = sg + p_ref[0, i:i + 1, lanes] * x + q_ref[0, i:i + 1, lanes] * pltpu.roll(x, state, axis=1)
        s_prev.append(jnp.where(kidx >= 1, pltpu.roll(sg, 1, axis=0), 0.0).astype(BF16))
    s_prev = jnp.concatenate(s_prev, axis=1)
    for c in range(SSM_COL_SPLIT):
        cols = slice(c * cw, (c + 1) * cw)
        y = jax.nn.gelu(ym_ref[:, cols] + _dot_nt(s_prev, ft_ref[0, cols, :]))
        for tl in range(cw // LANES):
            t = c * (cw // LANES) + tl
            ys_ref[pl.ds(t, rows, stride=SSM_T), :] = y[:, tl * LANES:(tl + 1) * LANES]
    o_ref[0] = ys_ref[...].astype(o_ref.dtype)


def _ssm(u4, m, e, ft, p, q, *, kchunks, nsteps, rt):
    nb, rows, width = u4.shape
    sl = e.shape[2]
    blocks = 2 * rt * width * 2 + width * width * 2 + 2 * width * sl * 2 + 2 * 8 * sl * 4
    return pl.pallas_call(
        functools.partial(_ssm_kernel, kchunks=kchunks, nsteps=nsteps, state=LANES // 2),
        out_shape=jax.ShapeDtypeStruct((nb, rows * SSM_T, LANES), BF16),
        grid=(nb, rows // rt),
        in_specs=[pl.BlockSpec((1, rt, width), lambda b, r: (b, r, 0)),
                  pl.BlockSpec((1, width, width), lambda b, r: (b, 0, 0)),
                  pl.BlockSpec((1, width, sl), lambda b, r: (b, 0, 0)),
                  pl.BlockSpec((1, width, sl), lambda b, r: (b, 0, 0)),
                  pl.BlockSpec((1, 8, sl), lambda b, r: (b, 0, 0)),
                  pl.BlockSpec((1, 8, sl), lambda b, r: (b, 0, 0))],
        out_specs=pl.BlockSpec((1, rt * SSM_T, LANES), lambda b, r: (b, r, 0)),
        scratch_shapes=[pltpu.VMEM((rt * SSM_T, LANES), F32), pltpu.VMEM((rt, width), F32)],
        compiler_params=_params(("parallel", "arbitrary"), blocks,
                                2 * rt * width * 4 + 4 * rt * sl * 4
                                + 4 * rt * (width // SSM_COL_SPLIT) * 4),
        name="ssm",
    )(u4, m, e, ft, p, q)


def _bias_kernel(tab_ref, idx_ref, o_ref, *, n_buckets, far_bucket):
    h = pl.program_id(0)
    far = tab_ref[far_bucket, h]
    blocks = []
    for t in range(2):
        idx = idx_ref[t]
        acc = jnp.full(idx.shape, MASK_VALUE, F32)
        for b in range(n_buckets):
            acc = jnp.where(idx == b, (tab_ref[b, h] - far) * LOG2_E, acc)
        blocks.append(acc)
    prev_blk, near_blk = blocks
    zero_blk = jnp.zeros_like(near_blk)
    mask_blk = jnp.full_like(near_blk, MASK_VALUE)
    nb = o_ref.shape[2] // LANES
    for r in range(nb):
        for c in range(nb):
            where = (slice(r * LANES, (r + 1) * LANES), slice(c * LANES, (c + 1) * LANES))
            o_ref[(0, 0) + where] = prev_blk if (r, c) == (nb - 1, 0) else zero_blk
            o_ref[(0, 1) + where] = (near_blk if r == c else prev_blk if r == c - 1
                                     else mask_blk if r > c else zero_blk)


def _t5_bucket_np(rel, n_buckets):
    nb = n_buckets // 2
    ret = np.where(rel > 0, nb, 0)
    n = np.abs(rel)
    max_exact = nb // 2
    nf = np.maximum(n, 1).astype(np.float32)
    large = max_exact + (np.log(nf / np.float32(max_exact)) / np.float32(math.log(MAX_DISTANCE / max_exact))
                         * np.float32(nb - max_exact)).astype(np.int32)
    large = np.minimum(large, nb - 1)
    return (ret + np.where(n < max_exact, n, large)).astype(np.int32)


def _bias_tiles(rel_bias, *, tq):
    n_buckets, n_heads = rel_bias.shape
    c = np.arange(LANES)[:, None]
    r = np.arange(LANES)[None, :]
    prev = _t5_bucket_np(c - r - LANES, n_buckets)
    near = np.where(c // CHUNK <= r // CHUNK, _t5_bucket_np(c - r, n_buckets), -1)
    idx = np.stack([prev, near])
    far_bucket = int(_t5_bucket_np(np.array(-MAX_DISTANCE), n_buckets))
    assert LANES >= MAX_DISTANCE and LANES % CHUNK == 0 and tq % LANES == 0
    return pl.pallas_call(
        functools.partial(_bias_kernel, n_buckets=n_buckets, far_bucket=far_bucket),
        out_shape=jax.ShapeDtypeStruct((n_heads, 2, tq, tq), F32),
        grid=(n_heads,),
        in_specs=[pl.BlockSpec(memory_space=pltpu.SMEM),
                  pl.BlockSpec((2, LANES, LANES), lambda h: (0, 0, 0))],
        out_specs=pl.BlockSpec((1, 2, tq, tq), lambda h: (h, 0, 0, 0)),
        compiler_params=pltpu.CompilerParams(dimension_semantics=("parallel",)),
        name="attn_bias",
    )(rel_bias, jnp.asarray(idx))


ONES_ROWS = 16
FIXED_SHIFT_MAX = 60.0
BOUND_SLACK = 1.02


def _attn_kernel(q_ref, k_ref, v_ref, bias_ref, lq1_ref, lk1_ref, lq2_ref, lk2_ref, sw_ref,
                 o_ref, vt_scr, qm_scr, shift_scr, st_a, st_b, m_scr, acc_scr, *, tq, head_dim, lambda_init):
    seq, vd = v_ref.shape
    nq = seq // tq

    def block(j):
        return slice(j * tq, (j + 1) * tq)

    lo = lax.broadcasted_iota(jnp.int32, (1, LANES), 1) < head_dim
    halves = (lax.broadcasted_iota(jnp.int32, (8, LANES), 0)
              == (lax.broadcasted_iota(jnp.int32, (8, LANES), 1) >= head_dim).astype(jnp.int32)
              ).astype(BF16)
    qsq, ksq = [], []
    for c in range(nq):
        vt_scr[0:vd, block(c)] = v_ref[block(c), :].T
        q = q_ref[block(c), :]
        zero = jnp.zeros_like(q)
        qm_scr[0, block(c), :] = jnp.where(lo, q, zero)
        qm_scr[1, block(c), :] = jnp.where(lo, zero, q)
        k = k_ref[block(c), :]
        qsq.append(_dot_nt(halves, q * q))
        ksq.append(_dot_nt(halves, k * k))
    vt_scr[vd:, :] = jnp.ones((ONES_ROWS, seq), BF16)

    k_max = jnp.max(jnp.concatenate(ksq, axis=1), axis=1, keepdims=True)
    bias_max = jnp.maximum(jnp.maximum(jnp.max(bias_ref[0, 1, 0:LANES, 0:LANES]),
                                       jnp.max(bias_ref[0, 0, tq - LANES:tq, 0:LANES])), 0.0)
    bound = jnp.sqrt(jnp.concatenate(qsq, axis=1) * k_max) * BOUND_SLACK + bias_max
    shift_scr[0] = bound[0:1]
    shift_scr[1] = bound[1:2]
    fixed_shift_ok = jnp.max(bound[0:2]) <= FIXED_SHIFT_MAX

    s1 = jnp.sum(lq1_ref[...] * lk1_ref[...], axis=-1, keepdims=True)
    s2 = jnp.sum(lq2_ref[...] * lk2_ref[...], axis=-1, keepdims=True)
    lam = jnp.exp(s1) - jnp.exp(s2) + lambda_init

    def finalize(rows, a1, a2):
        ot = a1[0:vd] * (1.0 / a1[vd:vd + 1]) - lam * (a2[0:vd] * (1.0 / a2[vd:vd + 1]))
        y = _rms_norm(ot.T, sw_ref[...], SUBLN_EPS) * (1.0 - lambda_init)
        o_ref[rows, :] = y.astype(o_ref.dtype)


    @pl.when(fixed_shift_ok)
    def _():
        def scores(item, st_ref):
            i, g, k0, nk, q0, nqr = item
            kb = k_ref[g * tq + k0:g * tq + k0 + nk, :]
            for m in range(2):
                st_ref[m, 0:nk, 0:nqr] = _dot_nt(kb, qm_scr[m, i * tq + q0:i * tq + q0 + nqr, :])

        def accumulate(item, st_ref):
            i, g, k0, nk, q0, nqr = item
            vt = vt_scr[:, g * tq + k0:g * tq + k0 + nk]
            for m in range(2):
                st = st_ref[m, 0:nk, 0:nqr]
                if g >= i - 1:
                    st = st + bias_ref[0, 1 if g == i else 0, k0:k0 + nk, q0:q0 + nqr]
                p = jnp.exp2(st - shift_scr[m, :, i * tq + q0:i * tq + q0 + nqr]).astype(BF16)
                pv = jnp.dot(vt, p, preferred_element_type=F32)
                if g == 0 and k0 == 0:
                    acc_scr[i, m] = pv
                else:
                    acc_scr[i, m, :, q0:q0 + nqr] += pv

        half = tq // 2
        assert half % CHUNK == 0
        work = []
        for i in range(nq):
            work += [(i, g, 0, tq, 0, tq) for g in range(i)]
            work += [(i, i, 0, half, 0, tq), (i, i, half, half, half, half)]
        bufs = (st_a, st_b)
        scores(work[0], bufs[0])
        for t, item in enumerate(work):
            if t + 1 < len(work):
                scores(work[t + 1], bufs[(t + 1) % 2])
            accumulate(item, bufs[t % 2])
            if item[1] == item[0] and item[2] > 0:
                finalize(block(item[0]), acc_scr[item[0], 0], acc_scr[item[0], 1])

    @pl.when(jnp.logical_not(fixed_shift_ok))
    def _():
        def update(i, g, bias):
            kb = k_ref[pl.ds(pl.multiple_of(g * tq, tq), tq), :]
            vt = vt_scr[:, pl.ds(pl.multiple_of(g * tq, tq), tq)]
            for m in range(2):
                st = _dot_nt(kb, qm_scr[m, pl.ds(pl.multiple_of(i * tq, tq), tq), :])
                if bias is not None:
                    st = st + bias
                m_prev = m_scr[m]
                m_new = jnp.maximum(m_prev, jnp.max(st, axis=0, keepdims=True))
                p = jnp.exp2(st - m_new).astype(BF16)
                acc_scr[0, m] = (jnp.exp2(m_prev - m_new) * acc_scr[0, m]
                                 + jnp.dot(vt, p, preferred_element_type=F32))
                m_scr[m] = m_new

        def query_block(i, carry):
            m_scr[...] = jnp.full(m_scr.shape, MASK_VALUE, F32)
            acc_scr[0] = jnp.zeros(acc_scr.shape[1:], F32)
            update(i, i, bias_ref[0, 1])
            pl.when(i >= 1)(lambda: update(i, i - 1, bias_ref[0, 0]))
            lax.fori_loop(0, jnp.maximum(i - 1, 0), lambda g, c: (update(i, g, None), c)[1], 0)
            finalize(pl.ds(pl.multiple_of(i * tq, tq), tq), acc_scr[0, 0], acc_scr[0, 1])
            return carry

        lax.fori_loop(0, nq, query_block, 0)


def _attention(z, bias, lq1, lk1, lq2, lk2, sw, *, batch, seq, n_heads, head_dim, tq,
               q_col, k_col, v_col, lambda_init):
    n = z.shape[0]
    nq = seq // tq
    acc_rows = LANES + ONES_ROWS
    vec = pl.BlockSpec((1, head_dim), lambda b, h: (0, 0))
    blocks = 4 * seq * LANES * 2 + 2 * tq * tq * 4
    scratch = (acc_rows * seq * 2 + 2 * seq * LANES * 2 + 2 * 8 * seq * 4 + 2 * 8 * tq * 4
               + nq * 2 * acc_rows * tq * 4)
    return pl.pallas_call(
        functools.partial(_attn_kernel, tq=tq, head_dim=head_dim, lambda_init=lambda_init),
        out_shape=jax.ShapeDtypeStruct((n, n_heads * LANES), BF16),
        grid=(batch, n_heads),
        in_specs=[
            pl.BlockSpec((seq, LANES), lambda b, h: (b, q_col + h)),
            pl.BlockSpec((seq, LANES), lambda b, h: (b, k_col + h)),
            pl.BlockSpec((seq, LANES), lambda b, h: (b, v_col + h)),
            pl.BlockSpec((1, 2, tq, tq), lambda b, h: (h, 0, 0, 0)),
            vec, vec, vec, vec,
            pl.BlockSpec((1, LANES), lambda b, h: (0, 0)),
        ],
        out_specs=pl.BlockSpec((seq, LANES), lambda b, h: (b, h)),
        scratch_shapes=[pltpu.VMEM((acc_rows, seq), BF16),
                        pltpu.VMEM((2, seq, LANES), BF16),
                        pltpu.VMEM((2, 1, seq), F32),
                        pltpu.VMEM((2, tq, tq), F32),
                        pltpu.VMEM((2, tq, tq), F32),
                        pltpu.VMEM((2, 1, tq), F32),
                        pltpu.VMEM((nq, 2, acc_rows, tq), F32)],
        compiler_params=_params(("parallel", "parallel"), blocks, scratch + 8 * tq * tq * 4),
        name="attention",
    )(z, z, z, bias, lq1, lk1, lq2, lk2, sw)


def _merge_kernel(yg_ref, wglu_ref, bglu_ref, ya_ref, gs_ref, ga_ref, ps_ref, pa_ref, wo_ref,
                  x_ref, o_ref, ys_ref, m_ref):
    j = pl.program_id(1)
    nblk = yg_ref.shape[0]
    per_chunk = PROJ_CHUNK // LANES

    @pl.when(j == 0)
    def _():
        yg = jnp.concatenate([yg_ref[b] for b in range(nblk)], axis=1)
        for c in range(nblk // per_chunk):
            cols = slice(c * PROJ_CHUNK, (c + 1) * PROJ_CHUNK)
            g = jnp.dot(yg, wglu_ref[:, cols], preferred_element_type=F32) + bglu_ref[:, cols]
            ys_ref[:, cols] = (yg[:, cols].astype(F32) * jax.nn.sigmoid(g)).astype(BF16)
        o_ref[...] = x_ref[...]

    ys = ys_ref[...]
    ya = ya_ref[...]
    for c in range(m_ref.shape[1] // PROJ_CHUNK):
        cols = slice(c * PROJ_CHUNK, (c + 1) * PROJ_CHUNK)
        m = (gs_ref[:, cols].astype(F32) * jnp.dot(ys, ps_ref[:, cols], preferred_element_type=F32)
             + ga_ref[:, cols].astype(F32) * jnp.dot(ya, pa_ref[:, cols], preferred_element_type=F32))
        m_ref[:, cols] = m.astype(BF16)
    m = m_ref[...]
    cw = o_ref.shape[1] // DOWN_SPLIT
    for c in range(DOWN_SPLIT):
        cols = slice(c * cw, (c + 1) * cw)
        o_ref[:, cols] += jnp.dot(m, wo_ref[:, cols], preferred_element_type=F32)


def _merge(yg, wglu, bglu, ya, z, ps, pa, wo, x2, *, layer, gs_col, ga_col, tm, tj):
    n, d = x2.shape
    ws = yg.shape[0] * LANES
    wa = ya.shape[1]
    blocks = (tm * ws * 2 + ws * ws * 2 + ws * 4 + tm * wa * 2 + 2 * tm * tj * 2
              + ws * tj * 2 + wa * tj * 2 + tj * d * 2 + 2 * tm * d * 4)
    return pl.pallas_call(
        _merge_kernel,
        out_shape=jax.ShapeDtypeStruct((n, d), F32),
        grid=(n // tm, d // tj),
        in_specs=[
            pl.BlockSpec((ws // LANES, tm, LANES), lambda i, j: (0, i, 0)),
            pl.BlockSpec((None, ws, ws), lambda i, j: (layer, 0, 0)),
            pl.BlockSpec((1, ws), lambda i, j: (0, 0)),
            pl.BlockSpec((tm, wa), lambda i, j: (i, 0)),
            pl.BlockSpec((tm, tj), lambda i, j: (i, gs_col + j)),
            pl.BlockSpec((tm, tj), lambda i, j: (i, ga_col + j)),
            pl.BlockSpec((None, ws, tj), lambda i, j: (layer, 0, j)),
            pl.BlockSpec((None, wa, tj), lambda i, j: (layer, 0, j)),
            pl.BlockSpec((None, tj, d), lambda i, j: (layer, j, 0)),
            pl.BlockSpec((tm, d), lambda i, j: (i, 0)),
        ],
        out_specs=pl.BlockSpec((tm, d), lambda i, j: (i, 0)),
        scratch_shapes=[pltpu.VMEM((tm, ws), BF16), pltpu.VMEM((tm, tj), BF16)],
        compiler_params=_params(("parallel", "arbitrary"), blocks,
                                tm * ws * 2 + tm * tj * 2 + 6 * tm * PROJ_CHUNK * 4 + 2 * tm * d * 4),
        name="merge",
    )(yg, wglu, bglu, ya, z, z, ps, pa, wo, x2)


def _ffn_kernel(x_ref, nw_ref, wg_ref, wu_ref, wd_ref, o_ref, h_ref, a_ref):
    j = pl.program_id(1)

    @pl.when(j == 0)
    def _():
        x = x_ref[...]
        h_ref[...] = _rms_norm(x, nw_ref[...], RMS_EPS).astype(BF16)
        o_ref[...] = x

    h = h_ref[...]
    for c in range(a_ref.shape[1] // PROJ_CHUNK):
        cols = slice(c * PROJ_CHUNK, (c + 1) * PROJ_CHUNK)
        g = jnp.dot(h, wg_ref[:, cols], preferred_element_type=F32)
        u = jnp.dot(h, wu_ref[:, cols], preferred_element_type=F32)
        a_ref[:, cols] = (jax.nn.silu(g) * u).astype(BF16)
    a = a_ref[...]
    cw = o_ref.shape[1] // DOWN_SPLIT
    for c in range(DOWN_SPLIT):
        cols = slice(c * cw, (c + 1) * cw)
        o_ref[:, cols] += jnp.dot(a, wd_ref[:, cols], preferred_element_type=F32)


def _ffn(x2, nw, wg, wu, wd, *, layer, tm, tf):
    n, d = x2.shape
    dff = wg.shape[2]
    blocks = 2 * tm * d * 4 + d * 4 + 3 * d * tf * 2
    return pl.pallas_call(
        _ffn_kernel,
        out_shape=jax.ShapeDtypeStruct((n, d), F32),
        grid=(n // tm, dff // tf),
        in_specs=[
            pl.BlockSpec((tm, d), lambda i, j: (i, 0)),
            pl.BlockSpec((1, d), lambda i, j: (0, 0)),
            pl.BlockSpec((None, d, tf), lambda i, j: (layer, 0, j)),
            pl.BlockSpec((None, d, tf), lambda i, j: (layer, 0, j)),
            pl.BlockSpec((None, tf, d), lambda i, j: (layer, j, 0)),
        ],
        out_specs=pl.BlockSpec((tm, d), lambda i, j: (i, 0)),
        scratch_shapes=[pltpu.VMEM((tm, d), BF16), pltpu.VMEM((tm, tf), BF16)],
        compiler_params=_params(("parallel", "arbitrary"), blocks,
                                tm * d * 2 + tm * tf * 2 + 6 * tm * PROJ_CHUNK * 4),
        name="ffn",
    )(x2, nw, wg, wu, wd)


def _largest_tile(total, target, quantum):
    t = min(total, target)
    while total % t or t % quantum:
        t -= quantum
    return t


def kernel(x, norm1_w, w_in, lam_re, lam_im, log_step, ssm_b_re, ssm_b_im, ssm_c_re, ssm_c_im, ssm_d, w_glu, b_glu, q_norm_w, k_norm_w, lambda_q1, lambda_k1, lambda_q2, lambda_k2, subln_w, w_proj_ssm, w_proj_attn, w_out, rel_bias, norm2_w, w_ffn_gate, w_ffn_up, w_ffn_down):
    batch, seq, d_model = x.shape
    depth = w_in.shape[0]
    n = batch * seq
    _, groups, state = lam_re.shape
    group = ssm_b_re.shape[-1]
    ssm_width = groups * group
    head_dim = q_norm_w.shape[-1]
    v_dim = subln_w.shape[-1]
    attn_width = w_proj_attn.shape[1]
    n_heads = attn_width // v_dim
    d_ff = w_ffn_gate.shape[-1]

    tn = ssm_width
    assert ssm_width == attn_width and d_model % tn == 0
    assert w_in.shape[2] == ssm_width + 3 * attn_width + 2 * d_model
    assert 2 * head_dim == LANES and v_dim == LANES and 2 * state == LANES
    assert SSM_T * group == V7X_MXU_DIM
    kchunks = seq // SSM_T
    nsteps = kchunks.bit_length() - 1
    assert seq % SSM_T == 0 and kchunks == 1 << nsteps and nsteps <= 8

    tm_proj = _largest_tile(n, 1024, 16 * SSM_T)
    tm_merge = _largest_tile(n, 512, 16)
    tm_ffn = _largest_tile(n, 512, 8)
    tq = _largest_tile(seq, 512, MAX_DISTANCE)
    assert tq % CHUNK == 0
    tj = _largest_tile(d_model, 1024, PROJ_CHUNK)
    tf = _largest_tile(d_ff, 512, LANES)

    q_col = 0
    k_col = q_col + attn_width // LANES
    v_col = k_col + attn_width // LANES
    gs_col = 3 * attn_width // tj
    ga_col = gs_col + d_model // tj

    bias = _bias_tiles(rel_bias, tq=tq)
    x2 = x.reshape(n, d_model)
    nblk = ssm_width // LANES
    seqs_per_tile = math.gcd(batch, max(1, SSM_ROWS // kchunks))
    rt = kchunks * seqs_per_tile
    dup = lambda a: jnp.concatenate([a, a], axis=-1)
    per_row = lambda a: jnp.repeat(a, group, axis=0).reshape(nblk, LANES, LANES)
    per_lane = lambda a: a.reshape(nblk, 1, -1)
    as_rows = lambda a: a.reshape(nblk, LANES, LANES)

    w_in, w_glu, w_proj_ssm, w_proj_attn, w_out, w_ffn_gate, w_ffn_up, w_ffn_down = (
        w.astype(BF16) for w in (w_in, w_glu, w_proj_ssm, w_proj_attn, w_out,
                                 w_ffn_gate, w_ffn_up, w_ffn_down))

    for l in range(depth):
        lambda_init = 0.8 - 0.6 * math.exp(-0.3 * l)

        qw2 = jnp.tile(q_norm_w[l], LANES // head_dim)[None, :]
        kw2 = jnp.tile(k_norm_w[l], LANES // head_dim)[None, :]
        u3, z = _in_proj(x2, norm1_w[l][None, :], w_in, qw2, kw2,
                         layer=l, head_dim=head_dim, tm=tm_proj, tn=tn)

        lam_re2, lam_im2 = dup(lam_re[l]), dup(lam_im[l])
        step2 = jnp.broadcast_to(log_step[l][:, None], (groups, 2 * state))
        b_re_t = jnp.swapaxes(ssm_b_re[l], 1, 2)
        b_im_t = jnp.swapaxes(ssm_b_im[l], 1, 2)
        c_re, c_im = ssm_c_re[l], ssm_c_im[l]
        d_rows = jnp.broadcast_to(ssm_d[l][:, None], (ssm_width, LANES)).reshape(nblk, LANES, LANES)
        m_op, e_op, ft_op, p_op, q_op = _ssm_prep(
            (per_row(lam_re2), per_row(lam_im2), per_row(step2),
             as_rows(jnp.concatenate([b_re_t, b_im_t], axis=-1)),
             as_rows(jnp.concatenate([b_im_t, b_re_t], axis=-1)),
             as_rows(jnp.concatenate([c_re, -c_im], axis=-1)),
             as_rows(jnp.concatenate([-c_im, -c_re], axis=-1)), d_rows),
            (per_lane(lam_re2), per_lane(lam_im2), per_lane(step2)),
            state=state, group=group, nsteps=nsteps)
        yg = _ssm(u3, m_op, e_op, ft_op, p_op, q_op, kchunks=kchunks, nsteps=nsteps, rt=rt)

        ya = _attention(z, bias, lambda_q1[l][None, :], lambda_k1[l][None, :],
                        lambda_q2[l][None, :], lambda_k2[l][None, :], subln_w[l][None, :],
                        batch=batch, seq=seq, n_heads=n_heads, head_dim=head_dim, tq=tq,
                        q_col=q_col, k_col=k_col, v_col=v_col, lambda_init=lambda_init)

        x2 = _merge(yg, w_glu, b_glu[l][None, :], ya, z, w_proj_ssm, w_proj_attn, w_out, x2,
                    layer=l, gs_col=gs_col, ga_col=ga_col, tm=tm_merge, tj=tj)
        x2 = _ffn(x2, norm2_w[l][None, :], w_ffn_gate, w_ffn_up, w_ffn_down,
                  layer=l, tm=tm_ffn, tf=tf)

    return x2.reshape(batch, seq, d_model)
```

```python
import functools
import math

import numpy as np
import jax
import jax.numpy as jnp
from jax import lax
from jax.experimental import pallas as pl
from jax.experimental.pallas import tpu as pltpu

F32 = jnp.float32
BF16 = jnp.bfloat16

CHUNK = 64
MAX_DISTANCE = 128
RMS_EPS = 1e-6
SUBLN_EPS = 1e-5
LAM_RE_MAX = -1e-4
MASK_VALUE = -1e30
LOG2_E = math.log2(math.e)

LANES = 128
V7X_MXU_DIM = 256
V7X_VMEM_BYTES = 64 * 1024 * 1024

PROJ_CHUNK = V7X_MXU_DIM
DOWN_SPLIT = 4
SSM_T = 16
SSM_ROWS = 256


def _vmem_limit(block_bytes, extra_bytes):
    return int(min(2 * block_bytes + extra_bytes + (4 << 20), V7X_VMEM_BYTES - (6 << 20)))


def _params(semantics, block_bytes, extra_bytes):
    return pltpu.CompilerParams(dimension_semantics=semantics,
                                vmem_limit_bytes=_vmem_limit(block_bytes, extra_bytes))


def _rms_norm(x, w, eps):
    ms = jnp.mean(x * x, axis=-1, keepdims=True)
    return x * lax.rsqrt(ms + eps) * w


def _dot_nt(a, b, **kw):
    return lax.dot_general(a, b, (((1,), (1,)), ((), ())), preferred_element_type=F32, **kw)


def _in_proj_kernel(x_ref, nw_ref, w_ref, qw_ref, kw_ref, u_ref, o_ref, h_ref, us_ref, *, head_dim):
    j = pl.program_id(1)

    @pl.when(j == 0)
    def _():
        h_ref[...] = _rms_norm(x_ref[...], nw_ref[...], RMS_EPS).astype(BF16)

    tn = w_ref.shape[1]

    def project(epilogue):
        h = h_ref[...]
        for c in range(tn // PROJ_CHUNK):
            zc = jnp.dot(h, w_ref[:, c * PROJ_CHUNK:(c + 1) * PROJ_CHUNK], preferred_element_type=F32)
            for s in range(PROJ_CHUNK // LANES):
                epilogue(c * (PROJ_CHUNK // LANES) + s, zc[:, s * LANES:(s + 1) * LANES])

    def store_z(b, y):
        o_ref[:, b * LANES:(b + 1) * LANES] = y.astype(o_ref.dtype)

    def seg_norm(w_row, scale):
        lo = lax.broadcasted_iota(jnp.int32, (1, LANES), 1) < head_dim

        def epilogue(b, zb):
            sq = zb * zb
            s_lo = jnp.sum(jnp.where(lo, sq, 0.0), axis=-1, keepdims=True)
            s_hi = jnp.sum(jnp.where(lo, 0.0, sq), axis=-1, keepdims=True)
            ms = jnp.where(lo, s_lo, s_hi) * (1.0 / head_dim)
            store_z(b, zb * lax.rsqrt(ms + RMS_EPS) * w_row * scale)
        return epilogue

    def store_u(b, zb):
        us_ref[b] = zb
        chunks = zb.shape[0] // SSM_T
        for t in range(SSM_T):
            u_ref[b, :, t * LANES:(t + 1) * LANES] = (
                us_ref[b, pl.ds(t, chunks, stride=SSM_T), :].astype(u_ref.dtype))

    pl.when(j == 0)(lambda: project(store_u))
    pl.when(j == 1)(lambda: project(seg_norm(qw_ref[...], head_dim ** -0.5 * LOG2_E)))
    pl.when(j == 2)(lambda: project(seg_norm(kw_ref[...], 1.0)))
    pl.when(j == 3)(lambda: project(store_z))
    pl.when(j >= 4)(lambda: project(lambda b, zb: store_z(b, jax.nn.sigmoid(zb))))


def _in_proj(x2, nw, w, qw2, kw2, *, layer, head_dim, tm, tn):
    n, d = x2.shape
    width = w.shape[2]
    blocks = tm * d * 4 + d * 4 + d * tn * 2 + 2 * tm * tn * 2
    return pl.pallas_call(
        functools.partial(_in_proj_kernel, head_dim=head_dim),
        out_shape=(jax.ShapeDtypeStruct((tn // LANES, n // SSM_T, SSM_T * LANES), BF16),
                   jax.ShapeDtypeStruct((n, width - tn), BF16)),
        grid=(n // tm, width // tn),
        in_specs=[
            pl.BlockSpec((tm, d), lambda i, j: (i, 0)),
            pl.BlockSpec((1, d), lambda i, j: (0, 0)),
            pl.BlockSpec((None, d, tn), lambda i, j: (layer, 0, j)),
            pl.BlockSpec((1, LANES), lambda i, j: (0, 0)),
            pl.BlockSpec((1, LANES), lambda i, j: (0, 0)),
        ],
        out_specs=(pl.BlockSpec((tn // LANES, tm // SSM_T, SSM_T * LANES), lambda i, j: (0, i, 0)),
                   pl.BlockSpec((tm, tn), lambda i, j: (i, jnp.maximum(j - 1, 0)))),
        scratch_shapes=[pltpu.VMEM((tm, d), BF16), pltpu.VMEM((tn // LANES, tm, LANES), F32)],
        compiler_params=_params(("parallel", "arbitrary"), blocks,
                                tm * d * 2 + tm * tn * 4 + 6 * tm * PROJ_CHUNK * 4),
        name="in_proj",
    )(x2, nw, w, qw2, kw2)


def _ssm_prep_kernel(lr_ref, li_ref, ls_ref, bcat_ref, bswap_ref, ccat_ref, cswap_ref, d_ref,
                     lrl_ref, lil_ref, lsl_ref, m_ref, e_ref, ft_ref, p_ref, q_ref, ft0_scr,
                     *, state, group, nsteps):
    gl = LANES // group
    lane = lax.broadcasted_iota(jnp.int32, (1, LANES), 1)
    sign = jnp.where(lane < state, -1.0, 1.0).astype(F32)
    lr = jnp.minimum(lr_ref[0], LAM_RE_MAX)
    li = li_ref[0]
    dt = jnp.exp(ls_ref[0])
    mag = jnp.exp(lr * dt)
    ab_r = mag * jnp.cos(li * dt)
    ab_i = mag * jnp.sin(li * dt)
    den = lr * lr + li * li
    nr = ab_r - 1.0
    fr = (nr * lr + ab_i * li) / den
    fi = (ab_i * lr - nr * li) / den
    bcat = bcat_ref[0]
    bswap = bswap_ref[0]
    bb_cat = fr * bcat + sign * fi * bswap
    bb_swap = fr * bswap - sign * fi * bcat
    ccat = ccat_ref[0]
    cswap = cswap_ref[0]

    row_group = lax.broadcasted_iota(jnp.int32, (LANES, 1), 0) // group

    def store_block_diag(ref, block, value):
        for g in range(gl):
            ref[0, block * LANES:(block + 1) * LANES, g * LANES:(g + 1) * LANES] = (
                jnp.where(row_group == g, value, 0.0).astype(ref.dtype))

    pw_r = jnp.ones((LANES, LANES), F32)
    pw_i = jnp.zeros((LANES, LANES), F32)
    for n in range(SSM_T + 1):
        ft_blk = pw_r * ccat + pw_i * cswap
        if n < SSM_T:
            ft0_scr[n * LANES:(n + 1) * LANES, :] = ft_blk
            store_block_diag(e_ref, SSM_T - 1 - n, pw_r * bb_cat + sign * pw_i * bb_swap)
        if n >= 1:
            store_block_diag(ft_ref, n - 1, ft_blk)
        pw_r, pw_i = pw_r * ab_r - pw_i * ab_i, pw_r * ab_i + pw_i * ab_r

    width = SSM_T * LANES
    r = _dot_nt(bb_cat, ft0_scr[...], precision=lax.Precision.HIGHEST)
    col = lax.broadcasted_iota(jnp.int32, (LANES, width), 1)
    row = lax.broadcasted_iota(jnp.int32, (LANES, width), 0)
    r = jnp.where(row // group == (col % LANES) // group, r, 0.0)
    r = r + jnp.where(col == row, jnp.tile(d_ref[0], (1, SSM_T)), 0.0)
    for j in range(SSM_T):
        rows_j = slice(j * LANES, (j + 1) * LANES)
        if j:
            m_ref[0, rows_j, 0:j * LANES] = jnp.zeros((LANES, j * LANES), m_ref.dtype)
        m_ref[0, rows_j, j * LANES:width] = r[:, 0:width - j * LANES].astype(m_ref.dtype)

    lrl = jnp.minimum(lrl_ref[0], LAM_RE_MAX)
    dtl = jnp.exp(lsl_ref[0])
    lane_l = lax.broadcasted_iota(jnp.int32, lrl.shape, 1)
    sign_l = jnp.where(lane_l % LANES < state, -1.0, 1.0).astype(F32)
    steps = lax.broadcasted_iota(jnp.int32, (8, 1), 0)
    nd = (SSM_T * jnp.left_shift(1, jnp.minimum(steps, nsteps))).astype(F32)
    mag_l = jnp.exp(lrl * dtl * nd)
    ang_l = lil_ref[0] * dtl * nd
    p_ref[0] = mag_l * jnp.cos(ang_l)
    q_ref[0] = sign_l * mag_l * jnp.sin(ang_l)


def _ssm_prep(rows_in, lanes_in, *, state, group, nsteps):
    nb = rows_in[0].shape[0]
    gl = LANES // group
    width = SSM_T * LANES
    bspec = lambda *shape: pl.BlockSpec((1,) + shape, lambda i: (i, 0, 0))
    blocks = width * width * 2 + 2 * width * gl * LANES * 2 + 8 * LANES * LANES * 4
    return pl.pallas_call(
        functools.partial(_ssm_prep_kernel, state=state, group=group, nsteps=nsteps),
        out_shape=(
            jax.ShapeDtypeStruct((nb, width, width), BF16),
            jax.ShapeDtypeStruct((nb, width, gl * LANES), BF16),
            jax.ShapeDtypeStruct((nb, width, gl * LANES), BF16),
            jax.ShapeDtypeStruct((nb, 8, gl * LANES), F32),
            jax.ShapeDtypeStruct((nb, 8, gl * LANES), F32),
        ),
        grid=(nb,),
        in_specs=[bspec(LANES, LANES)] * 8 + [bspec(1, gl * LANES)] * 3,
        out_specs=(bspec(width, width), bspec(width, gl * LANES), bspec(width, gl * LANES),
                   bspec(8, gl * LANES), bspec(8, gl * LANES)),
        scratch_shapes=[pltpu.VMEM((width, LANES), F32)],
        compiler_params=_params(("parallel",), blocks, width * LANES * 4 + 6 * LANES * width * 4),
        name="ssm_prep",
    )(*rows_in, *lanes_in)


SSM_COL_SPLIT = 4


def _ssm_kernel(u_ref, m_ref, e_ref, ft_ref, p_ref, q_ref, o_ref, ys_ref, ym_ref,
                *, kchunks, nsteps, state):
    u = u_ref[0]
    rows, width = u.shape
    cw = width // SSM_COL_SPLIT
    s = jnp.dot(u, e_ref[0], preferred_element_type=F32)
    for c in range(SSM_COL_SPLIT):
        kk = (c + 1) * cw
        ym_ref[:, c * cw:kk] = jnp.dot(u[:, 0:kk], m_ref[0, 0:kk, c * cw:kk],
                                       preferred_element_type=F32)
    kidx = lax.broadcasted_iota(jnp.int32, (rows, 1), 0) & (kchunks - 1)
    s_prev = []
    for g in range(s.shape[1] // LANES):
        lanes = slice(g * LANES, (g + 1) * LANES)
        sg = s[:, lanes]
        for i in range(nsteps):
            d = 1 << i
            x = jnp.where(kidx >= d, pltpu.roll(sg, d, axis=0), 0.0)
            sg = sg + p_ref[0, i:i + 1, lanes] * x + q_ref[0, i:i + 1, lanes] * pltpu.roll(x, state, axis=1)
        s_prev.append(jnp.where(kidx >= 1, pltpu.roll(sg, 1, axis=0), 0.0).astype(BF16))
    s_prev = jnp.concatenate(s_prev, axis=1)
    for c in range(SSM_COL_SPLIT):
        cols = slice(c * cw, (c + 1) * cw)
        y = jax.nn.gelu(ym_ref[:, cols] + _dot_nt(s_prev, ft_ref[0, cols, :]))
        for tl in range(cw // LANES):
            t = c * (cw // LANES) + tl
            ys_ref[pl.ds(t, rows, stride=SSM_T), :] = y[:, tl * LANES:(tl + 1) * LANES]
    o_ref[0] = ys_ref[...].astype(o_ref.dtype)


def _ssm(u4, m, e, ft, p, q, *, kchunks, nsteps, rt):
    nb, rows, width = u4.shape
    sl = e.shape[2]
    blocks = 2 * rt * width * 2 + width * width * 2 + 2 * width * sl * 2 + 2 * 8 * sl * 4
    return pl.pallas_call(
        functools.partial(_ssm_kernel, kchunks=kchunks, nsteps=nsteps, state=LANES // 2),
        out_shape=jax.ShapeDtypeStruct((nb, rows * SSM_T, LANES), BF16),
        grid=(nb, rows // rt),
        in_specs=[pl.BlockSpec((1, rt, width), lambda b, r: (b, r, 0)),
                  pl.BlockSpec((1, width, width), lambda b, r: (b, 0, 0)),
                  pl.BlockSpec((1, width, sl), lambda b, r: (b, 0, 0)),
                  pl.BlockSpec((1, width, sl), lambda b, r: (b, 0, 0)),
                  pl.BlockSpec((1, 8, sl), lambda b, r: (b, 0, 0)),
                  pl.BlockSpec((1, 8, sl), lambda b, r: (b, 0, 0))],
        out_specs=pl.BlockSpec((1, rt * SSM_T, LANES), lambda b, r: (b, r, 0)),
        scratch_shapes=[pltpu.VMEM((rt * SSM_T, LANES), F32), pltpu.VMEM((rt, width), F32)],
        compiler_params=_params(("parallel", "arbitrary"), blocks,
                                2 * rt * width * 4 + 4 * rt * sl * 4
                                + 4 * rt * (width // SSM_COL_SPLIT) * 4),
        name="ssm",
    )(u4, m, e, ft, p, q)


def _bias_kernel(tab_ref, idx_ref, o_ref, *, n_buckets, far_bucket):
    h = pl.program_id(0)
    far = tab_ref[far_bucket, h]
    blocks = []
    for t in range(2):
        idx = idx_ref[t]
        acc = jnp.full(idx.shape, MASK_VALUE, F32)
        for b in range(n_buckets):
            acc = jnp.where(idx == b, (tab_ref[b, h] - far) * LOG2_E, acc)
        blocks.append(acc)
    prev_blk, near_blk = blocks
    zero_blk = jnp.zeros_like(near_blk)
    mask_blk = jnp.full_like(near_blk, MASK_VALUE)
    nb = o_ref.shape[2] // LANES
    for r in range(nb):
        for c in range(nb):
            where = (slice(r * LANES, (r + 1) * LANES), slice(c * LANES, (c + 1) * LANES))
            o_ref[(0, 0) + where] = prev_blk if (r, c) == (nb - 1, 0) else zero_blk
            o_ref[(0, 1) + where] = (near_blk if r == c else prev_blk if r == c - 1
                                     else mask_blk if r > c else zero_blk)


def _t5_bucket_np(rel, n_buckets):
    nb = n_buckets // 2
    ret = np.where(rel > 0, nb, 0)
    n = np.abs(rel)
    max_exact = nb // 2
    nf = np.maximum(n, 1).astype(np.float32)
    large = max_exact + (np.log(nf / np.float32(max_exact)) / np.float32(math.log(MAX_DISTANCE / max_exact))
                         * np.float32(nb - max_exact)).astype(np.int32)
    large = np.minimum(large, nb - 1)
    return (ret + np.where(n < max_exact, n, large)).astype(np.int32)


def _bias_tiles(rel_bias, *, tq):
    n_buckets, n_heads = rel_bias.shape
    c = np.arange(LANES)[:, None]
    r = np.arange(LANES)[None, :]
    prev = _t5_bucket_np(c - r - LANES, n_buckets)
    near = np.where(c // CHUNK <= r // CHUNK, _t5_bucket_np(c - r, n_buckets), -1)
    idx = np.stack([prev, near])
    far_bucket = int(_t5_bucket_np(np.array(-MAX_DISTANCE), n_buckets))
    assert LANES >= MAX_DISTANCE and LANES % CHUNK == 0 and tq % LANES == 0
    return pl.pallas_call(
        functools.partial(_bias_kernel, n_buckets=n_buckets, far_bucket=far_bucket),
        out_shape=jax.ShapeDtypeStruct((n_heads, 2, tq, tq), F32),
        grid=(n_heads,),
        in_specs=[pl.BlockSpec(memory_space=pltpu.SMEM),
                  pl.BlockSpec((2, LANES, LANES), lambda h: (0, 0, 0))],
        out_specs=pl.BlockSpec((1, 2, tq, tq), lambda h: (h, 0, 0, 0)),
        compiler_params=pltpu.CompilerParams(dimension_semantics=("parallel",)),
        name="attn_bias",
    )(rel_bias, jnp.asarray(idx))


ONES_ROWS = 16
FIXED_SHIFT_MAX = 60.0
BOUND_SLACK = 1.02


def _attn_kernel(q_ref, k_ref, v_ref, bias_ref, lq1_ref, lk1_ref, lq2_ref, lk2_ref, sw_ref,
                 o_ref, vt_scr, qm_scr, shift_scr, st_a, st_b, m_scr, acc_scr, *, tq, head_dim, lambda_init):
    seq, vd = v_ref.shape
    nq = seq // tq

    def block(j):
        return slice(j * tq, (j + 1) * tq)

    lo = lax.broadcasted_iota(jnp.int32, (1, LANES), 1) < head_dim
    halves = (lax.broadcasted_iota(jnp.int32, (8, LANES), 0)
              == (lax.broadcasted_iota(jnp.int32, (8, LANES), 1) >= head_dim).astype(jnp.int32)
              ).astype(BF16)
    qsq, ksq = [], []
    for c in range(nq):
        vt_scr[0:vd, block(c)] = v_ref[block(c), :].T
        q = q_ref[block(c), :]
        zero = jnp.zeros_like(q)
        qm_scr[0, block(c), :] = jnp.where(lo, q, zero)
        qm_scr[1, block(c), :] = jnp.where(lo, zero, q)
        k = k_ref[block(c), :]
        qsq.append(_dot_nt(halves, q * q))
        ksq.append(_dot_nt(halves, k * k))
    vt_scr[vd:, :] = jnp.ones((ONES_ROWS, seq), BF16)

    k_max = jnp.max(jnp.concatenate(ksq, axis=1), axis=1, keepdims=True)
    bias_max = jnp.maximum(jnp.maximum(jnp.max(bias_ref[0, 1, 0:LANES, 0:LANES]),
                                       jnp.max(bias_ref[0, 0, tq - LANES:tq, 0:LANES])), 0.0)
    bound = jnp.sqrt(jnp.concatenate(qsq, axis=1) * k_max) * BOUND_SLACK + bias_max
    shift_scr[0] = bound[0:1]
    shift_scr[1] = bound[1:2]
    fixed_shift_ok = jnp.max(bound[0:2]) <= FIXED_SHIFT_MAX

    s1 = jnp.sum(lq1_ref[...] * lk1_ref[...], axis=-1, keepdims=True)
    s2 = jnp.sum(lq2_ref[...] * lk2_ref[...], axis=-1, keepdims=True)
    lam = jnp.exp(s1) - jnp.exp(s2) + lambda_init

    def finalize(rows, a1, a2):
        ot = a1[0:vd] * (1.0 / a1[vd:vd + 1]) - lam * (a2[0:vd] * (1.0 / a2[vd:vd + 1]))
        y = _rms_norm(ot.T, sw_ref[...], SUBLN_EPS) * (1.0 - lambda_init)
        o_ref[rows, :] = y.astype(o_ref.dtype)


    @pl.when(fixed_shift_ok)
    def _():
        def scores(item, st_ref):
            i, g, k0, nk, q0, nqr = item
            kb = k_ref[g * tq + k0:g * tq + k0 + nk, :]
            for m in range(2):
                st_ref[m, 0:nk, 0:nqr] = _dot_nt(kb, qm_scr[m, i * tq + q0:i * tq + q0 + nqr, :])

        def accumulate(item, st_ref):
            i, g, k0, nk, q0, nqr = item
            vt = vt_scr[:, g * tq + k0:g * tq + k0 + nk]
            for m in range(2):
                st = st_ref[m, 0:nk, 0:nqr]
                if g >= i - 1:
                    st = st + bias_ref[0, 1 if g == i else 0, k0:k0 + nk, q0:q0 + nqr]
                p = jnp.exp2(st - shift_scr[m, :, i * tq + q0:i * tq + q0 + nqr]).astype(BF16)
                pv = jnp.dot(vt, p, preferred_element_type=F32)
                if g == 0 and k0 == 0:
                    acc_scr[i, m] = pv
                else:
                    acc_scr[i, m, :, q0:q0 + nqr] += pv

        half = tq // 2
        assert half % CHUNK == 0
        work = []
        for i in range(nq):
            work += [(i, g, 0, tq, 0, tq) for g in range(i)]
            work += [(i, i, 0, half, 0, tq), (i, i, half, half, half, half)]
        bufs = (st_a, st_b)
        scores(work[0], bufs[0])
        for t, item in enumerate(work):
            if t + 1 < len(work):
                scores(work[t + 1], bufs[(t + 1) % 2])
            accumulate(item, bufs[t % 2])
            if item[1] == item[0] and item[2] > 0:
                finalize(block(item[0]), acc_scr[item[0], 0], acc_scr[item[0], 1])

    @pl.when(jnp.logical_not(fixed_shift_ok))
    def _():
        def update(i, g, bias):
            kb = k_ref[pl.ds(pl.multiple_of(g * tq, tq), tq), :]
            vt = vt_scr[:, pl.ds(pl.multiple_of(g * tq, tq), tq)]
            for m in range(2):
                st = _dot_nt(kb, qm_scr[m, pl.ds(pl.multiple_of(i * tq, tq), tq), :])
                if bias is not None:
                    st = st + bias
                m_prev = m_scr[m]
                m_new = jnp.maximum(m_prev, jnp.max(st, axis=0, keepdims=True))
                p = jnp.exp2(st - m_new).astype(BF16)
                acc_scr[0, m] = (jnp.exp2(m_prev - m_new) * acc_scr[0, m]
                                 + jnp.dot(vt, p, preferred_element_type=F32))
                m_scr[m] = m_new

        def query_block(i, carry):
            m_scr[...] = jnp.full(m_scr.shape, MASK_VALUE, F32)
            acc_scr[0] = jnp.zeros(acc_scr.shape[1:], F32)
            update(i, i, bias_ref[0, 1])
            pl.when(i >= 1)(lambda: update(i, i - 1, bias_ref[0, 0]))
            lax.fori_loop(0, jnp.maximum(i - 1, 0), lambda g, c: (update(i, g, None), c)[1], 0)
            finalize(pl.ds(pl.multiple_of(i * tq, tq), tq), acc_scr[0, 0], acc_scr[0, 1])
            return carry

        lax.fori_loop(0, nq, query_block, 0)


def _attention(z, bias, lq1, lk1, lq2, lk2, sw, *, batch, seq, n_heads, head_dim, tq,
               q_col, k_col, v_col, lambda_init):
    n = z.shape[0]
    nq = seq // tq
    acc_rows = LANES + ONES_ROWS
    vec = pl.BlockSpec((1, head_dim), lambda b, h: (0, 0))
    blocks = 4 * seq * LANES * 2 + 2 * tq * tq * 4
    scratch = (acc_rows * seq * 2 + 2 * seq * LANES * 2 + 2 * 8 * seq * 4 + 2 * 8 * tq * 4
               + nq * 2 * acc_rows * tq * 4)
    return pl.pallas_call(
        functools.partial(_attn_kernel, tq=tq, head_dim=head_dim, lambda_init=lambda_init),
        out_shape=jax.ShapeDtypeStruct((n, n_heads * LANES), BF16),
        grid=(batch, n_heads),
        in_specs=[
            pl.BlockSpec((seq, LANES), lambda b, h: (b, q_col + h)),
            pl.BlockSpec((seq, LANES), lambda b, h: (b, k_col + h)),
            pl.BlockSpec((seq, LANES), lambda b, h: (b, v_col + h)),
            pl.BlockSpec((1, 2, tq, tq), lambda b, h: (h, 0, 0, 0)),
            vec, vec, vec, vec,
            pl.BlockSpec((1, LANES), lambda b, h: (0, 0)),
        ],
        out_specs=pl.BlockSpec((seq, LANES), lambda b, h: (b, h)),
        scratch_shapes=[pltpu.VMEM((acc_rows, seq), BF16),
                        pltpu.VMEM((2, seq, LANES), BF16),
                        pltpu.VMEM((2, 1, seq), F32),
                        pltpu.VMEM((2, tq, tq), F32),
                        pltpu.VMEM((2, tq, tq), F32),
                        pltpu.VMEM((2, 1, tq), F32),
                        pltpu.VMEM((nq, 2, acc_rows, tq), F32)],
        compiler_params=_params(("parallel", "parallel"), blocks, scratch + 8 * tq * tq * 4),
        name="attention",
    )(z, z, z, bias, lq1, lk1, lq2, lk2, sw)


def _merge_kernel(yg_ref, wglu_ref, bglu_ref, ya_ref, gs_ref, ga_ref, ps_ref, pa_ref, wo_ref,
                  x_hbm, o_ref, ys_ref, m_ref, x_sem):
    i = pl.program_id(0)
    j = pl.program_id(1)
    nblk = yg_ref.shape[0]
    per_chunk = PROJ_CHUNK // LANES
    tm = o_ref.shape[0]

    @pl.when(j == 0)
    def _():
        x_copy = pltpu.make_async_copy(x_hbm.at[pl.ds(pl.multiple_of(i * tm, tm), tm), :],
                                       o_ref, x_sem)
        x_copy.start()
        yg = jnp.concatenate([yg_ref[b] for b in range(nblk)], axis=1)
        for c in range(nblk // per_chunk):
            cols = slice(c * PROJ_CHUNK, (c + 1) * PROJ_CHUNK)
            g = jnp.dot(yg, wglu_ref[:, cols], preferred_element_type=F32) + bglu_ref[:, cols]
            ys_ref[:, cols] = (yg[:, cols].astype(F32) * jax.nn.sigmoid(g)).astype(BF16)
        x_copy.wait()

    ys = ys_ref[...]
    ya = ya_ref[...]
    for c in range(m_ref.shape[1] // PROJ_CHUNK):
        cols = slice(c * PROJ_CHUNK, (c + 1) * PROJ_CHUNK)
        m = (gs_ref[:, cols].astype(F32) * jnp.dot(ys, ps_ref[:, cols], preferred_element_type=F32)
             + ga_ref[:, cols].astype(F32) * jnp.dot(ya, pa_ref[:, cols], preferred_element_type=F32))
        m_ref[:, cols] = m.astype(BF16)
    m = m_ref[...]
    cw = o_ref.shape[1] // DOWN_SPLIT
    for c in range(DOWN_SPLIT):
        cols = slice(c * cw, (c + 1) * cw)
        o_ref[:, cols] += jnp.dot(m, wo_ref[:, cols], preferred_element_type=F32)


def _merge(yg, wglu, bglu, ya, z, ps, pa, wo, x2, *, layer, gs_col, ga_col, tm, tj):
    n, d = x2.shape
    ws = yg.shape[0] * LANES
    wa = ya.shape[1]
    blocks = (tm * ws * 2 + ws * ws * 2 + ws * 4 + tm * wa * 2 + 2 * tm * tj * 2
              + ws * tj * 2 + wa * tj * 2 + tj * d * 2 + tm * d * 4)
    return pl.pallas_call(
        _merge_kernel,
        out_shape=jax.ShapeDtypeStruct((n, d), F32),
        grid=(n // tm, d // tj),
        in_specs=[
            pl.BlockSpec((ws // LANES, tm, LANES), lambda i, j: (0, i, 0)),
            pl.BlockSpec((None, ws, ws), lambda i, j: (layer, 0, 0)),
            pl.BlockSpec((1, ws), lambda i, j: (0, 0)),
            pl.BlockSpec((tm, wa), lambda i, j: (i, 0)),
            pl.BlockSpec((tm, tj), lambda i, j: (i, gs_col + j)),
            pl.BlockSpec((tm, tj), lambda i, j: (i, ga_col + j)),
            pl.BlockSpec((None, ws, tj), lambda i, j: (layer, 0, j)),
            pl.BlockSpec((None, wa, tj), lambda i, j: (layer, 0, j)),
            pl.BlockSpec((None, tj, d), lambda i, j: (layer, j, 0)),
            pl.BlockSpec(memory_space=pl.ANY),
        ],
        out_specs=pl.BlockSpec((tm, d), lambda i, j: (i, 0)),
        scratch_shapes=[pltpu.VMEM((tm, ws), BF16), pltpu.VMEM((tm, tj), BF16),
                        pltpu.SemaphoreType.DMA],
        compiler_params=_params(("parallel", "arbitrary"), blocks,
                                tm * ws * 2 + tm * tj * 2 + 8 * tm * PROJ_CHUNK * 4),
        name="merge",
    )(yg, wglu, bglu, ya, z, z, ps, pa, wo, x2)


def _ffn_kernel(x_hbm, nw_ref, wg_ref, wu_ref, wd_ref, o_ref, h_ref, a_ref, x_buf, x_sem):
    i = pl.program_id(0)
    j = pl.program_id(1)
    tm = o_ref.shape[0]

    def x_copy(tile):
        return pltpu.make_async_copy(x_hbm.at[pl.ds(pl.multiple_of(tile * tm, tm), tm), :],
                                     x_buf, x_sem)

    @pl.when(j == 0)
    def _():
        pl.when(i == 0)(lambda: x_copy(0).start())
        x_copy(i).wait()
        x = x_buf[...]
        h_ref[...] = _rms_norm(x, nw_ref[...], RMS_EPS).astype(BF16)
        o_ref[...] = x

    @pl.when(jnp.logical_and(j == 1, i + 1 < pl.num_programs(0)))
    def _():
        x_copy(i + 1).start()

    h = h_ref[...]
    for c in range(a_ref.shape[1] // PROJ_CHUNK):
        cols = slice(c * PROJ_CHUNK, (c + 1) * PROJ_CHUNK)
        g = jnp.dot(h, wg_ref[:, cols], preferred_element_type=F32)
        u = jnp.dot(h, wu_ref[:, cols], preferred_element_type=F32)
        a_ref[:, cols] = (jax.nn.silu(g) * u).astype(BF16)
    a = a_ref[...]
    cw = o_ref.shape[1] // DOWN_SPLIT
    for c in range(DOWN_SPLIT):
        cols = slice(c * cw, (c + 1) * cw)
        o_ref[:, cols] += jnp.dot(a, wd_ref[:, cols], preferred_element_type=F32)


def _ffn(x2, nw, wg, wu, wd, *, layer, tm, tf):
    n, d = x2.shape
    dff = wg.shape[2]
    assert dff // tf >= 2
    blocks = tm * d * 4 + d * 4 + 3 * d * tf * 2
    return pl.pallas_call(
        _ffn_kernel,
        out_shape=jax.ShapeDtypeStruct((n, d), F32),
        grid=(n // tm, dff // tf),
        in_specs=[
            pl.BlockSpec(memory_space=pl.ANY),
            pl.BlockSpec((1, d), lambda i, j: (0, 0)),
            pl.BlockSpec((None, d, tf), lambda i, j: (layer, 0, j)),
            pl.BlockSpec((None, d, tf), lambda i, j: (layer, 0, j)),
            pl.BlockSpec((None, tf, d), lambda i, j: (layer, j, 0)),
        ],
        out_specs=pl.BlockSpec((tm, d), lambda i, j: (i, 0)),
        scratch_shapes=[pltpu.VMEM((tm, d), BF16), pltpu.VMEM((tm, tf), BF16),
                        pltpu.VMEM((tm, d), F32), pltpu.SemaphoreType.DMA],
        compiler_params=_params(("arbitrary", "arbitrary"), blocks,
                                tm * d * 2 + tm * tf * 2 + tm * d * 4 + 6 * tm * PROJ_CHUNK * 4),
        name="ffn",
    )(x2, nw, wg, wu, wd)


def _largest_tile(total, target, quantum):
    t = min(total, target)
    while total % t or t % quantum:
        t -= quantum
    return t


def kernel(x, norm1_w, w_in, lam_re, lam_im, log_step, ssm_b_re, ssm_b_im, ssm_c_re, ssm_c_im, ssm_d, w_glu, b_glu, q_norm_w, k_norm_w, lambda_q1, lambda_k1, lambda_q2, lambda_k2, subln_w, w_proj_ssm, w_proj_attn, w_out, rel_bias, norm2_w, w_ffn_gate, w_ffn_up, w_ffn_down):
    batch, seq, d_model = x.shape
    depth = w_in.shape[0]
    n = batch * seq
    _, groups, state = lam_re.shape
    group = ssm_b_re.shape[-1]
    ssm_width = groups * group
    head_dim = q_norm_w.shape[-1]
    v_dim = subln_w.shape[-1]
    attn_width = w_proj_attn.shape[1]
    n_heads = attn_width // v_dim
    d_ff = w_ffn_gate.shape[-1]

    tn = ssm_width
    assert ssm_width == attn_width and d_model % tn == 0
    assert w_in.shape[2] == ssm_width + 3 * attn_width + 2 * d_model
    assert 2 * head_dim == LANES and v_dim == LANES and 2 * state == LANES
    assert SSM_T * group == V7X_MXU_DIM
    kchunks = seq // SSM_T
    nsteps = kchunks.bit_length() - 1
    assert seq % SSM_T == 0 and kchunks == 1 << nsteps and nsteps <= 8

    tm_proj = _largest_tile(n, 1024, 16 * SSM_T)
    tm_merge = _largest_tile(n, 1024, 16)
    tm_ffn = _largest_tile(n, 1024, 8)
    tq = _largest_tile(seq, 512, MAX_DISTANCE)
    assert tq % CHUNK == 0
    tj = _largest_tile(d_model, 512, PROJ_CHUNK)
    tf = _largest_tile(d_ff, 512, LANES)

    q_col = 0
    k_col = q_col + attn_width // LANES
    v_col = k_col + attn_width // LANES
    gs_col = 3 * attn_width // tj
    ga_col = gs_col + d_model // tj

    bias = _bias_tiles(rel_bias, tq=tq)
    x2 = x.reshape(n, d_model)
    nblk = ssm_width // LANES
    seqs_per_tile = math.gcd(batch, max(1, SSM_ROWS // kchunks))
    rt = kchunks * seqs_per_tile
    dup = lambda a: jnp.concatenate([a, a], axis=-1)
    per_row = lambda a: jnp.repeat(a, group, axis=0).reshape(nblk, LANES, LANES)
    per_lane = lambda a: a.reshape(nblk, 1, -1)
    as_rows = lambda a: a.reshape(nblk, LANES, LANES)

    w_in, w_glu, w_proj_ssm, w_proj_attn, w_out, w_ffn_gate, w_ffn_up, w_ffn_down = (
        w.astype(BF16) for w in (w_in, w_glu, w_proj_ssm, w_proj_attn, w_out,
                                 w_ffn_gate, w_ffn_up, w_ffn_down))

    for l in range(depth):
        lambda_init = 0.8 - 0.6 * math.exp(-0.3 * l)

        qw2 = jnp.tile(q_norm_w[l], LANES // head_dim)[None, :]
        kw2 = jnp.tile(k_norm_w[l], LANES // head_dim)[None, :]
        u3, z = _in_proj(x2, norm1_w[l][None, :], w_in, qw2, kw2,
                         layer=l, head_dim=head_dim, tm=tm_proj, tn=tn)

        lam_re2, lam_im2 = dup(lam_re[l]), dup(lam_im[l])
        step2 = jnp.broadcast_to(log_step[l][:, None], (groups, 2 * state))
        b_re_t = jnp.swapaxes(ssm_b_re[l], 1, 2)
        b_im_t = jnp.swapaxes(ssm_b_im[l], 1, 2)
        c_re, c_im = ssm_c_re[l], ssm_c_im[l]
        d_rows = jnp.broadcast_to(ssm_d[l][:, None], (ssm_width, LANES)).reshape(nblk, LANES, LANES)
        m_op, e_op, ft_op, p_op, q_op = _ssm_prep(
            (per_row(lam_re2), per_row(lam_im2), per_row(step2),
             as_rows(jnp.concatenate([b_re_t, b_im_t], axis=-1)),
             as_rows(jnp.concatenate([b_im_t, b_re_t], axis=-1)),
             as_rows(jnp.concatenate([c_re, -c_im], axis=-1)),
             as_rows(jnp.concatenate([-c_im, -c_re], axis=-1)), d_rows),
            (per_lane(lam_re2), per_lane(lam_im2), per_lane(step2)),
            state=state, group=group, nsteps=nsteps)
        yg = _ssm(u3, m_op, e_op, ft_op, p_op, q_op, kchunks=kchunks, nsteps=nsteps, rt=rt)

        ya = _attention(z, bias, lambda_q1[l][None, :], lambda_k1[l][None, :],
                        lambda_q2[l][None, :], lambda_k2[l][None, :], subln_w[l][None, :],
                        batch=batch, seq=seq, n_heads=n_heads, head_dim=head_dim, tq=tq,
                        q_col=q_col, k_col=k_col, v_col=v_col, lambda_init=lambda_init)

        x2 = _merge(yg, w_glu, b_glu[l][None, :], ya, z, w_proj_ssm, w_proj_attn, w_out, x2,
                    layer=l, gs_col=gs_col, ga_col=ga_col, tm=tm_merge, tj=tj)
        x2 = _ffn(x2, norm2_w[l][None, :], w_ffn_gate, w_ffn_up, w_ffn_down,
                  layer=l, tm=tm_ffn, tf=tf)

    return x2.reshape(batch, seq, d_model)
```

```python
import functools
import math

import numpy as np
import jax
import jax.numpy as jnp
from jax import lax
from jax.experimental import pallas as pl
from jax.experimental.pallas import tpu as pltpu

F32 = jnp.float32
BF16 = jnp.bfloat16

CHUNK = 64
MAX_DISTANCE = 128
RMS_EPS = 1e-6
SUBLN_EPS = 1e-5
LAM_RE_MAX = -1e-4
MASK_VALUE = -1e30
LOG2_E = math.log2(math.e)

LANES = 128
V7X_MXU_DIM = 256
V7X_VMEM_BYTES = 64 * 1024 * 1024

PROJ_CHUNK = V7X_MXU_DIM
DOWN_SPLIT = 4
SSM_T = 16
SSM_ROWS = 256


def _vmem_limit(block_bytes, extra_bytes):
    return int(min(2 * block_bytes + extra_bytes + (4 << 20), V7X_VMEM_BYTES - (6 << 20)))


def _params(semantics, block_bytes, extra_bytes):
    return pltpu.CompilerParams(dimension_semantics=semantics,
                                vmem_limit_bytes=_vmem_limit(block_bytes, extra_bytes))


def _rms_norm(x, w, eps):
    ms = jnp.mean(x * x, axis=-1, keepdims=True)
    return x * lax.rsqrt(ms + eps) * w


def _dot_nt(a, b, **kw):
    return lax.dot_general(a, b, (((1,), (1,)), ((), ())), preferred_element_type=F32, **kw)


def _in_proj_kernel(x_ref, nw_ref, w_ref, qw_ref, kw_ref, u_ref, o_ref, h_ref, us_ref, *, head_dim):
    j = pl.program_id(1)

    @pl.when(j == 0)
    def _():
        h_ref[...] = _rms_norm(x_ref[...], nw_ref[...], RMS_EPS).astype(BF16)

    tn = w_ref.shape[1]

    def project(epilogue):
        h = h_ref[...]
        for c in range(tn // PROJ_CHUNK):
            zc = jnp.dot(h, w_ref[:, c * PROJ_CHUNK:(c + 1) * PROJ_CHUNK], preferred_element_type=F32)
            for s in range(PROJ_CHUNK // LANES):
                epilogue(c * (PROJ_CHUNK // LANES) + s, zc[:, s * LANES:(s + 1) * LANES])

    def store_z(b, y):
        o_ref[:, b * LANES:(b + 1) * LANES] = y.astype(o_ref.dtype)

    def seg_norm(w_row, scale):
        lo = lax.broadcasted_iota(jnp.int32, (1, LANES), 1) < head_dim

        def epilogue(b, zb):
            sq = zb * zb
            s_lo = jnp.sum(jnp.where(lo, sq, 0.0), axis=-1, keepdims=True)
            s_hi = jnp.sum(jnp.where(lo, 0.0, sq), axis=-1, keepdims=True)
            ms = jnp.where(lo, s_lo, s_hi) * (1.0 / head_dim)
            store_z(b, zb * lax.rsqrt(ms + RMS_EPS) * w_row * scale)
        return epilogue

    def store_u(b, zb):
        us_ref[b] = zb
        chunks = zb.shape[0] // SSM_T
        for t in range(SSM_T):
            u_ref[b, :, t * LANES:(t + 1) * LANES] = (
                us_ref[b, pl.ds(t, chunks, stride=SSM_T), :].astype(u_ref.dtype))

    pl.when(j == 0)(lambda: project(store_u))
    pl.when(j == 1)(lambda: project(seg_norm(qw_ref[...], head_dim ** -0.5 * LOG2_E)))
    pl.when(j == 2)(lambda: project(seg_norm(kw_ref[...], 1.0)))
    pl.when(j == 3)(lambda: project(store_z))
    pl.when(j >= 4)(lambda: project(lambda b, zb: store_z(b, jax.nn.sigmoid(zb))))


def _in_proj(x2, nw, w, qw2, kw2, *, layer, head_dim, tm, tn):
    n, d = x2.shape
    width = w.shape[2]
    blocks = tm * d * 4 + d * 4 + d * tn * 2 + 2 * tm * tn * 2
    return pl.pallas_call(
        functools.partial(_in_proj_kernel, head_dim=head_dim),
        out_shape=(jax.ShapeDtypeStruct((tn // LANES, n // SSM_T, SSM_T * LANES), BF16),
                   jax.ShapeDtypeStruct((n, width - tn), BF16)),
        grid=(n // tm, width // tn),
        in_specs=[
            pl.BlockSpec((tm, d), lambda i, j: (i, 0)),
            pl.BlockSpec((1, d), lambda i, j: (0, 0)),
            pl.BlockSpec((None, d, tn), lambda i, j: (layer, 0, j)),
            pl.BlockSpec((1, LANES), lambda i, j: (0, 0)),
            pl.BlockSpec((1, LANES), lambda i, j: (0, 0)),
        ],
        out_specs=(pl.BlockSpec((tn // LANES, tm // SSM_T, SSM_T * LANES), lambda i, j: (0, i, 0)),
                   pl.BlockSpec((tm, tn), lambda i, j: (i, jnp.maximum(j - 1, 0)))),
        scratch_shapes=[pltpu.VMEM((tm, d), BF16), pltpu.VMEM((tn // LANES, tm, LANES), F32)],
        compiler_params=_params(("parallel", "arbitrary"), blocks,
                                tm * d * 2 + tm * tn * 4 + 6 * tm * PROJ_CHUNK * 4),
        name="in_proj",
    )(x2, nw, w, qw2, kw2)


def _ssm_prep_kernel(lr_ref, li_ref, ls_ref, bcat_ref, bswap_ref, ccat_ref, cswap_ref, d_ref,
                     lrl_ref, lil_ref, lsl_ref, m_ref, e_ref, ft_ref, p_ref, q_ref, ft0_scr,
                     *, state, group, nsteps):
    gl = LANES // group
    lane = lax.broadcasted_iota(jnp.int32, (1, LANES), 1)
    sign = jnp.where(lane < state, -1.0, 1.0).astype(F32)
    lr = jnp.minimum(lr_ref[0], LAM_RE_MAX)
    li = li_ref[0]
    dt = jnp.exp(ls_ref[0])
    mag = jnp.exp(lr * dt)
    ab_r = mag * jnp.cos(li * dt)
    ab_i = mag * jnp.sin(li * dt)
    den = lr * lr + li * li
    nr = ab_r - 1.0
    fr = (nr * lr + ab_i * li) / den
    fi = (ab_i * lr - nr * li) / den
    bcat = bcat_ref[0]
    bswap = bswap_ref[0]
    bb_cat = fr * bcat + sign * fi * bswap
    bb_swap = fr * bswap - sign * fi * bcat
    ccat = ccat_ref[0]
    cswap = cswap_ref[0]

    row_group = lax.broadcasted_iota(jnp.int32, (LANES, 1), 0) // group

    def store_block_diag(ref, block, value):
        for g in range(gl):
            ref[0, block * LANES:(block + 1) * LANES, g * LANES:(g + 1) * LANES] = (
                jnp.where(row_group == g, value, 0.0).astype(ref.dtype))

    pw_r = jnp.ones((LANES, LANES), F32)
    pw_i = jnp.zeros((LANES, LANES), F32)
    for n in range(SSM_T + 1):
        ft_blk = pw_r * ccat + pw_i * cswap
        if n < SSM_T:
            ft0_scr[n * LANES:(n + 1) * LANES, :] = ft_blk
            store_block_diag(e_ref, SSM_T - 1 - n, pw_r * bb_cat + sign * pw_i * bb_swap)
        if n >= 1:
            store_block_diag(ft_ref, n - 1, ft_blk)
        pw_r, pw_i = pw_r * ab_r - pw_i * ab_i, pw_r * ab_i + pw_i * ab_r

    width = SSM_T * LANES
    r = _dot_nt(bb_cat, ft0_scr[...], precision=lax.Precision.HIGHEST)
    col = lax.broadcasted_iota(jnp.int32, (LANES, width), 1)
    row = lax.broadcasted_iota(jnp.int32, (LANES, width), 0)
    r = jnp.where(row // group == (col % LANES) // group, r, 0.0)
    r = r + jnp.where(col == row, jnp.tile(d_ref[0], (1, SSM_T)), 0.0)
    for j in range(SSM_T):
        rows_j = slice(j * LANES, (j + 1) * LANES)
        if j:
            m_ref[0, rows_j, 0:j * LANES] = jnp.zeros((LANES, j * LANES), m_ref.dtype)
        m_ref[0, rows_j, j * LANES:width] = r[:, 0:width - j * LANES].astype(m_ref.dtype)

    lrl = jnp.minimum(lrl_ref[0], LAM_RE_MAX)
    dtl = jnp.exp(lsl_ref[0])
    lane_l = lax.broadcasted_iota(jnp.int32, lrl.shape, 1)
    sign_l = jnp.where(lane_l % LANES < state, -1.0, 1.0).astype(F32)
    steps = lax.broadcasted_iota(jnp.int32, (8, 1), 0)
    nd = (SSM_T * jnp.left_shift(1, jnp.minimum(steps, nsteps))).astype(F32)
    mag_l = jnp.exp(lrl * dtl * nd)
    ang_l = lil_ref[0] * dtl * nd
    p_ref[0] = mag_l * jnp.cos(ang_l)
    q_ref[0] = sign_l * mag_l * jnp.sin(ang_l)


def _ssm_prep(rows_in, lanes_in, *, state, group, nsteps):
    nb = rows_in[0].shape[0]
    gl = LANES // group
    width = SSM_T * LANES
    bspec = lambda *shape: pl.BlockSpec((1,) + shape, lambda i: (i, 0, 0))
    blocks = width * width * 2 + 2 * width * gl * LANES * 2 + 8 * LANES * LANES * 4
    return pl.pallas_call(
        functools.partial(_ssm_prep_kernel, state=state, group=group, nsteps=nsteps),
        out_shape=(
            jax.ShapeDtypeStruct((nb, width, width), BF16),
            jax.ShapeDtypeStruct((nb, width, gl * LANES), BF16),
            jax.ShapeDtypeStruct((nb, width, gl * LANES), BF16),
            jax.ShapeDtypeStruct((nb, 8, gl * LANES), F32),
            jax.ShapeDtypeStruct((nb, 8, gl * LANES), F32),
        ),
        grid=(nb,),
        in_specs=[bspec(LANES, LANES)] * 8 + [bspec(1, gl * LANES)] * 3,
        out_specs=(bspec(width, width), bspec(width, gl * LANES), bspec(width, gl * LANES),
                   bspec(8, gl * LANES), bspec(8, gl * LANES)),
        scratch_shapes=[pltpu.VMEM((width, LANES), F32)],
        compiler_params=_params(("parallel",), blocks, width * LANES * 4 + 6 * LANES * width * 4),
        name="ssm_prep",
    )(*rows_in, *lanes_in)


SSM_COL_SPLIT = 4


def _ssm_kernel(u_ref, m_ref, e_ref, ft_ref, p_ref, q_ref, o_ref, ys_ref, ym_ref,
                *, kchunks, nsteps, state):
    u = u_ref[0]
    rows, width = u.shape
    cw = width // SSM_COL_SPLIT
    s = jnp.dot(u, e_ref[0], preferred_element_type=F32)
    for c in range(SSM_COL_SPLIT):
        kk = (c + 1) * cw
        ym_ref[:, c * cw:kk] = jnp.dot(u[:, 0:kk], m_ref[0, 0:kk, c * cw:kk],
                                       preferred_element_type=F32)
    kidx = lax.broadcasted_iota(jnp.int32, (rows, 1), 0) & (kchunks - 1)
    s_prev = []
    for g in range(s.shape[1] // LANES):
        lanes = slice(g * LANES, (g + 1) * LANES)
        sg = s[:, lanes]
        for i in range(nsteps):
            d = 1 << i
            x = jnp.where(kidx >= d, pltpu.roll(sg, d, axis=0), 0.0)
            sg = sg + p_ref[0, i:i + 1, lanes] * x + q_ref[0, i:i + 1, lanes] * pltpu.roll(x, state, axis=1)
        s_prev.append(jnp.where(kidx >= 1, pltpu.roll(sg, 1, axis=0), 0.0).astype(BF16))
    s_prev = jnp.concatenate(s_prev, axis=1)
    for c in range(SSM_COL_SPLIT):
        cols = slice(c * cw, (c + 1) * cw)
        y = jax.nn.gelu(ym_ref[:, cols] + _dot_nt(s_prev, ft_ref[0, cols, :]))
        for tl in range(cw // LANES):
            t = c * (cw // LANES) + tl
            ys_ref[pl.ds(t, rows, stride=SSM_T), :] = y[:, tl * LANES:(tl + 1) * LANES]
    o_ref[0] = ys_ref[...].astype(o_ref.dtype)


def _ssm(u4, m, e, ft, p, q, *, kchunks, nsteps, rt):
    nb, rows, width = u4.shape
    sl = e.shape[2]
    blocks = 2 * rt * width * 2 + width * width * 2 + 2 * width * sl * 2 + 2 * 8 * sl * 4
    return pl.pallas_call(
        functools.partial(_ssm_kernel, kchunks=kchunks, nsteps=nsteps, state=LANES // 2),
        out_shape=jax.ShapeDtypeStruct((nb, rows * SSM_T, LANES), BF16),
        grid=(nb, rows // rt),
        in_specs=[pl.BlockSpec((1, rt, width), lambda b, r: (b, r, 0)),
                  pl.BlockSpec((1, width, width), lambda b, r: (b, 0, 0)),
                  pl.BlockSpec((1, width, sl), lambda b, r: (b, 0, 0)),
                  pl.BlockSpec((1, width, sl), lambda b, r: (b, 0, 0)),
                  pl.BlockSpec((1, 8, sl), lambda b, r: (b, 0, 0)),
                  pl.BlockSpec((1, 8, sl), lambda b, r: (b, 0, 0))],
        out_specs=pl.BlockSpec((1, rt * SSM_T, LANES), lambda b, r: (b, r, 0)),
        scratch_shapes=[pltpu.VMEM((rt * SSM_T, LANES), F32), pltpu.VMEM((rt, width), F32)],
        compiler_params=_params(("parallel", "arbitrary"), blocks,
                                2 * rt * width * 4 + 4 * rt * sl * 4
                                + 4 * rt * (width // SSM_COL_SPLIT) * 4),
        name="ssm",
    )(u4, m, e, ft, p, q)


def _bias_kernel(tab_ref, idx_ref, o_ref, *, n_buckets, far_bucket):
    h = pl.program_id(0)
    far = tab_ref[far_bucket, h]
    blocks = []
    for t in range(2):
        idx = idx_ref[t]
        acc = jnp.full(idx.shape, MASK_VALUE, F32)
        for b in range(n_buckets):
            acc = jnp.where(idx == b, (tab_ref[b, h] - far) * LOG2_E, acc)
        blocks.append(acc)
    prev_blk, near_blk = blocks
    zero_blk = jnp.zeros_like(near_blk)
    mask_blk = jnp.full_like(near_blk, MASK_VALUE)
    nb = o_ref.shape[2] // LANES
    for r in range(nb):
        for c in range(nb):
            where = (slice(r * LANES, (r + 1) * LANES), slice(c * LANES, (c + 1) * LANES))
            o_ref[(0, 0) + where] = prev_blk if (r, c) == (nb - 1, 0) else zero_blk
            o_ref[(0, 1) + where] = (near_blk if r == c else prev_blk if r == c - 1
                                     else mask_blk if r > c else zero_blk)


def _t5_bucket_np(rel, n_buckets):
    nb = n_buckets // 2
    ret = np.where(rel > 0, nb, 0)
    n = np.abs(rel)
    max_exact = nb // 2
    nf = np.maximum(n, 1).astype(np.float32)
    large = max_exact + (np.log(nf / np.float32(max_exact)) / np.float32(math.log(MAX_DISTANCE / max_exact))
                         * np.float32(nb - max_exact)).astype(np.int32)
    large = np.minimum(large, nb - 1)
    return (ret + np.where(n < max_exact, n, large)).astype(np.int32)


def _bias_tiles(rel_bias, *, tq):
    n_buckets, n_heads = rel_bias.shape
    c = np.arange(LANES)[:, None]
    r = np.arange(LANES)[None, :]
    prev = _t5_bucket_np(c - r - LANES, n_buckets)
    near = np.where(c // CHUNK <= r // CHUNK, _t5_bucket_np(c - r, n_buckets), -1)
    idx = np.stack([prev, near])
    far_bucket = int(_t5_bucket_np(np.array(-MAX_DISTANCE), n_buckets))
    assert LANES >= MAX_DISTANCE and LANES % CHUNK == 0 and tq % LANES == 0
    return pl.pallas_call(
        functools.partial(_bias_kernel, n_buckets=n_buckets, far_bucket=far_bucket),
        out_shape=jax.ShapeDtypeStruct((n_heads, 2, tq, tq), F32),
        grid=(n_heads,),
        in_specs=[pl.BlockSpec(memory_space=pltpu.SMEM),
                  pl.BlockSpec((2, LANES, LANES), lambda h: (0, 0, 0))],
        out_specs=pl.BlockSpec((1, 2, tq, tq), lambda h: (h, 0, 0, 0)),
        compiler_params=pltpu.CompilerParams(dimension_semantics=("parallel",)),
        name="attn_bias",
    )(rel_bias, jnp.asarray(idx))


ONES_ROWS = 16
FIXED_SHIFT_MAX = 60.0
BOUND_SLACK = 1.02


def _attn_kernel(q_ref, k_ref, v_ref, bias_ref, lq1_ref, lk1_ref, lq2_ref, lk2_ref, sw_ref,
                 o_ref, vt_scr, qm_scr, shift_scr, st_a, st_b, m_scr, acc_scr, *, tq, head_dim, lambda_init):
    seq, vd = v_ref.shape
    nq = seq // tq

    def block(j):
        return slice(j * tq, (j + 1) * tq)

    lo = lax.broadcasted_iota(jnp.int32, (1, LANES), 1) < head_dim
    halves = (lax.broadcasted_iota(jnp.int32, (8, LANES), 0)
              == (lax.broadcasted_iota(jnp.int32, (8, LANES), 1) >= head_dim).astype(jnp.int32)
              ).astype(BF16)
    qsq, ksq = [], []
    for c in range(nq):
        vt_scr[0:vd, block(c)] = v_ref[block(c), :].T
        q = q_ref[block(c), :]
        zero = jnp.zeros_like(q)
        qm_scr[0, block(c), :] = jnp.where(lo, q, zero)
        qm_scr[1, block(c), :] = jnp.where(lo, zero, q)
        k = k_ref[block(c), :]
        qsq.append(_dot_nt(halves, q * q))
        ksq.append(_dot_nt(halves, k * k))
    vt_scr[vd:, :] = jnp.ones((ONES_ROWS, seq), BF16)

    k_max = jnp.max(jnp.concatenate(ksq, axis=1), axis=1, keepdims=True)
    bias_max = jnp.maximum(jnp.maximum(jnp.max(bias_ref[0, 1, 0:LANES, 0:LANES]),
                                       jnp.max(bias_ref[0, 0, tq - LANES:tq, 0:LANES])), 0.0)
    bound = jnp.sqrt(jnp.concatenate(qsq, axis=1) * k_max) * BOUND_SLACK + bias_max
    shift_scr[0] = bound[0:1]
    shift_scr[1] = bound[1:2]
    fixed_shift_ok = jnp.max(bound[0:2]) <= FIXED_SHIFT_MAX

    s1 = jnp.sum(lq1_ref[...] * lk1_ref[...], axis=-1, keepdims=True)
    s2 = jnp.sum(lq2_ref[...] * lk2_ref[...], axis=-1, keepdims=True)
    lam = jnp.exp(s1) - jnp.exp(s2) + lambda_init

    def finalize(rows, a1, a2):
        ot = a1[0:vd] * (1.0 / a1[vd:vd + 1]) - lam * (a2[0:vd] * (1.0 / a2[vd:vd + 1]))
        y = _rms_norm(ot.T, sw_ref[...], SUBLN_EPS) * (1.0 - lambda_init)
        o_ref[rows, :] = y.astype(o_ref.dtype)


    @pl.when(fixed_shift_ok)
    def _():
        def scores(item, st_ref):
            i, g, k0, nk, q0, nqr = item
            kb = k_ref[g * tq + k0:g * tq + k0 + nk, :]
            for m in range(2):
                st_ref[m, 0:nk, 0:nqr] = _dot_nt(kb, qm_scr[m, i * tq + q0:i * tq + q0 + nqr, :])

        def accumulate(item, st_ref):
            i, g, k0, nk, q0, nqr = item
            vt = vt_scr[:, g * tq + k0:g * tq + k0 + nk]
            for m in range(2):
                st = st_ref[m, 0:nk, 0:nqr]
                if g >= i - 1:
                    st = st + bias_ref[0, 1 if g == i else 0, k0:k0 + nk, q0:q0 + nqr]
                p = jnp.exp2(st - shift_scr[m, :, i * tq + q0:i * tq + q0 + nqr]).astype(BF16)
                pv = jnp.dot(vt, p, preferred_element_type=F32)
                if g == 0 and k0 == 0:
                    acc_scr[i, m] = pv
                else:
                    acc_scr[i, m, :, q0:q0 + nqr] += pv

        half = tq // 2
        assert half % CHUNK == 0
        work = []
        for i in range(nq):
            work += [(i, g, 0, tq, 0, tq) for g in range(i)]
            work += [(i, i, 0, half, 0, tq), (i, i, half, half, half, half)]
        bufs = (st_a, st_b)
        scores(work[0], bufs[0])
        for t, item in enumerate(work):
            if t + 1 < len(work):
                scores(work[t + 1], bufs[(t + 1) % 2])
            accumulate(item, bufs[t % 2])
            if item[1] == item[0] and item[2] > 0:
                finalize(block(item[0]), acc_scr[item[0], 0], acc_scr[item[0], 1])

    @pl.when(jnp.logical_not(fixed_shift_ok))
    def _():
        def update(i, g, bias):
            kb = k_ref[pl.ds(pl.multiple_of(g * tq, tq), tq), :]
            vt = vt_scr[:, pl.ds(pl.multiple_of(g * tq, tq), tq)]
            for m in range(2):
                st = _dot_nt(kb, qm_scr[m, pl.ds(pl.multiple_of(i * tq, tq), tq), :])
                if bias is not None:
                    st = st + bias
                m_prev = m_scr[m]
                m_new = jnp.maximum(m_prev, jnp.max(st, axis=0, keepdims=True))
                p = jnp.exp2(st - m_new).astype(BF16)
                acc_scr[0, m] = (jnp.exp2(m_prev - m_new) * acc_scr[0, m]
                                 + jnp.dot(vt, p, preferred_element_type=F32))
                m_scr[m] = m_new

        def query_block(i, carry):
            m_scr[...] = jnp.full(m_scr.shape, MASK_VALUE, F32)
            acc_scr[0] = jnp.zeros(acc_scr.shape[1:], F32)
            update(i, i, bias_ref[0, 1])
            pl.when(i >= 1)(lambda: update(i, i - 1, bias_ref[0, 0]))
            lax.fori_loop(0, jnp.maximum(i - 1, 0), lambda g, c: (update(i, g, None), c)[1], 0)
            finalize(pl.ds(pl.multiple_of(i * tq, tq), tq), acc_scr[0, 0], acc_scr[0, 1])
            return carry

        lax.fori_loop(0, nq, query_block, 0)


def _attention(z, bias, lq1, lk1, lq2, lk2, sw, *, batch, seq, n_heads, head_dim, tq,
               q_col, k_col, v_col, lambda_init):
    n = z.shape[0]
    nq = seq // tq
    acc_rows = LANES + ONES_ROWS
    vec = pl.BlockSpec((1, head_dim), lambda b, h: (0, 0))
    blocks = 4 * seq * LANES * 2 + 2 * tq * tq * 4
    scratch = (acc_rows * seq * 2 + 2 * seq * LANES * 2 + 2 * 8 * seq * 4 + 2 * 8 * tq * 4
               + nq * 2 * acc_rows * tq * 4)
    return pl.pallas_call(
        functools.partial(_attn_kernel, tq=tq, head_dim=head_dim, lambda_init=lambda_init),
        out_shape=jax.ShapeDtypeStruct((n, n_heads * LANES), BF16),
        grid=(batch, n_heads),
        in_specs=[
            pl.BlockSpec((seq, LANES), lambda b, h: (b, q_col + h)),
            pl.BlockSpec((seq, LANES), lambda b, h: (b, k_col + h)),
            pl.BlockSpec((seq, LANES), lambda b, h: (b, v_col + h)),
            pl.BlockSpec((1, 2, tq, tq), lambda b, h: (h, 0, 0, 0)),
            vec, vec, vec, vec,
            pl.BlockSpec((1, LANES), lambda b, h: (0, 0)),
        ],
        out_specs=pl.BlockSpec((seq, LANES), lambda b, h: (b, h)),
        scratch_shapes=[pltpu.VMEM((acc_rows, seq), BF16),
                        pltpu.VMEM((2, seq, LANES), BF16),
                        pltpu.VMEM((2, 1, seq), F32),
                        pltpu.VMEM((2, tq, tq), F32),
                        pltpu.VMEM((2, tq, tq), F32),
                        pltpu.VMEM((2, 1, tq), F32),
                        pltpu.VMEM((nq, 2, acc_rows, tq), F32)],
        compiler_params=_params(("parallel", "parallel"), blocks, scratch + 8 * tq * tq * 4),
        name="attention",
    )(z, z, z, bias, lq1, lk1, lq2, lk2, sw)


def _merge_kernel(yg_ref, wglu_ref, bglu_ref, ya_ref, gs_ref, ga_ref, ps_ref, pa_ref, wo_ref,
                  x_hbm, o_ref, ys_ref, m_ref, x_sem):
    i = pl.program_id(0)
    j = pl.program_id(1)
    nblk = yg_ref.shape[0]
    per_chunk = PROJ_CHUNK // LANES
    tm = o_ref.shape[0]

    x_copy = pltpu.make_async_copy(x_hbm.at[pl.ds(pl.multiple_of(i * tm, tm), tm), :],
                                   o_ref, x_sem)

    @pl.when(j == 0)
    def _():
        x_copy.start()
        yg = jnp.concatenate([yg_ref[b] for b in range(nblk)], axis=1)
        for c in range(nblk // per_chunk):
            cols = slice(c * PROJ_CHUNK, (c + 1) * PROJ_CHUNK)
            g = jnp.dot(yg, wglu_ref[:, cols], preferred_element_type=F32) + bglu_ref[:, cols]
            ys_ref[:, cols] = (yg[:, cols].astype(F32) * jax.nn.sigmoid(g)).astype(BF16)

    ys = ys_ref[...]
    ya = ya_ref[...]
    for c in range(m_ref.shape[1] // PROJ_CHUNK):
        cols = slice(c * PROJ_CHUNK, (c + 1) * PROJ_CHUNK)
        m = (gs_ref[:, cols].astype(F32) * jnp.dot(ys, ps_ref[:, cols], preferred_element_type=F32)
             + ga_ref[:, cols].astype(F32) * jnp.dot(ya, pa_ref[:, cols], preferred_element_type=F32))
        m_ref[:, cols] = m.astype(BF16)
    pl.when(j == 0)(x_copy.wait)
    m = m_ref[...]
    cw = o_ref.shape[1] // DOWN_SPLIT
    for c in range(DOWN_SPLIT):
        cols = slice(c * cw, (c + 1) * cw)
        o_ref[:, cols] += jnp.dot(m, wo_ref[:, cols], preferred_element_type=F32)


def _merge(yg, wglu, bglu, ya, z, ps, pa, wo, x2, *, layer, gs_col, ga_col, tm, tj):
    n, d = x2.shape
    ws = yg.shape[0] * LANES
    wa = ya.shape[1]
    blocks = (tm * ws * 2 + ws * ws * 2 + ws * 4 + tm * wa * 2 + 2 * tm * tj * 2
              + ws * tj * 2 + wa * tj * 2 + tj * d * 2 + tm * d * 4)
    return pl.pallas_call(
        _merge_kernel,
        out_shape=jax.ShapeDtypeStruct((n, d), F32),
        grid=(n // tm, d // tj),
        in_specs=[
            pl.BlockSpec((ws // LANES, tm, LANES), lambda i, j: (0, i, 0)),
            pl.BlockSpec((None, ws, ws), lambda i, j: (layer, 0, 0)),
            pl.BlockSpec((1, ws), lambda i, j: (0, 0)),
            pl.BlockSpec((tm, wa), lambda i, j: (i, 0)),
            pl.BlockSpec((tm, tj), lambda i, j: (i, gs_col + j)),
            pl.BlockSpec((tm, tj), lambda i, j: (i, ga_col + j)),
            pl.BlockSpec((None, ws, tj), lambda i, j: (layer, 0, j)),
            pl.BlockSpec((None, wa, tj), lambda i, j: (layer, 0, j)),
            pl.BlockSpec((None, tj, d), lambda i, j: (layer, j, 0)),
            pl.BlockSpec(memory_space=pl.ANY),
        ],
        out_specs=pl.BlockSpec((tm, d), lambda i, j: (i, 0)),
        scratch_shapes=[pltpu.VMEM((tm, ws), BF16), pltpu.VMEM((tm, tj), BF16),
                        pltpu.SemaphoreType.DMA],
        compiler_params=_params(("parallel", "arbitrary"), blocks,
                                tm * ws * 2 + tm * tj * 2 + 8 * tm * PROJ_CHUNK * 4),
        name="merge",
    )(yg, wglu, bglu, ya, z, z, ps, pa, wo, x2)


def _ffn_kernel(x_hbm, nw_ref, wg_ref, wu_ref, wd_ref, o_ref, h_ref, a_ref, x_buf, x_sem):
    i = pl.program_id(0)
    j = pl.program_id(1)
    tm = o_ref.shape[0]

    def x_copy(tile):
        return pltpu.make_async_copy(x_hbm.at[pl.ds(pl.multiple_of(tile * tm, tm), tm), :],
                                     x_buf, x_sem)

    @pl.when(j == 0)
    def _():
        pl.when(i == 0)(lambda: x_copy(0).start())
        x_copy(i).wait()
        x = x_buf[...]
        h_ref[...] = _rms_norm(x, nw_ref[...], RMS_EPS).astype(BF16)
        o_ref[...] = x

    @pl.when(jnp.logical_and(j == 1, i + 1 < pl.num_programs(0)))
    def _():
        x_copy(i + 1).start()

    h = h_ref[...]
    for c in range(a_ref.shape[1] // PROJ_CHUNK):
        cols = slice(c * PROJ_CHUNK, (c + 1) * PROJ_CHUNK)
        g = jnp.dot(h, wg_ref[:, cols], preferred_element_type=F32)
        u = jnp.dot(h, wu_ref[:, cols], preferred_element_type=F32)
        a_ref[:, cols] = (jax.nn.silu(g) * u).astype(BF16)
    a = a_ref[...]
    cw = o_ref.shape[1] // DOWN_SPLIT
    for c in range(DOWN_SPLIT):
        cols = slice(c * cw, (c + 1) * cw)
        o_ref[:, cols] += jnp.dot(a, wd_ref[:, cols], preferred_element_type=F32)


def _ffn(x2, nw, wg, wu, wd, *, layer, tm, tf):
    n, d = x2.shape
    dff = wg.shape[2]
    assert dff // tf >= 2
    blocks = tm * d * 4 + d * 4 + 3 * d * tf * 2
    return pl.pallas_call(
        _ffn_kernel,
        out_shape=jax.ShapeDtypeStruct((n, d), F32),
        grid=(n // tm, dff // tf),
        in_specs=[
            pl.BlockSpec(memory_space=pl.ANY),
            pl.BlockSpec((1, d), lambda i, j: (0, 0)),
            pl.BlockSpec((None, d, tf), lambda i, j: (layer, 0, j)),
            pl.BlockSpec((None, d, tf), lambda i, j: (layer, 0, j)),
            pl.BlockSpec((None, tf, d), lambda i, j: (layer, j, 0)),
        ],
        out_specs=pl.BlockSpec((tm, d), lambda i, j: (i, 0)),
        scratch_shapes=[pltpu.VMEM((tm, d), BF16), pltpu.VMEM((tm, tf), BF16),
                        pltpu.VMEM((tm, d), F32), pltpu.SemaphoreType.DMA],
        compiler_params=_params(("arbitrary", "arbitrary"), blocks,
                                tm * d * 2 + tm * tf * 2 + tm * d * 4 + 6 * tm * PROJ_CHUNK * 4),
        name="ffn",
    )(x2, nw, wg, wu, wd)


def _largest_tile(total, target, quantum):
    t = min(total, target)
    while total % t or t % quantum:
        t -= quantum
    return t


def kernel(x, norm1_w, w_in, lam_re, lam_im, log_step, ssm_b_re, ssm_b_im, ssm_c_re, ssm_c_im, ssm_d, w_glu, b_glu, q_norm_w, k_norm_w, lambda_q1, lambda_k1, lambda_q2, lambda_k2, subln_w, w_proj_ssm, w_proj_attn, w_out, rel_bias, norm2_w, w_ffn_gate, w_ffn_up, w_ffn_down):
    batch, seq, d_model = x.shape
    depth = w_in.shape[0]
    n = batch * seq
    _, groups, state = lam_re.shape
    group = ssm_b_re.shape[-1]
    ssm_width = groups * group
    head_dim = q_norm_w.shape[-1]
    v_dim = subln_w.shape[-1]
    attn_width = w_proj_attn.shape[1]
    n_heads = attn_width // v_dim
    d_ff = w_ffn_gate.shape[-1]

    tn = ssm_width
    assert ssm_width == attn_width and d_model % tn == 0
    assert w_in.shape[2] == ssm_width + 3 * attn_width + 2 * d_model
    assert 2 * head_dim == LANES and v_dim == LANES and 2 * state == LANES
    assert SSM_T * group == V7X_MXU_DIM
    kchunks = seq // SSM_T
    nsteps = kchunks.bit_length() - 1
    assert seq % SSM_T == 0 and kchunks == 1 << nsteps and nsteps <= 8

    tm_proj = _largest_tile(n, 1024, 16 * SSM_T)
    tm_merge = _largest_tile(n, 1024, 16)
    tm_ffn = _largest_tile(n, 1024, 8)
    tq = _largest_tile(seq, 512, MAX_DISTANCE)
    assert tq % CHUNK == 0
    tj = _largest_tile(d_model, 512, PROJ_CHUNK)
    tf = _largest_tile(d_ff, 512, LANES)

    q_col = 0
    k_col = q_col + attn_width // LANES
    v_col = k_col + attn_width // LANES
    gs_col = 3 * attn_width // tj
    ga_col = gs_col + d_model // tj

    bias = _bias_tiles(rel_bias, tq=tq)
    x2 = x.reshape(n, d_model)
    nblk = ssm_width // LANES
    seqs_per_tile = math.gcd(batch, max(1, SSM_ROWS // kchunks))
    rt = kchunks * seqs_per_tile
    dup = lambda a: jnp.concatenate([a, a], axis=-1)
    per_row = lambda a: jnp.repeat(a, group, axis=0).reshape(nblk, LANES, LANES)
    per_lane = lambda a: a.reshape(nblk, 1, -1)
    as_rows = lambda a: a.reshape(nblk, LANES, LANES)

    w_in, w_glu, w_proj_ssm, w_proj_attn, w_out, w_ffn_gate, w_ffn_up, w_ffn_down = (
        w.astype(BF16) for w in (w_in, w_glu, w_proj_ssm, w_proj_attn, w_out,
                                 w_ffn_gate, w_ffn_up, w_ffn_down))

    for l in range(depth):
        lambda_init = 0.8 - 0.6 * math.exp(-0.3 * l)

        qw2 = jnp.tile(q_norm_w[l], LANES // head_dim)[None, :]
        kw2 = jnp.tile(k_norm_w[l], LANES // head_dim)[None, :]
        u3, z = _in_proj(x2, norm1_w[l][None, :], w_in, qw2, kw2,
                         layer=l, head_dim=head_dim, tm=tm_proj, tn=tn)

        lam_re2, lam_im2 = dup(lam_re[l]), dup(lam_im[l])
        step2 = jnp.broadcast_to(log_step[l][:, None], (groups, 2 * state))
        b_re_t = jnp.swapaxes(ssm_b_re[l], 1, 2)
        b_im_t = jnp.swapaxes(ssm_b_im[l], 1, 2)
        c_re, c_im = ssm_c_re[l], ssm_c_im[l]
        d_rows = jnp.broadcast_to(ssm_d[l][:, None], (ssm_width, LANES)).reshape(nblk, LANES, LANES)
        m_op, e_op, ft_op, p_op, q_op = _ssm_prep(
            (per_row(lam_re2), per_row(lam_im2), per_row(step2),
             as_rows(jnp.concatenate([b_re_t, b_im_t], axis=-1)),
             as_rows(jnp.concatenate([b_im_t, b_re_t], axis=-1)),
             as_rows(jnp.concatenate([c_re, -c_im], axis=-1)),
             as_rows(jnp.concatenate([-c_im, -c_re], axis=-1)), d_rows),
            (per_lane(lam_re2), per_lane(lam_im2), per_lane(step2)),
            state=state, group=group, nsteps=nsteps)
        yg = _ssm(u3, m_op, e_op, ft_op, p_op, q_op, kchunks=kchunks, nsteps=nsteps, rt=rt)

        ya = _attention(z, bias, lambda_q1[l][None, :], lambda_k1[l][None, :],
                        lambda_q2[l][None, :], lambda_k2[l][None, :], subln_w[l][None, :],
                        batch=batch, seq=seq, n_heads=n_heads, head_dim=head_dim, tq=tq,
                        q_col=q_col, k_col=k_col, v_col=v_col, lambda_init=lambda_init)

        x2 = _merge(yg, w_glu, b_glu[l][None, :], ya, z, w_proj_ssm, w_proj_attn, w_out, x2,
                    layer=l, gs_col=gs_col, ga_col=ga_col, tm=tm_merge, tj=tj)
        x2 = _ffn(x2, norm2_w[l][None, :], w_ffn_gate, w_ffn_up, w_ffn_down,
                  layer=l, tm=tm_ffn, tf=tf)

    return x2.reshape(batch, seq, d_model)
```

```python
import functools
import math

import numpy as np
import jax
import jax.numpy as jnp
from jax import lax
from jax.experimental import pallas as pl
from jax.experimental.pallas import tpu as pltpu

F32 = jnp.float32
BF16 = jnp.bfloat16

CHUNK = 64
MAX_DISTANCE = 128
RMS_EPS = 1e-6
SUBLN_EPS = 1e-5
LAM_RE_MAX = -1e-4
MASK_VALUE = -1e30
LOG2_E = math.log2(math.e)

LANES = 128
V7X_MXU_DIM = 256
V7X_VMEM_BYTES = 64 * 1024 * 1024

PROJ_CHUNK = V7X_MXU_DIM
DOWN_SPLIT = 4
SSM_T = 16
SSM_ROWS = 512


def _vmem_limit(block_bytes, extra_bytes):
    return int(min(2 * block_bytes + extra_bytes + (4 << 20), V7X_VMEM_BYTES - (6 << 20)))


def _params(semantics, block_bytes, extra_bytes):
    return pltpu.CompilerParams(dimension_semantics=semantics,
                                vmem_limit_bytes=_vmem_limit(block_bytes, extra_bytes))


def _rms_norm(x, w, eps):
    ms = jnp.mean(x * x, axis=-1, keepdims=True)
    return x * lax.rsqrt(ms + eps) * w


def _dot_nt(a, b, **kw):
    return lax.dot_general(a, b, (((1,), (1,)), ((), ())), preferred_element_type=F32, **kw)


def _in_proj_kernel(x_ref, nw_ref, w_ref, qw_ref, kw_ref, u_ref, o_ref, h_ref, us_ref, *, head_dim):
    j = pl.program_id(1)

    @pl.when(j == 0)
    def _():
        h_ref[...] = _rms_norm(x_ref[...], nw_ref[...], RMS_EPS).astype(BF16)

    tn = w_ref.shape[1]

    def project(epilogue):
        h = h_ref[...]
        for c in range(tn // PROJ_CHUNK):
            zc = jnp.dot(h, w_ref[:, c * PROJ_CHUNK:(c + 1) * PROJ_CHUNK], preferred_element_type=F32)
            for s in range(PROJ_CHUNK // LANES):
                epilogue(c * (PROJ_CHUNK // LANES) + s, zc[:, s * LANES:(s + 1) * LANES])

    def store_z(b, y):
        o_ref[:, b * LANES:(b + 1) * LANES] = y.astype(o_ref.dtype)

    def seg_norm(w_row, scale):
        lo = lax.broadcasted_iota(jnp.int32, (1, LANES), 1) < head_dim

        def epilogue(b, zb):
            sq = zb * zb
            s_lo = jnp.sum(jnp.where(lo, sq, 0.0), axis=-1, keepdims=True)
            s_hi = jnp.sum(jnp.where(lo, 0.0, sq), axis=-1, keepdims=True)
            ms = jnp.where(lo, s_lo, s_hi) * (1.0 / head_dim)
            store_z(b, zb * lax.rsqrt(ms + RMS_EPS) * w_row * scale)
        return epilogue

    def store_u(b, zb):
        us_ref[b] = zb
        chunks = zb.shape[0] // SSM_T
        for t in range(SSM_T):
            u_ref[b, :, t * LANES:(t + 1) * LANES] = (
                us_ref[b, pl.ds(t, chunks, stride=SSM_T), :].astype(u_ref.dtype))

    pl.when(j == 0)(lambda: project(store_u))
    pl.when(j == 1)(lambda: project(seg_norm(qw_ref[...], head_dim ** -0.5 * LOG2_E)))
    pl.when(j == 2)(lambda: project(seg_norm(kw_ref[...], 1.0)))
    pl.when(j == 3)(lambda: project(store_z))
    pl.when(j >= 4)(lambda: project(lambda b, zb: store_z(b, jax.nn.sigmoid(zb))))


def _in_proj(x2, nw, w, qw2, kw2, *, layer, head_dim, tm, tn):
    n, d = x2.shape
    width = w.shape[2]
    blocks = tm * d * 4 + d * 4 + d * tn * 2 + 2 * tm * tn * 2
    return pl.pallas_call(
        functools.partial(_in_proj_kernel, head_dim=head_dim),
        out_shape=(jax.ShapeDtypeStruct((tn // LANES, n // SSM_T, SSM_T * LANES), BF16),
                   jax.ShapeDtypeStruct((n, width - tn), BF16)),
        grid=(n // tm, width // tn),
        in_specs=[
            pl.BlockSpec((tm, d), lambda i, j: (i, 0)),
            pl.BlockSpec((1, d), lambda i, j: (0, 0)),
            pl.BlockSpec((None, d, tn), lambda i, j: (layer, 0, j)),
            pl.BlockSpec((1, LANES), lambda i, j: (0, 0)),
            pl.BlockSpec((1, LANES), lambda i, j: (0, 0)),
        ],
        out_specs=(pl.BlockSpec((tn // LANES, tm // SSM_T, SSM_T * LANES), lambda i, j: (0, i, 0)),
                   pl.BlockSpec((tm, tn), lambda i, j: (i, jnp.maximum(j - 1, 0)))),
        scratch_shapes=[pltpu.VMEM((tm, d), BF16), pltpu.VMEM((tn // LANES, tm, LANES), F32)],
        compiler_params=_params(("parallel", "arbitrary"), blocks,
                                tm * d * 2 + tm * tn * 4 + 6 * tm * PROJ_CHUNK * 4),
        name="in_proj",
    )(x2, nw, w, qw2, kw2)


def _ssm_prep_kernel(lr_ref, li_ref, ls_ref, bcat_ref, bswap_ref, ccat_ref, cswap_ref, d_ref,
                     lrl_ref, lil_ref, lsl_ref, m_ref, e_ref, ft_ref, p_ref, q_ref, ft0_scr,
                     *, state, group, nsteps):
    gl = LANES // group
    lane = lax.broadcasted_iota(jnp.int32, (1, LANES), 1)
    sign = jnp.where(lane < state, -1.0, 1.0).astype(F32)
    lr = jnp.minimum(lr_ref[0], LAM_RE_MAX)
    li = li_ref[0]
    dt = jnp.exp(ls_ref[0])
    mag = jnp.exp(lr * dt)
    ab_r = mag * jnp.cos(li * dt)
    ab_i = mag * jnp.sin(li * dt)
    den = lr * lr + li * li
    nr = ab_r - 1.0
    fr = (nr * lr + ab_i * li) / den
    fi = (ab_i * lr - nr * li) / den
    bcat = bcat_ref[0]
    bswap = bswap_ref[0]
    bb_cat = fr * bcat + sign * fi * bswap
    bb_swap = fr * bswap - sign * fi * bcat
    ccat = ccat_ref[0]
    cswap = cswap_ref[0]

    row_group = lax.broadcasted_iota(jnp.int32, (LANES, 1), 0) // group

    def store_block_diag(ref, block, value):
        for g in range(gl):
            ref[0, block * LANES:(block + 1) * LANES, g * LANES:(g + 1) * LANES] = (
                jnp.where(row_group == g, value, 0.0).astype(ref.dtype))

    pw_r = jnp.ones((LANES, LANES), F32)
    pw_i = jnp.zeros((LANES, LANES), F32)
    for n in range(SSM_T + 1):
        ft_blk = pw_r * ccat + pw_i * cswap
        if n < SSM_T:
            ft0_scr[n * LANES:(n + 1) * LANES, :] = ft_blk
            store_block_diag(e_ref, SSM_T - 1 - n, pw_r * bb_cat + sign * pw_i * bb_swap)
        if n >= 1:
            store_block_diag(ft_ref, n - 1, ft_blk)
        pw_r, pw_i = pw_r * ab_r - pw_i * ab_i, pw_r * ab_i + pw_i * ab_r

    width = SSM_T * LANES
    r = _dot_nt(bb_cat, ft0_scr[...], precision=lax.Precision.HIGHEST)
    col = lax.broadcasted_iota(jnp.int32, (LANES, width), 1)
    row = lax.broadcasted_iota(jnp.int32, (LANES, width), 0)
    r = jnp.where(row // group == (col % LANES) // group, r, 0.0)
    r = r + jnp.where(col == row, jnp.tile(d_ref[0], (1, SSM_T)), 0.0)
    for j in range(SSM_T):
        rows_j = slice(j * LANES, (j + 1) * LANES)
        if j:
            m_ref[0, rows_j, 0:j * LANES] = jnp.zeros((LANES, j * LANES), m_ref.dtype)
        m_ref[0, rows_j, j * LANES:width] = r[:, 0:width - j * LANES].astype(m_ref.dtype)

    lrl = jnp.minimum(lrl_ref[0], LAM_RE_MAX)
    dtl = jnp.exp(lsl_ref[0])
    lane_l = lax.broadcasted_iota(jnp.int32, lrl.shape, 1)
    sign_l = jnp.where(lane_l % LANES < state, -1.0, 1.0).astype(F32)
    steps = lax.broadcasted_iota(jnp.int32, (8, 1), 0)
    nd = (SSM_T * jnp.left_shift(1, jnp.minimum(steps, nsteps))).astype(F32)
    mag_l = jnp.exp(lrl * dtl * nd)
    ang_l = lil_ref[0] * dtl * nd
    p_ref[0] = mag_l * jnp.cos(ang_l)
    q_ref[0] = sign_l * mag_l * jnp.sin(ang_l)


def _ssm_prep(rows_in, lanes_in, *, state, group, nsteps):
    nb = rows_in[0].shape[0]
    gl = LANES // group
    width = SSM_T * LANES
    bspec = lambda *shape: pl.BlockSpec((1,) + shape, lambda i: (i, 0, 0))
    blocks = width * width * 2 + 2 * width * gl * LANES * 2 + 8 * LANES * LANES * 4
    return pl.pallas_call(
        functools.partial(_ssm_prep_kernel, state=state, group=group, nsteps=nsteps),
        out_shape=(
            jax.ShapeDtypeStruct((nb, width, width), BF16),
            jax.ShapeDtypeStruct((nb, width, gl * LANES), BF16),
            jax.ShapeDtypeStruct((nb, width, gl * LANES), BF16),
            jax.ShapeDtypeStruct((nb, 8, gl * LANES), F32),
            jax.ShapeDtypeStruct((nb, 8, gl * LANES), F32),
        ),
        grid=(nb,),
        in_specs=[bspec(LANES, LANES)] * 8 + [bspec(1, gl * LANES)] * 3,
        out_specs=(bspec(width, width), bspec(width, gl * LANES), bspec(width, gl * LANES),
                   bspec(8, gl * LANES), bspec(8, gl * LANES)),
        scratch_shapes=[pltpu.VMEM((width, LANES), F32)],
        compiler_params=_params(("parallel",), blocks, width * LANES * 4 + 6 * LANES * width * 4),
        name="ssm_prep",
    )(*rows_in, *lanes_in)


SSM_COL_SPLIT = 4


def _ssm_kernel(u_ref, m_ref, e_ref, ft_ref, p_ref, q_ref, o_ref, ys_ref, ym_ref,
                *, kchunks, nsteps, state):
    u = u_ref[0]
    rows, width = u.shape
    cw = width // SSM_COL_SPLIT
    s = jnp.dot(u, e_ref[0], preferred_element_type=F32)
    for c in range(SSM_COL_SPLIT):
        kk = (c + 1) * cw
        ym_ref[:, c * cw:kk] = jnp.dot(u[:, 0:kk], m_ref[0, 0:kk, c * cw:kk],
                                       preferred_element_type=F32)
    kidx = lax.broadcasted_iota(jnp.int32, (rows, 1), 0) & (kchunks - 1)
    s_prev = []
    for g in range(s.shape[1] // LANES):
        lanes = slice(g * LANES, (g + 1) * LANES)
        sg = s[:, lanes]
        for i in range(nsteps):
            d = 1 << i
            x = jnp.where(kidx >= d, pltpu.roll(sg, d, axis=0), 0.0)
            sg = sg + p_ref[0, i:i + 1, lanes] * x + q_ref[0, i:i + 1, lanes] * pltpu.roll(x, state, axis=1)
        s_prev.append(jnp.where(kidx >= 1, pltpu.roll(sg, 1, axis=0), 0.0).astype(BF16))
    s_prev = jnp.concatenate(s_prev, axis=1)
    for c in range(SSM_COL_SPLIT):
        cols = slice(c * cw, (c + 1) * cw)
        y = jax.nn.gelu(ym_ref[:, cols] + _dot_nt(s_prev, ft_ref[0, cols, :]))
        for tl in range(cw // LANES):
            t = c * (cw // LANES) + tl
            ys_ref[pl.ds(t, rows, stride=SSM_T), :] = y[:, tl * LANES:(tl + 1) * LANES]
    o_ref[0] = ys_ref[...].astype(o_ref.dtype)


def _ssm(u4, m, e, ft, p, q, *, kchunks, nsteps, rt):
    nb, rows, width = u4.shape
    sl = e.shape[2]
    blocks = 2 * rt * width * 2 + width * width * 2 + 2 * width * sl * 2 + 2 * 8 * sl * 4
    return pl.pallas_call(
        functools.partial(_ssm_kernel, kchunks=kchunks, nsteps=nsteps, state=LANES // 2),
        out_shape=jax.ShapeDtypeStruct((nb, rows * SSM_T, LANES), BF16),
        grid=(nb, rows // rt),
        in_specs=[pl.BlockSpec((1, rt, width), lambda b, r: (b, r, 0)),
                  pl.BlockSpec((1, width, width), lambda b, r: (b, 0, 0)),
                  pl.BlockSpec((1, width, sl), lambda b, r: (b, 0, 0)),
                  pl.BlockSpec((1, width, sl), lambda b, r: (b, 0, 0)),
                  pl.BlockSpec((1, 8, sl), lambda b, r: (b, 0, 0)),
                  pl.BlockSpec((1, 8, sl), lambda b, r: (b, 0, 0))],
        out_specs=pl.BlockSpec((1, rt * SSM_T, LANES), lambda b, r: (b, r, 0)),
        scratch_shapes=[pltpu.VMEM((rt * SSM_T, LANES), F32), pltpu.VMEM((rt, width), F32)],
        compiler_params=_params(("parallel", "arbitrary"), blocks,
                                2 * rt * width * 4 + 4 * rt * sl * 4
                                + 4 * rt * (width // SSM_COL_SPLIT) * 4),
        name="ssm",
    )(u4, m, e, ft, p, q)


def _bias_kernel(tab_ref, idx_ref, o_ref, *, n_buckets, far_bucket):
    h = pl.program_id(0)
    far = tab_ref[far_bucket, h]
    blocks = []
    for t in range(2):
        idx = idx_ref[t]
        acc = jnp.full(idx.shape, MASK_VALUE, F32)
        for b in range(n_buckets):
            acc = jnp.where(idx == b, (tab_ref[b, h] - far) * LOG2_E, acc)
        blocks.append(acc)
    prev_blk, near_blk = blocks
    zero_blk = jnp.zeros_like(near_blk)
    mask_blk = jnp.full_like(near_blk, MASK_VALUE)
    nb = o_ref.shape[2] // LANES
    for r in range(nb):
        for c in range(nb):
            where = (slice(r * LANES, (r + 1) * LANES), slice(c * LANES, (c + 1) * LANES))
            o_ref[(0, 0) + where] = prev_blk if (r, c) == (nb - 1, 0) else zero_blk
            o_ref[(0, 1) + where] = (near_blk if r == c else prev_blk if r == c - 1
                                     else mask_blk if r > c else zero_blk)


def _t5_bucket_np(rel, n_buckets):
    nb = n_buckets // 2
    ret = np.where(rel > 0, nb, 0)
    n = np.abs(rel)
    max_exact = nb // 2
    nf = np.maximum(n, 1).astype(np.float32)
    large = max_exact + (np.log(nf / np.float32(max_exact)) / np.float32(math.log(MAX_DISTANCE / max_exact))
                         * np.float32(nb - max_exact)).astype(np.int32)
    large = np.minimum(large, nb - 1)
    return (ret + np.where(n < max_exact, n, large)).astype(np.int32)


def _bias_tiles(rel_bias, *, tq):
    n_buckets, n_heads = rel_bias.shape
    c = np.arange(LANES)[:, None]
    r = np.arange(LANES)[None, :]
    prev = _t5_bucket_np(c - r - LANES, n_buckets)
    near = np.where(c // CHUNK <= r // CHUNK, _t5_bucket_np(c - r, n_buckets), -1)
    idx = np.stack([prev, near])
    far_bucket = int(_t5_bucket_np(np.array(-MAX_DISTANCE), n_buckets))
    assert LANES >= MAX_DISTANCE and LANES % CHUNK == 0 and tq % LANES == 0
    return pl.pallas_call(
        functools.partial(_bias_kernel, n_buckets=n_buckets, far_bucket=far_bucket),
        out_shape=jax.ShapeDtypeStruct((n_heads, 2, tq, tq), F32),
        grid=(n_heads,),
        in_specs=[pl.BlockSpec(memory_space=pltpu.SMEM),
                  pl.BlockSpec((2, LANES, LANES), lambda h: (0, 0, 0))],
        out_specs=pl.BlockSpec((1, 2, tq, tq), lambda h: (h, 0, 0, 0)),
        compiler_params=pltpu.CompilerParams(dimension_semantics=("parallel",)),
        name="attn_bias",
    )(rel_bias, jnp.asarray(idx))


ONES_ROWS = 16
FIXED_SHIFT_MAX = 60.0
BOUND_SLACK = 1.02


def _attn_kernel(q_ref, k_ref, v_ref, bias_ref, lq1_ref, lk1_ref, lq2_ref, lk2_ref, sw_ref,
                 o_ref, vt_scr, qm_scr, shift_scr, st_a, st_b, m_scr, acc_scr, *, tq, head_dim, lambda_init):
    seq, vd = v_ref.shape
    nq = seq // tq

    def block(j):
        return slice(j * tq, (j + 1) * tq)

    lo = lax.broadcasted_iota(jnp.int32, (1, LANES), 1) < head_dim
    halves = (lax.broadcasted_iota(jnp.int32, (8, LANES), 0)
              == (lax.broadcasted_iota(jnp.int32, (8, LANES), 1) >= head_dim).astype(jnp.int32)
              ).astype(BF16)
    qsq, ksq = [], []
    for c in range(nq):
        vt_scr[0:vd, block(c)] = v_ref[block(c), :].T
        q = q_ref[block(c), :]
        zero = jnp.zeros_like(q)
        qm_scr[0, block(c), :] = jnp.where(lo, q, zero)
        qm_scr[1, block(c), :] = jnp.where(lo, zero, q)
        k = k_ref[block(c), :]
        qsq.append(_dot_nt(halves, q * q))
        ksq.append(_dot_nt(halves, k * k))
    vt_scr[vd:, :] = jnp.ones((ONES_ROWS, seq), BF16)

    k_max = jnp.max(jnp.concatenate(ksq, axis=1), axis=1, keepdims=True)
    bias_max = jnp.maximum(jnp.maximum(jnp.max(bias_ref[0, 1, 0:LANES, 0:LANES]),
                                       jnp.max(bias_ref[0, 0, tq - LANES:tq, 0:LANES])), 0.0)
    bound = jnp.sqrt(jnp.concatenate(qsq, axis=1) * k_max) * BOUND_SLACK + bias_max
    shift_scr[0] = bound[0:1]
    shift_scr[1] = bound[1:2]
    fixed_shift_ok = jnp.max(bound[0:2]) <= FIXED_SHIFT_MAX

    s1 = jnp.sum(lq1_ref[...] * lk1_ref[...], axis=-1, keepdims=True)
    s2 = jnp.sum(lq2_ref[...] * lk2_ref[...], axis=-1, keepdims=True)
    lam = jnp.exp(s1) - jnp.exp(s2) + lambda_init

    def finalize(rows, a1, a2):
        ot = a1[0:vd] * (1.0 / a1[vd:vd + 1]) - lam * (a2[0:vd] * (1.0 / a2[vd:vd + 1]))
        y = _rms_norm(ot.T, sw_ref[...], SUBLN_EPS) * (1.0 - lambda_init)
        o_ref[rows, :] = y.astype(o_ref.dtype)


    @pl.when(fixed_shift_ok)
    def _():
        def scores(item, st_ref):
            i, g, k0, nk, q0, nqr = item
            kb = k_ref[g * tq + k0:g * tq + k0 + nk, :]
            for m in range(2):
                st_ref[m, 0:nk, 0:nqr] = _dot_nt(kb, qm_scr[m, i * tq + q0:i * tq + q0 + nqr, :])

        def accumulate(item, st_ref):
            i, g, k0, nk, q0, nqr = item
            vt = vt_scr[:, g * tq + k0:g * tq + k0 + nk]
            for m in range(2):
                st = st_ref[m, 0:nk, 0:nqr]
                if g >= i - 1:
                    st = st + bias_ref[0, 1 if g == i else 0, k0:k0 + nk, q0:q0 + nqr]
                p = jnp.exp2(st - shift_scr[m, :, i * tq + q0:i * tq + q0 + nqr]).astype(BF16)
                pv = jnp.dot(vt, p, preferred_element_type=F32)
                if g == 0 and k0 == 0:
                    acc_scr[i, m] = pv
                else:
                    acc_scr[i, m, :, q0:q0 + nqr] += pv

        half = tq // 2
        assert half % CHUNK == 0
        work = []
        for i in range(nq):
            work += [(i, g, 0, tq, 0, tq) for g in range(i)]
            work += [(i, i, 0, half, 0, tq), (i, i, half, half, half, half)]
        bufs = (st_a, st_b)
        scores(work[0], bufs[0])
        for t, item in enumerate(work):
            if t + 1 < len(work):
                scores(work[t + 1], bufs[(t + 1) % 2])
            accumulate(item, bufs[t % 2])
            if item[1] == item[0] and item[2] > 0:
                finalize(block(item[0]), acc_scr[item[0], 0], acc_scr[item[0], 1])

    @pl.when(jnp.logical_not(fixed_shift_ok))
    def _():
        def update(i, g, bias):
            kb = k_ref[pl.ds(pl.multiple_of(g * tq, tq), tq), :]
            vt = vt_scr[:, pl.ds(pl.multiple_of(g * tq, tq), tq)]
            for m in range(2):
                st = _dot_nt(kb, qm_scr[m, pl.ds(pl.multiple_of(i * tq, tq), tq), :])
                if bias is not None:
                    st = st + bias
                m_prev = m_scr[m]
                m_new = jnp.maximum(m_prev, jnp.max(st, axis=0, keepdims=True))
                p = jnp.exp2(st - m_new).astype(BF16)
                acc_scr[0, m] = (jnp.exp2(m_prev - m_new) * acc_scr[0, m]
                                 + jnp.dot(vt, p, preferred_element_type=F32))
                m_scr[m] = m_new

        def query_block(i, carry):
            m_scr[...] = jnp.full(m_scr.shape, MASK_VALUE, F32)
            acc_scr[0] = jnp.zeros(acc_scr.shape[1:], F32)
            update(i, i, bias_ref[0, 1])
            pl.when(i >= 1)(lambda: update(i, i - 1, bias_ref[0, 0]))
            lax.fori_loop(0, jnp.maximum(i - 1, 0), lambda g, c: (update(i, g, None), c)[1], 0)
            finalize(pl.ds(pl.multiple_of(i * tq, tq), tq), acc_scr[0, 0], acc_scr[0, 1])
            return carry

        lax.fori_loop(0, nq, query_block, 0)


def _attention(z, bias, lq1, lk1, lq2, lk2, sw, *, batch, seq, n_heads, head_dim, tq,
               q_col, k_col, v_col, lambda_init):
    n = z.shape[0]
    nq = seq // tq
    acc_rows = LANES + ONES_ROWS
    vec = pl.BlockSpec((1, head_dim), lambda b, h: (0, 0))
    blocks = 4 * seq * LANES * 2 + 2 * tq * tq * 4
    scratch = (acc_rows * seq * 2 + 2 * seq * LANES * 2 + 2 * 8 * seq * 4 + 2 * 8 * tq * 4
               + nq * 2 * acc_rows * tq * 4)
    return pl.pallas_call(
        functools.partial(_attn_kernel, tq=tq, head_dim=head_dim, lambda_init=lambda_init),
        out_shape=jax.ShapeDtypeStruct((n, n_heads * LANES), BF16),
        grid=(batch, n_heads),
        in_specs=[
            pl.BlockSpec((seq, LANES), lambda b, h: (b, q_col + h)),
            pl.BlockSpec((seq, LANES), lambda b, h: (b, k_col + h)),
            pl.BlockSpec((seq, LANES), lambda b, h: (b, v_col + h)),
            pl.BlockSpec((1, 2, tq, tq), lambda b, h: (h, 0, 0, 0)),
            vec, vec, vec, vec,
            pl.BlockSpec((1, LANES), lambda b, h: (0, 0)),
        ],
        out_specs=pl.BlockSpec((seq, LANES), lambda b, h: (b, h)),
        scratch_shapes=[pltpu.VMEM((acc_rows, seq), BF16),
                        pltpu.VMEM((2, seq, LANES), BF16),
                        pltpu.VMEM((2, 1, seq), F32),
                        pltpu.VMEM((2, tq, tq), F32),
                        pltpu.VMEM((2, tq, tq), F32),
                        pltpu.VMEM((2, 1, tq), F32),
                        pltpu.VMEM((nq, 2, acc_rows, tq), F32)],
        compiler_params=_params(("parallel", "parallel"), blocks, scratch + 8 * tq * tq * 4),
        name="attention",
    )(z, z, z, bias, lq1, lk1, lq2, lk2, sw)


def _merge_kernel(yg_ref, wglu_ref, bglu_ref, ya_ref, gs_ref, ga_ref, ps_ref, pa_ref, wo_ref,
                  x_hbm, o_ref, ys_ref, m_ref, x_sem):
    i = pl.program_id(0)
    j = pl.program_id(1)
    nblk = yg_ref.shape[0]
    per_chunk = PROJ_CHUNK // LANES
    tm = o_ref.shape[0]

    x_copy = pltpu.make_async_copy(x_hbm.at[pl.ds(pl.multiple_of(i * tm, tm), tm), :],
                                   o_ref, x_sem)

    @pl.when(j == 0)
    def _():
        x_copy.start()
        yg = jnp.concatenate([yg_ref[b] for b in range(nblk)], axis=1)
        for c in range(nblk // per_chunk):
            cols = slice(c * PROJ_CHUNK, (c + 1) * PROJ_CHUNK)
            g = jnp.dot(yg, wglu_ref[:, cols], preferred_element_type=F32) + bglu_ref[:, cols]
            ys_ref[:, cols] = (yg[:, cols].astype(F32) * jax.nn.sigmoid(g)).astype(BF16)

    ys = ys_ref[...]
    ya = ya_ref[...]
    for c in range(m_ref.shape[1] // PROJ_CHUNK):
        cols = slice(c * PROJ_CHUNK, (c + 1) * PROJ_CHUNK)
        m = (gs_ref[:, cols].astype(F32) * jnp.dot(ys, ps_ref[:, cols], preferred_element_type=F32)
             + ga_ref[:, cols].astype(F32) * jnp.dot(ya, pa_ref[:, cols], preferred_element_type=F32))
        m_ref[:, cols] = m.astype(BF16)
    pl.when(j == 0)(x_copy.wait)
    m = m_ref[...]
    cw = o_ref.shape[1] // DOWN_SPLIT
    for c in range(DOWN_SPLIT):
        cols = slice(c * cw, (c + 1) * cw)
        o_ref[:, cols] += jnp.dot(m, wo_ref[:, cols], preferred_element_type=F32)


def _merge(yg, wglu, bglu, ya, z, ps, pa, wo, x2, *, layer, gs_col, ga_col, tm, tj):
    n, d = x2.shape
    ws = yg.shape[0] * LANES
    wa = ya.shape[1]
    blocks = (tm * ws * 2 + ws * ws * 2 + ws * 4 + tm * wa * 2 + 2 * tm * tj * 2
              + ws * tj * 2 + wa * tj * 2 + tj * d * 2 + tm * d * 4)
    return pl.pallas_call(
        _merge_kernel,
        out_shape=jax.ShapeDtypeStruct((n, d), F32),
        grid=(n // tm, d // tj),
        in_specs=[
            pl.BlockSpec((ws // LANES, tm, LANES), lambda i, j: (0, i, 0)),
            pl.BlockSpec((None, ws, ws), lambda i, j: (layer, 0, 0)),
            pl.BlockSpec((1, ws), lambda i, j: (0, 0)),
            pl.BlockSpec((tm, wa), lambda i, j: (i, 0)),
            pl.BlockSpec((tm, tj), lambda i, j: (i, gs_col + j)),
            pl.BlockSpec((tm, tj), lambda i, j: (i, ga_col + j)),
            pl.BlockSpec((None, ws, tj), lambda i, j: (layer, 0, j)),
            pl.BlockSpec((None, wa, tj), lambda i, j: (layer, 0, j)),
            pl.BlockSpec((None, tj, d), lambda i, j: (layer, j, 0)),
            pl.BlockSpec(memory_space=pl.ANY),
        ],
        out_specs=pl.BlockSpec((tm, d), lambda i, j: (i, 0)),
        scratch_shapes=[pltpu.VMEM((tm, ws), BF16), pltpu.VMEM((tm, tj), BF16),
                        pltpu.SemaphoreType.DMA],
        compiler_params=_params(("parallel", "arbitrary"), blocks,
                                tm * ws * 2 + tm * tj * 2 + 8 * tm * PROJ_CHUNK * 4),
        name="merge",
    )(yg, wglu, bglu, ya, z, z, ps, pa, wo, x2)


def _ffn_kernel(x_hbm, nw_ref, wg_ref, wu_ref, wd_ref, o_ref, h_ref, a_ref, x_buf, x_sem):
    i = pl.program_id(0)
    j = pl.program_id(1)
    tm = o_ref.shape[0]

    def x_copy(tile):
        return pltpu.make_async_copy(x_hbm.at[pl.ds(pl.multiple_of(tile * tm, tm), tm), :],
                                     x_buf, x_sem)

    @pl.when(j == 0)
    def _():
        pl.when(i == 0)(lambda: x_copy(0).start())
        x_copy(i).wait()
        x = x_buf[...]
        h_ref[...] = _rms_norm(x, nw_ref[...], RMS_EPS).astype(BF16)
        o_ref[...] = x

    @pl.when(jnp.logical_and(j == 1, i + 1 < pl.num_programs(0)))
    def _():
        x_copy(i + 1).start()

    h = h_ref[...]
    for c in range(a_ref.shape[1] // PROJ_CHUNK):
        cols = slice(c * PROJ_CHUNK, (c + 1) * PROJ_CHUNK)
        g = jnp.dot(h, wg_ref[:, cols], preferred_element_type=F32)
        u = jnp.dot(h, wu_ref[:, cols], preferred_element_type=F32)
        a_ref[:, cols] = (jax.nn.silu(g) * u).astype(BF16)
    a = a_ref[...]
    cw = o_ref.shape[1] // DOWN_SPLIT
    for c in range(DOWN_SPLIT):
        cols = slice(c * cw, (c + 1) * cw)
        o_ref[:, cols] += jnp.dot(a, wd_ref[:, cols], preferred_element_type=F32)


def _ffn(x2, nw, wg, wu, wd, *, layer, tm, tf):
    n, d = x2.shape
    dff = wg.shape[2]
    assert dff // tf >= 2
    blocks = tm * d * 4 + d * 4 + 3 * d * tf * 2
    return pl.pallas_call(
        _ffn_kernel,
        out_shape=jax.ShapeDtypeStruct((n, d), F32),
        grid=(n // tm, dff // tf),
        in_specs=[
            pl.BlockSpec(memory_space=pl.ANY),
            pl.BlockSpec((1, d), lambda i, j: (0, 0)),
            pl.BlockSpec((None, d, tf), lambda i, j: (layer, 0, j)),
            pl.BlockSpec((None, d, tf), lambda i, j: (layer, 0, j)),
            pl.BlockSpec((None, tf, d), lambda i, j: (layer, j, 0)),
        ],
        out_specs=pl.BlockSpec((tm, d), lambda i, j: (i, 0)),
        scratch_shapes=[pltpu.VMEM((tm, d), BF16), pltpu.VMEM((tm, tf), BF16),
                        pltpu.VMEM((tm, d), F32), pltpu.SemaphoreType.DMA],
        compiler_params=_params(("arbitrary", "arbitrary"), blocks,
                                tm * d * 2 + tm * tf * 2 + tm * d * 4 + 6 * tm * PROJ_CHUNK * 4),
        name="ffn",
    )(x2, nw, wg, wu, wd)


def _largest_tile(total, target, quantum):
    t = min(total, target)
    while total % t or t % quantum:
        t -= quantum
    return t


def kernel(x, norm1_w, w_in, lam_re, lam_im, log_step, ssm_b_re, ssm_b_im, ssm_c_re, ssm_c_im, ssm_d, w_glu, b_glu, q_norm_w, k_norm_w, lambda_q1, lambda_k1, lambda_q2, lambda_k2, subln_w, w_proj_ssm, w_proj_attn, w_out, rel_bias, norm2_w, w_ffn_gate, w_ffn_up, w_ffn_down):
    batch, seq, d_model = x.shape
    depth = w_in.shape[0]
    n = batch * seq
    _, groups, state = lam_re.shape
    group = ssm_b_re.shape[-1]
    ssm_width = groups * group
    head_dim = q_norm_w.shape[-1]
    v_dim = subln_w.shape[-1]
    attn_width = w_proj_attn.shape[1]
    n_heads = attn_width // v_dim
    d_ff = w_ffn_gate.shape[-1]

    tn = ssm_width
    assert ssm_width == attn_width and d_model % tn == 0
    assert w_in.shape[2] == ssm_width + 3 * attn_width + 2 * d_model
    assert 2 * head_dim == LANES and v_dim == LANES and 2 * state == LANES
    assert SSM_T * group == V7X_MXU_DIM
    kchunks = seq // SSM_T
    nsteps = kchunks.bit_length() - 1
    assert seq % SSM_T == 0 and kchunks == 1 << nsteps and nsteps <= 8

    tm_proj = _largest_tile(n, 1024, 16 * SSM_T)
    tm_merge = _largest_tile(n, 1024, 16)
    tm_ffn = _largest_tile(n, 1024, 8)
    tq = _largest_tile(seq, 512, MAX_DISTANCE)
    assert tq % CHUNK == 0
    tj = _largest_tile(d_model, 512, PROJ_CHUNK)
    tf = _largest_tile(d_ff, 512, LANES)

    q_col = 0
    k_col = q_col + attn_width // LANES
    v_col = k_col + attn_width // LANES
    gs_col = 3 * attn_width // tj
    ga_col = gs_col + d_model // tj

    bias = _bias_tiles(rel_bias, tq=tq)
    x2 = x.reshape(n, d_model)
    nblk = ssm_width // LANES
    seqs_per_tile = math.gcd(batch, max(1, SSM_ROWS // kchunks))
    rt = kchunks * seqs_per_tile
    dup = lambda a: jnp.concatenate([a, a], axis=-1)
    per_row = lambda a: jnp.repeat(a, group, axis=0).reshape(nblk, LANES, LANES)
    per_lane = lambda a: a.reshape(nblk, 1, -1)
    as_rows = lambda a: a.reshape(nblk, LANES, LANES)

    w_in, w_glu, w_proj_ssm, w_proj_attn, w_out, w_ffn_gate, w_ffn_up, w_ffn_down = (
        w.astype(BF16) for w in (w_in, w_glu, w_proj_ssm, w_proj_attn, w_out,
                                 w_ffn_gate, w_ffn_up, w_ffn_down))

    for l in range(depth):
        lambda_init = 0.8 - 0.6 * math.exp(-0.3 * l)

        qw2 = jnp.tile(q_norm_w[l], LANES // head_dim)[None, :]
        kw2 = jnp.tile(k_norm_w[l], LANES // head_dim)[None, :]
        u3, z = _in_proj(x2, norm1_w[l][None, :], w_in, qw2, kw2,
                         layer=l, head_dim=head_dim, tm=tm_proj, tn=tn)

        lam_re2, lam_im2 = dup(lam_re[l]), dup(lam_im[l])
        step2 = jnp.broadcast_to(log_step[l][:, None], (groups, 2 * state))
        b_re_t = jnp.swapaxes(ssm_b_re[l], 1, 2)
        b_im_t = jnp.swapaxes(ssm_b_im[l], 1, 2)
        c_re, c_im = ssm_c_re[l], ssm_c_im[l]
        d_rows = jnp.broadcast_to(ssm_d[l][:, None], (ssm_width, LANES)).reshape(nblk, LANES, LANES)
        m_op, e_op, ft_op, p_op, q_op = _ssm_prep(
            (per_row(lam_re2), per_row(lam_im2), per_row(step2),
             as_rows(jnp.concatenate([b_re_t, b_im_t], axis=-1)),
             as_rows(jnp.concatenate([b_im_t, b_re_t], axis=-1)),
             as_rows(jnp.concatenate([c_re, -c_im], axis=-1)),
             as_rows(jnp.concatenate([-c_im, -c_re], axis=-1)), d_rows),
            (per_lane(lam_re2), per_lane(lam_im2), per_lane(step2)),
            state=state, group=group, nsteps=nsteps)
        yg = _ssm(u3, m_op, e_op, ft_op, p_op, q_op, kchunks=kchunks, nsteps=nsteps, rt=rt)

        ya = _attention(z, bias, lambda_q1[l][None, :], lambda_k1[l][None, :],
                        lambda_q2[l][None, :], lambda_k2[l][None, :], subln_w[l][None, :],
                        batch=batch, seq=seq, n_heads=n_heads, head_dim=head_dim, tq=tq,
                        q_col=q_col, k_col=k_col, v_col=v_col, lambda_init=lambda_init)

        x2 = _merge(yg, w_glu, b_glu[l][None, :], ya, z, w_proj_ssm, w_proj_attn, w_out, x2,
                    layer=l, gs_col=gs_col, ga_col=ga_col, tm=tm_merge, tj=tj)
        x2 = _ffn(x2, norm2_w[l][None, :], w_ffn_gate, w_ffn_up, w_ffn_down,
                  layer=l, tm=tm_ffn, tf=tf)

    return x2.reshape(batch, seq, d_model)
```

```python
import functools
import math

import numpy as np
import jax
import jax.numpy as jnp
from jax import lax
from jax.experimental import pallas as pl
from jax.experimental.pallas import tpu as pltpu

F32 = jnp.float32
BF16 = jnp.bfloat16

CHUNK = 64
MAX_DISTANCE = 128
RMS_EPS = 1e-6
SUBLN_EPS = 1e-5
LAM_RE_MAX = -1e-4
MASK_VALUE = -1e30
LOG2_E = math.log2(math.e)

LANES = 128
V7X_MXU_DIM = 256
V7X_VMEM_BYTES = 64 * 1024 * 1024

PROJ_CHUNK = V7X_MXU_DIM
DOWN_SPLIT = 4
SSM_T = 16
SSM_ROWS = 512


def _vmem_limit(block_bytes, extra_bytes):
    return int(min(2 * block_bytes + extra_bytes + (4 << 20), V7X_VMEM_BYTES - (6 << 20)))


def _params(semantics, block_bytes, extra_bytes):
    return pltpu.CompilerParams(dimension_semantics=semantics,
                                vmem_limit_bytes=_vmem_limit(block_bytes, extra_bytes))


def _rms_norm(x, w, eps):
    ms = jnp.mean(x * x, axis=-1, keepdims=True)
    return x * lax.rsqrt(ms + eps) * w


def _dot_nt(a, b, **kw):
    return lax.dot_general(a, b, (((1,), (1,)), ((), ())), preferred_element_type=F32, **kw)


def _in_proj_kernel(x_ref, nw_ref, w_ref, qw_ref, kw_ref, u_ref, o_ref, h_ref, us_ref, *, head_dim):
    j = pl.program_id(1)

    @pl.when(j == 0)
    def _():
        h_ref[...] = _rms_norm(x_ref[...], nw_ref[...], RMS_EPS).astype(BF16)

    tn = w_ref.shape[1]

    def project(epilogue):
        h = h_ref[...]
        for c in range(tn // PROJ_CHUNK):
            zc = jnp.dot(h, w_ref[:, c * PROJ_CHUNK:(c + 1) * PROJ_CHUNK], preferred_element_type=F32)
            for s in range(PROJ_CHUNK // LANES):
                epilogue(c * (PROJ_CHUNK // LANES) + s, zc[:, s * LANES:(s + 1) * LANES])

    def store_z(b, y):
        o_ref[:, b * LANES:(b + 1) * LANES] = y.astype(o_ref.dtype)

    def seg_norm(w_row, scale):
        lo = lax.broadcasted_iota(jnp.int32, (1, LANES), 1) < head_dim

        def epilogue(b, zb):
            sq = zb * zb
            s_lo = jnp.sum(jnp.where(lo, sq, 0.0), axis=-1, keepdims=True)
            s_hi = jnp.sum(jnp.where(lo, 0.0, sq), axis=-1, keepdims=True)
            ms = jnp.where(lo, s_lo, s_hi) * (1.0 / head_dim)
            store_z(b, zb * lax.rsqrt(ms + RMS_EPS) * w_row * scale)
        return epilogue

    def store_u(b, zb):
        us_ref[b] = zb
        chunks = zb.shape[0] // SSM_T
        for t in range(SSM_T):
            u_ref[b, :, t * LANES:(t + 1) * LANES] = (
                us_ref[b, pl.ds(t, chunks, stride=SSM_T), :].astype(u_ref.dtype))

    pl.when(j == 0)(lambda: project(store_u))
    pl.when(j == 1)(lambda: project(seg_norm(qw_ref[...], head_dim ** -0.5 * LOG2_E)))
    pl.when(j == 2)(lambda: project(seg_norm(kw_ref[...], 1.0)))
    pl.when(j == 3)(lambda: project(store_z))
    pl.when(j >= 4)(lambda: project(lambda b, zb: store_z(b, jax.nn.sigmoid(zb))))


def _in_proj(x2, nw, w, qw2, kw2, *, layer, head_dim, tm, tn):
    n, d = x2.shape
    width = w.shape[2]
    blocks = tm * d * 4 + d * 4 + d * tn * 2 + 2 * tm * tn * 2
    return pl.pallas_call(
        functools.partial(_in_proj_kernel, head_dim=head_dim),
        out_shape=(jax.ShapeDtypeStruct((tn // LANES, n // SSM_T, SSM_T * LANES), BF16),
                   jax.ShapeDtypeStruct((n, width - tn), BF16)),
        grid=(n // tm, width // tn),
        in_specs=[
            pl.BlockSpec((tm, d), lambda i, j: (i, 0)),
            pl.BlockSpec((1, d), lambda i, j: (0, 0)),
            pl.BlockSpec((None, d, tn), lambda i, j: (layer, 0, j)),
            pl.BlockSpec((1, LANES), lambda i, j: (0, 0)),
            pl.BlockSpec((1, LANES), lambda i, j: (0, 0)),
        ],
        out_specs=(pl.BlockSpec((tn // LANES, tm // SSM_T, SSM_T * LANES), lambda i, j: (0, i, 0)),
                   pl.BlockSpec((tm, tn), lambda i, j: (i, jnp.maximum(j - 1, 0)))),
        scratch_shapes=[pltpu.VMEM((tm, d), BF16), pltpu.VMEM((tn // LANES, tm, LANES), F32)],
        compiler_params=_params(("parallel", "arbitrary"), blocks,
                                tm * d * 2 + tm * tn * 4 + 6 * tm * PROJ_CHUNK * 4),
        name="in_proj",
    )(x2, nw, w, qw2, kw2)


def _ssm_prep_kernel(lr_ref, li_ref, ls_ref, bcat_ref, bswap_ref, ccat_ref, cswap_ref, d_ref,
                     lrl_ref, lil_ref, lsl_ref, m_ref, e_ref, ft_ref, p_ref, q_ref, ft0_scr,
                     *, state, group, nsteps):
    gl = LANES // group
    lane = lax.broadcasted_iota(jnp.int32, (1, LANES), 1)
    sign = jnp.where(lane < state, -1.0, 1.0).astype(F32)
    lr = jnp.minimum(lr_ref[0], LAM_RE_MAX)
    li = li_ref[0]
    dt = jnp.exp(ls_ref[0])
    mag = jnp.exp(lr * dt)
    ab_r = mag * jnp.cos(li * dt)
    ab_i = mag * jnp.sin(li * dt)
    den = lr * lr + li * li
    nr = ab_r - 1.0
    fr = (nr * lr + ab_i * li) / den
    fi = (ab_i * lr - nr * li) / den
    bcat = bcat_ref[0]
    bswap = bswap_ref[0]
    bb_cat = fr * bcat + sign * fi * bswap
    bb_swap = fr * bswap - sign * fi * bcat
    ccat = ccat_ref[0]
    cswap = cswap_ref[0]

    row_group = lax.broadcasted_iota(jnp.int32, (LANES, 1), 0) // group

    def store_block_diag(ref, block, value):
        for g in range(gl):
            ref[0, block * LANES:(block + 1) * LANES, g * LANES:(g + 1) * LANES] = (
                jnp.where(row_group == g, value, 0.0).astype(ref.dtype))

    pw_r = jnp.ones((LANES, LANES), F32)
    pw_i = jnp.zeros((LANES, LANES), F32)
    for n in range(SSM_T + 1):
        ft_blk = pw_r * ccat + pw_i * cswap
        if n < SSM_T:
            ft0_scr[n * LANES:(n + 1) * LANES, :] = ft_blk
            store_block_diag(e_ref, SSM_T - 1 - n, pw_r * bb_cat + sign * pw_i * bb_swap)
        if n >= 1:
            store_block_diag(ft_ref, n - 1, ft_blk)
        pw_r, pw_i = pw_r * ab_r - pw_i * ab_i, pw_r * ab_i + pw_i * ab_r

    width = SSM_T * LANES
    r = _dot_nt(bb_cat, ft0_scr[...], precision=lax.Precision.HIGHEST)
    col = lax.broadcasted_iota(jnp.int32, (LANES, width), 1)
    row = lax.broadcasted_iota(jnp.int32, (LANES, width), 0)
    r = jnp.where(row // group == (col % LANES) // group, r, 0.0)
    r = r + jnp.where(col == row, jnp.tile(d_ref[0], (1, SSM_T)), 0.0)
    for j in range(SSM_T):
        rows_j = slice(j * LANES, (j + 1) * LANES)
        if j:
            m_ref[0, rows_j, 0:j * LANES] = jnp.zeros((LANES, j * LANES), m_ref.dtype)
        m_ref[0, rows_j, j * LANES:width] = r[:, 0:width - j * LANES].astype(m_ref.dtype)

    lrl = jnp.minimum(lrl_ref[0], LAM_RE_MAX)
    dtl = jnp.exp(lsl_ref[0])
    lane_l = lax.broadcasted_iota(jnp.int32, lrl.shape, 1)
    sign_l = jnp.where(lane_l % LANES < state, -1.0, 1.0).astype(F32)
    steps = lax.broadcasted_iota(jnp.int32, (8, 1), 0)
    nd = (SSM_T * jnp.left_shift(1, jnp.minimum(steps, nsteps))).astype(F32)
    mag_l = jnp.exp(lrl * dtl * nd)
    ang_l = lil_ref[0] * dtl * nd
    p_ref[0] = mag_l * jnp.cos(ang_l)
    q_ref[0] = sign_l * mag_l * jnp.sin(ang_l)


def _ssm_prep(rows_in, lanes_in, *, state, group, nsteps):
    nb = rows_in[0].shape[0]
    gl = LANES // group
    width = SSM_T * LANES
    bspec = lambda *shape: pl.BlockSpec((1,) + shape, lambda i: (i, 0, 0))
    blocks = width * width * 2 + 2 * width * gl * LANES * 2 + 8 * LANES * LANES * 4
    return pl.pallas_call(
        functools.partial(_ssm_prep_kernel, state=state, group=group, nsteps=nsteps),
        out_shape=(
            jax.ShapeDtypeStruct((nb, width, width), BF16),
            jax.ShapeDtypeStruct((nb, width, gl * LANES), BF16),
            jax.ShapeDtypeStruct((nb, width, gl * LANES), BF16),
            jax.ShapeDtypeStruct((nb, 8, gl * LANES), F32),
            jax.ShapeDtypeStruct((nb, 8, gl * LANES), F32),
        ),
        grid=(nb,),
        in_specs=[bspec(LANES, LANES)] * 8 + [bspec(1, gl * LANES)] * 3,
        out_specs=(bspec(width, width), bspec(width, gl * LANES), bspec(width, gl * LANES),
                   bspec(8, gl * LANES), bspec(8, gl * LANES)),
        scratch_shapes=[pltpu.VMEM((width, LANES), F32)],
        compiler_params=_params(("parallel",), blocks, width * LANES * 4 + 6 * LANES * width * 4),
        name="ssm_prep",
    )(*rows_in, *lanes_in)


SSM_COL_SPLIT = 4


def _ssm_kernel(u_ref, m_ref, e_ref, ft_ref, p_ref, q_ref, o_ref, ys_ref, ym_ref,
                *, kchunks, nsteps, state):
    u = u_ref[0]
    rows, width = u.shape
    cw = width // SSM_COL_SPLIT
    s = jnp.dot(u, e_ref[0], preferred_element_type=F32)
    for c in range(SSM_COL_SPLIT):
        kk = (c + 1) * cw
        ym_ref[:, c * cw:kk] = jnp.dot(u[:, 0:kk], m_ref[0, 0:kk, c * cw:kk],
                                       preferred_element_type=F32)
    kidx = lax.broadcasted_iota(jnp.int32, (rows, 1), 0) & (kchunks - 1)
    s_prev = []
    for g in range(s.shape[1] // LANES):
        lanes = slice(g * LANES, (g + 1) * LANES)
        sg = s[:, lanes]
        for i in range(nsteps):
            d = 1 << i
            x = jnp.where(kidx >= d, pltpu.roll(sg, d, axis=0), 0.0)
            sg = sg + p_ref[0, i:i + 1, lanes] * x + q_ref[0, i:i + 1, lanes] * pltpu.roll(x, state, axis=1)
        s_prev.append(jnp.where(kidx >= 1, pltpu.roll(sg, 1, axis=0), 0.0).astype(BF16))
    s_prev = jnp.concatenate(s_prev, axis=1)
    for c in range(SSM_COL_SPLIT):
        cols = slice(c * cw, (c + 1) * cw)
        y = jax.nn.gelu(ym_ref[:, cols] + _dot_nt(s_prev, ft_ref[0, cols, :]))
        for tl in range(cw // LANES):
            t = c * (cw // LANES) + tl
            ys_ref[pl.ds(t, rows, stride=SSM_T), :] = y[:, tl * LANES:(tl + 1) * LANES]
    o_ref[0] = ys_ref[...].astype(o_ref.dtype)


def _ssm(u4, m, e, ft, p, q, *, kchunks, nsteps, rt):
    nb, rows, width = u4.shape
    sl = e.shape[2]
    blocks = 2 * rt * width * 2 + width * width * 2 + 2 * width * sl * 2 + 2 * 8 * sl * 4
    return pl.pallas_call(
        functools.partial(_ssm_kernel, kchunks=kchunks, nsteps=nsteps, state=LANES // 2),
        out_shape=jax.ShapeDtypeStruct((nb, rows * SSM_T, LANES), BF16),
        grid=(nb, rows // rt),
        in_specs=[pl.BlockSpec((1, rt, width), lambda b, r: (b, r, 0)),
                  pl.BlockSpec((1, width, width), lambda b, r: (b, 0, 0)),
                  pl.BlockSpec((1, width, sl), lambda b, r: (b, 0, 0)),
                  pl.BlockSpec((1, width, sl), lambda b, r: (b, 0, 0)),
                  pl.BlockSpec((1, 8, sl), lambda b, r: (b, 0, 0)),
                  pl.BlockSpec((1, 8, sl), lambda b, r: (b, 0, 0))],
        out_specs=pl.BlockSpec((1, rt * SSM_T, LANES), lambda b, r: (b, r, 0)),
        scratch_shapes=[pltpu.VMEM((rt * SSM_T, LANES), F32), pltpu.VMEM((rt, width), F32)],
        compiler_params=_params(("parallel", "arbitrary"), blocks,
                                2 * rt * width * 4 + 4 * rt * sl * 4
                                + 4 * rt * (width // SSM_COL_SPLIT) * 4),
        name="ssm",
    )(u4, m, e, ft, p, q)


def _bias_kernel(tab_ref, idx_ref, o_ref, *, n_buckets, far_bucket):
    h = pl.program_id(0)
    far = tab_ref[far_bucket, h]
    blocks = []
    for t in range(2):
        idx = idx_ref[t]
        acc = jnp.full(idx.shape, MASK_VALUE, F32)
        for b in range(n_buckets):
            acc = jnp.where(idx == b, (tab_ref[b, h] - far) * LOG2_E, acc)
        blocks.append(acc)
    prev_blk, near_blk = blocks
    zero_blk = jnp.zeros_like(near_blk)
    mask_blk = jnp.full_like(near_blk, MASK_VALUE)
    nb = o_ref.shape[2] // LANES
    for r in range(nb):
        for c in range(nb):
            where = (slice(r * LANES, (r + 1) * LANES), slice(c * LANES, (c + 1) * LANES))
            o_ref[(0, 0) + where] = prev_blk if (r, c) == (nb - 1, 0) else zero_blk
            o_ref[(0, 1) + where] = (near_blk if r == c else prev_blk if r == c - 1
                                     else mask_blk if r > c else zero_blk)


def _t5_bucket_np(rel, n_buckets):
    nb = n_buckets // 2
    ret = np.where(rel > 0, nb, 0)
    n = np.abs(rel)
    max_exact = nb // 2
    nf = np.maximum(n, 1).astype(np.float32)
    large = max_exact + (np.log(nf / np.float32(max_exact)) / np.float32(math.log(MAX_DISTANCE / max_exact))
                         * np.float32(nb - max_exact)).astype(np.int32)
    large = np.minimum(large, nb - 1)
    return (ret + np.where(n < max_exact, n, large)).astype(np.int32)


def _bias_tiles(rel_bias, *, tq):
    n_buckets, n_heads = rel_bias.shape
    c = np.arange(LANES)[:, None]
    r = np.arange(LANES)[None, :]
    prev = _t5_bucket_np(c - r - LANES, n_buckets)
    near = np.where(c // CHUNK <= r // CHUNK, _t5_bucket_np(c - r, n_buckets), -1)
    idx = np.stack([prev, near])
    far_bucket = int(_t5_bucket_np(np.array(-MAX_DISTANCE), n_buckets))
    assert LANES >= MAX_DISTANCE and LANES % CHUNK == 0 and tq % LANES == 0
    return pl.pallas_call(
        functools.partial(_bias_kernel, n_buckets=n_buckets, far_bucket=far_bucket),
        out_shape=jax.ShapeDtypeStruct((n_heads, 2, tq, tq), F32),
        grid=(n_heads,),
        in_specs=[pl.BlockSpec(memory_space=pltpu.SMEM),
                  pl.BlockSpec((2, LANES, LANES), lambda h: (0, 0, 0))],
        out_specs=pl.BlockSpec((1, 2, tq, tq), lambda h: (h, 0, 0, 0)),
        compiler_params=pltpu.CompilerParams(dimension_semantics=("parallel",)),
        name="attn_bias",
    )(rel_bias, jnp.asarray(idx))


ONES_ROWS = 16
FIXED_SHIFT_MAX = 60.0
BOUND_SLACK = 1.02


def _attn_kernel(q_ref, k_ref, v_ref, bias_ref, lq1_ref, lk1_ref, lq2_ref, lk2_ref, sw_ref,
                 o_ref, vt_scr, qm_scr, shift_scr, st_a, st_b, m_scr, acc_scr, *, tq, head_dim, lambda_init):
    seq, vd = v_ref.shape
    nq = seq // tq

    def block(j):
        return slice(j * tq, (j + 1) * tq)

    lo = lax.broadcasted_iota(jnp.int32, (1, LANES), 1) < head_dim
    halves = (lax.broadcasted_iota(jnp.int32, (8, LANES), 0)
              == (lax.broadcasted_iota(jnp.int32, (8, LANES), 1) >= head_dim).astype(jnp.int32)
              ).astype(BF16)
    qsq, ksq = [], []
    for c in range(nq):
        vt_scr[0:vd, block(c)] = v_ref[block(c), :].T
        q = q_ref[block(c), :]
        zero = jnp.zeros_like(q)
        qm_scr[0, block(c), :] = jnp.where(lo, q, zero)
        qm_scr[1, block(c), :] = jnp.where(lo, zero, q)
        k = k_ref[block(c), :]
        qsq.append(_dot_nt(halves, q * q))
        ksq.append(_dot_nt(halves, k * k))
    vt_scr[vd:, :] = jnp.ones((ONES_ROWS, seq), BF16)

    k_max = jnp.max(jnp.concatenate(ksq, axis=1), axis=1, keepdims=True)
    bias_max = jnp.maximum(jnp.maximum(jnp.max(bias_ref[0, 1, 0:LANES, 0:LANES]),
                                       jnp.max(bias_ref[0, 0, tq - LANES:tq, 0:LANES])), 0.0)
    bound = jnp.sqrt(jnp.concatenate(qsq, axis=1) * k_max) * BOUND_SLACK + bias_max
    shift_scr[0] = bound[0:1]
    shift_scr[1] = bound[1:2]
    fixed_shift_ok = jnp.max(bound[0:2]) <= FIXED_SHIFT_MAX

    s1 = jnp.sum(lq1_ref[...] * lk1_ref[...], axis=-1, keepdims=True)
    s2 = jnp.sum(lq2_ref[...] * lk2_ref[...], axis=-1, keepdims=True)
    lam = jnp.exp(s1) - jnp.exp(s2) + lambda_init

    def finalize(rows, a1, a2):
        ot = a1[0:vd] * (1.0 / a1[vd:vd + 1]) - lam * (a2[0:vd] * (1.0 / a2[vd:vd + 1]))
        y = _rms_norm(ot.T, sw_ref[...], SUBLN_EPS) * (1.0 - lambda_init)
        o_ref[rows, :] = y.astype(o_ref.dtype)


    @pl.when(fixed_shift_ok)
    def _():
        def scores(item, st_ref):
            i, g, k0, nk, q0, nqr = item
            kb = k_ref[g * tq + k0:g * tq + k0 + nk, :]
            for m in range(2):
                st_ref[m, 0:nk, 0:nqr] = _dot_nt(kb, qm_scr[m, i * tq + q0:i * tq + q0 + nqr, :])

        def accumulate(item, st_ref):
            i, g, k0, nk, q0, nqr = item
            vt = vt_scr[:, g * tq + k0:g * tq + k0 + nk]
            for m in range(2):
                st = st_ref[m, 0:nk, 0:nqr]
                if g >= i - 1:
                    st = st + bias_ref[0, 1 if g == i else 0, k0:k0 + nk, q0:q0 + nqr]
                p = jnp.exp2(st - shift_scr[m, :, i * tq + q0:i * tq + q0 + nqr]).astype(BF16)
                pv = jnp.dot(vt, p, preferred_element_type=F32)
                if g == 0 and k0 == 0:
                    acc_scr[i, m] = pv
                else:
                    acc_scr[i, m, :, q0:q0 + nqr] += pv

        half = tq // 2
        assert half % CHUNK == 0
        work = []
        for i in range(nq):
            work += [(i, g, 0, tq, 0, tq) for g in range(i)]
            work += [(i, i, 0, half, 0, tq), (i, i, half, half, half, half)]
        bufs = (st_a, st_b)
        scores(work[0], bufs[0])
        for t, item in enumerate(work):
            if t + 1 < len(work):
                scores(work[t + 1], bufs[(t + 1) % 2])
            accumulate(item, bufs[t % 2])
            if item[1] == item[0] and item[2] > 0:
                finalize(block(item[0]), acc_scr[item[0], 0], acc_scr[item[0], 1])

    @pl.when(jnp.logical_not(fixed_shift_ok))
    def _():
        def update(i, g, bias):
            kb = k_ref[pl.ds(pl.multiple_of(g * tq, tq), tq), :]
            vt = vt_scr[:, pl.ds(pl.multiple_of(g * tq, tq), tq)]
            for m in range(2):
                st = _dot_nt(kb, qm_scr[m, pl.ds(pl.multiple_of(i * tq, tq), tq), :])
                if bias is not None:
                    st = st + bias
                m_prev = m_scr[m]
                m_new = jnp.maximum(m_prev, jnp.max(st, axis=0, keepdims=True))
                p = jnp.exp2(st - m_new).astype(BF16)
                acc_scr[0, m] = (jnp.exp2(m_prev - m_new) * acc_scr[0, m]
                                 + jnp.dot(vt, p, preferred_element_type=F32))
                m_scr[m] = m_new

        def query_block(i, carry):
            m_scr[...] = jnp.full(m_scr.shape, MASK_VALUE, F32)
            acc_scr[0] = jnp.zeros(acc_scr.shape[1:], F32)
            update(i, i, bias_ref[0, 1])
            pl.when(i >= 1)(lambda: update(i, i - 1, bias_ref[0, 0]))
            lax.fori_loop(0, jnp.maximum(i - 1, 0), lambda g, c: (update(i, g, None), c)[1], 0)
            finalize(pl.ds(pl.multiple_of(i * tq, tq), tq), acc_scr[0, 0], acc_scr[0, 1])
            return carry

        lax.fori_loop(0, nq, query_block, 0)


def _attention(z, bias, lq1, lk1, lq2, lk2, sw, *, batch, seq, n_heads, head_dim, tq,
               q_col, k_col, v_col, lambda_init):
    n = z.shape[0]
    nq = seq // tq
    acc_rows = LANES + ONES_ROWS
    vec = pl.BlockSpec((1, head_dim), lambda b, h: (0, 0))
    blocks = 4 * seq * LANES * 2 + 2 * tq * tq * 4
    scratch = (acc_rows * seq * 2 + 2 * seq * LANES * 2 + 2 * 8 * seq * 4 + 2 * 8 * tq * 4
               + nq * 2 * acc_rows * tq * 4)
    return pl.pallas_call(
        functools.partial(_attn_kernel, tq=tq, head_dim=head_dim, lambda_init=lambda_init),
        out_shape=jax.ShapeDtypeStruct((n, n_heads * LANES), BF16),
        grid=(batch, n_heads),
        in_specs=[
            pl.BlockSpec((seq, LANES), lambda b, h: (b, q_col + h)),
            pl.BlockSpec((seq, LANES), lambda b, h: (b, k_col + h)),
            pl.BlockSpec((seq, LANES), lambda b, h: (b, v_col + h)),
            pl.BlockSpec((1, 2, tq, tq), lambda b, h: (h, 0, 0, 0)),
            vec, vec, vec, vec,
            pl.BlockSpec((1, LANES), lambda b, h: (0, 0)),
        ],
        out_specs=pl.BlockSpec((seq, LANES), lambda b, h: (b, h)),
        scratch_shapes=[pltpu.VMEM((acc_rows, seq), BF16),
                        pltpu.VMEM((2, seq, LANES), BF16),
                        pltpu.VMEM((2, 1, seq), F32),
                        pltpu.VMEM((2, tq, tq), F32),
                        pltpu.VMEM((2, tq, tq), F32),
                        pltpu.VMEM((2, 1, tq), F32),
                        pltpu.VMEM((nq, 2, acc_rows, tq), F32)],
        compiler_params=_params(("parallel", "parallel"), blocks, scratch + 8 * tq * tq * 4),
        name="attention",
    )(z, z, z, bias, lq1, lk1, lq2, lk2, sw)


def _merge_kernel(yg_ref, wglu_ref, bglu_ref, ya_ref, *refs, tj):
    *gate_refs, ps_ref, pa_ref, wo_ref, x_hbm, o_ref, ys_ref, m_ref, x_sem = refs
    i = pl.program_id(0)
    nblk = yg_ref.shape[0]
    per_chunk = PROJ_CHUNK // LANES
    tm, d = o_ref.shape
    gs_refs, ga_refs = gate_refs[:d // tj], gate_refs[d // tj:]

    x_copy = pltpu.make_async_copy(x_hbm.at[pl.ds(pl.multiple_of(i * tm, tm), tm), :],
                                   o_ref, x_sem)
    x_copy.start()
    yg = jnp.concatenate([yg_ref[b] for b in range(nblk)], axis=1)
    for c in range(nblk // per_chunk):
        cols = slice(c * PROJ_CHUNK, (c + 1) * PROJ_CHUNK)
        g = jnp.dot(yg, wglu_ref[:, cols], preferred_element_type=F32) + bglu_ref[:, cols]
        ys_ref[:, cols] = (yg[:, cols].astype(F32) * jax.nn.sigmoid(g)).astype(BF16)

    ys = ys_ref[...]
    ya = ya_ref[...]
    for c in range(d // PROJ_CHUNK):
        cols = slice(c * PROJ_CHUNK, (c + 1) * PROJ_CHUNK)
        t, off = divmod(c * PROJ_CHUNK, tj)
        gcols = slice(off, off + PROJ_CHUNK)
        m = (gs_refs[t][:, gcols].astype(F32) * jnp.dot(ys, ps_ref[:, cols], preferred_element_type=F32)
             + ga_refs[t][:, gcols].astype(F32) * jnp.dot(ya, pa_ref[:, cols], preferred_element_type=F32))
        m_ref[:, cols] = m.astype(BF16)
    x_copy.wait()
    m = m_ref[...]
    cw = d // DOWN_SPLIT
    for c in range(DOWN_SPLIT):
        cols = slice(c * cw, (c + 1) * cw)
        o_ref[:, cols] += jnp.dot(m, wo_ref[:, cols], preferred_element_type=F32)


def _merge(yg, wglu, bglu, ya, z, ps, pa, wo, x2, *, layer, gs_col, ga_col, tm, tj):
    n, d = x2.shape
    ws = yg.shape[0] * LANES
    wa = ya.shape[1]
    nt = d // tj
    resident = pl.Buffered(1)
    weights = ws * ws * 2 + ws * d * 2 + wa * d * 2 + d * d * 2
    blocks = tm * ws * 2 + tm * wa * 2 + 2 * tm * d * 2 + tm * d * 4 + weights // 2
    gate_specs = [pl.BlockSpec((tm, tj), functools.partial(lambda i, col: (i, col), col=col0 + t))
                  for col0 in (gs_col, ga_col) for t in range(nt)]
    return pl.pallas_call(
        functools.partial(_merge_kernel, tj=tj),
        out_shape=jax.ShapeDtypeStruct((n, d), F32),
        grid=(n // tm,),
        in_specs=[
            pl.BlockSpec((ws // LANES, tm, LANES), lambda i: (0, i, 0)),
            pl.BlockSpec((None, ws, ws), lambda i: (layer, 0, 0), pipeline_mode=resident),
            pl.BlockSpec((1, ws), lambda i: (0, 0)),
            pl.BlockSpec((tm, wa), lambda i: (i, 0)),
            *gate_specs,
            pl.BlockSpec((None, ws, d), lambda i: (layer, 0, 0), pipeline_mode=resident),
            pl.BlockSpec((None, wa, d), lambda i: (layer, 0, 0), pipeline_mode=resident),
            pl.BlockSpec((None, d, d), lambda i: (layer, 0, 0), pipeline_mode=resident),
            pl.BlockSpec(memory_space=pl.ANY),
        ],
        out_specs=pl.BlockSpec((tm, d), lambda i: (i, 0)),
        scratch_shapes=[pltpu.VMEM((tm, ws), BF16), pltpu.VMEM((tm, d), BF16),
                        pltpu.SemaphoreType.DMA],
        compiler_params=_params(("parallel",), blocks,
                                tm * ws * 2 + tm * d * 2 + 8 * tm * PROJ_CHUNK * 4),
        name="merge",
    )(yg, wglu, bglu, ya, *([z] * (2 * nt)), ps, pa, wo, x2)


def _ffn_kernel(x_hbm, nw_ref, wg_ref, wu_ref, wd_ref, o_ref, h_ref, a_ref, x_buf, x_sem):
    i = pl.program_id(0)
    j = pl.program_id(1)
    tm = o_ref.shape[0]

    def x_copy(tile):
        return pltpu.make_async_copy(x_hbm.at[pl.ds(pl.multiple_of(tile * tm, tm), tm), :],
                                     x_buf, x_sem)

    @pl.when(j == 0)
    def _():
        pl.when(i == 0)(lambda: x_copy(0).start())
        x_copy(i).wait()
        x = x_buf[...]
        h_ref[...] = _rms_norm(x, nw_ref[...], RMS_EPS).astype(BF16)
        o_ref[...] = x

    @pl.when(jnp.logical_and(j == 1, i + 1 < pl.num_programs(0)))
    def _():
        x_copy(i + 1).start()

    h = h_ref[...]
    for c in range(a_ref.shape[1] // PROJ_CHUNK):
        cols = slice(c * PROJ_CHUNK, (c + 1) * PROJ_CHUNK)
        g = jnp.dot(h, wg_ref[:, cols], preferred_element_type=F32)
        u = jnp.dot(h, wu_ref[:, cols], preferred_element_type=F32)
        a_ref[:, cols] = (jax.nn.silu(g) * u).astype(BF16)
    a = a_ref[...]
    cw = o_ref.shape[1] // DOWN_SPLIT
    for c in range(DOWN_SPLIT):
        cols = slice(c * cw, (c + 1) * cw)
        o_ref[:, cols] += jnp.dot(a, wd_ref[:, cols], preferred_element_type=F32)


def _ffn(x2, nw, wg, wu, wd, *, layer, tm, tf):
    n, d = x2.shape
    dff = wg.shape[2]
    assert dff // tf >= 2
    blocks = tm * d * 4 + d * 4 + 3 * d * tf * 2
    return pl.pallas_call(
        _ffn_kernel,
        out_shape=jax.ShapeDtypeStruct((n, d), F32),
        grid=(n // tm, dff // tf),
        in_specs=[
            pl.BlockSpec(memory_space=pl.ANY),
            pl.BlockSpec((1, d), lambda i, j: (0, 0)),
            pl.BlockSpec((None, d, tf), lambda i, j: (layer, 0, j)),
            pl.BlockSpec((None, d, tf), lambda i, j: (layer, 0, j)),
            pl.BlockSpec((None, tf, d), lambda i, j: (layer, j, 0)),
        ],
        out_specs=pl.BlockSpec((tm, d), lambda i, j: (i, 0)),
        scratch_shapes=[pltpu.VMEM((tm, d), BF16), pltpu.VMEM((tm, tf), BF16),
                        pltpu.VMEM((tm, d), F32), pltpu.SemaphoreType.DMA],
        compiler_params=_params(("arbitrary", "arbitrary"), blocks,
                                tm * d * 2 + tm * tf * 2 + tm * d * 4 + 6 * tm * PROJ_CHUNK * 4),
        name="ffn",
    )(x2, nw, wg, wu, wd)


def _largest_tile(total, target, quantum):
    t = min(total, target)
    while total % t or t % quantum:
        t -= quantum
    return t


def kernel(x, norm1_w, w_in, lam_re, lam_im, log_step, ssm_b_re, ssm_b_im, ssm_c_re, ssm_c_im, ssm_d, w_glu, b_glu, q_norm_w, k_norm_w, lambda_q1, lambda_k1, lambda_q2, lambda_k2, subln_w, w_proj_ssm, w_proj_attn, w_out, rel_bias, norm2_w, w_ffn_gate, w_ffn_up, w_ffn_down):
    batch, seq, d_model = x.shape
    depth = w_in.shape[0]
    n = batch * seq
    _, groups, state = lam_re.shape
    group = ssm_b_re.shape[-1]
    ssm_width = groups * group
    head_dim = q_norm_w.shape[-1]
    v_dim = subln_w.shape[-1]
    attn_width = w_proj_attn.shape[1]
    n_heads = attn_width // v_dim
    d_ff = w_ffn_gate.shape[-1]

    tn = ssm_width
    assert ssm_width == attn_width and d_model % tn == 0
    assert w_in.shape[2] == ssm_width + 3 * attn_width + 2 * d_model
    assert 2 * head_dim == LANES and v_dim == LANES and 2 * state == LANES
    assert SSM_T * group == V7X_MXU_DIM
    kchunks = seq // SSM_T
    nsteps = kchunks.bit_length() - 1
    assert seq % SSM_T == 0 and kchunks == 1 << nsteps and nsteps <= 8

    tm_proj = _largest_tile(n, 1024, 16 * SSM_T)
    tm_merge = _largest_tile(n, 512, 16)
    tm_ffn = _largest_tile(n, 1024, 8)
    tq = _largest_tile(seq, 512, MAX_DISTANCE)
    assert tq % CHUNK == 0
    tj = _largest_tile(d_model, 1024, PROJ_CHUNK)
    tf = _largest_tile(d_ff, 512, LANES)

    q_col = 0
    k_col = q_col + attn_width // LANES
    v_col = k_col + attn_width // LANES
    gs_col = 3 * attn_width // tj
    ga_col = gs_col + d_model // tj

    bias = _bias_tiles(rel_bias, tq=tq)
    x2 = x.reshape(n, d_model)
    nblk = ssm_width // LANES
    seqs_per_tile = math.gcd(batch, max(1, SSM_ROWS // kchunks))
    rt = kchunks * seqs_per_tile
    dup = lambda a: jnp.concatenate([a, a], axis=-1)
    per_row = lambda a: jnp.repeat(a, group, axis=0).reshape(nblk, LANES, LANES)
    per_lane = lambda a: a.reshape(nblk, 1, -1)
    as_rows = lambda a: a.reshape(nblk, LANES, LANES)

    w_in, w_glu, w_proj_ssm, w_proj_attn, w_out, w_ffn_gate, w_ffn_up, w_ffn_down = (
        w.astype(BF16) for w in (w_in, w_glu, w_proj_ssm, w_proj_attn, w_out,
                                 w_ffn_gate, w_ffn_up, w_ffn_down))

    for l in range(depth):
        lambda_init = 0.8 - 0.6 * math.exp(-0.3 * l)

        qw2 = jnp.tile(q_norm_w[l], LANES // head_dim)[None, :]
        kw2 = jnp.tile(k_norm_w[l], LANES // head_dim)[None, :]
        u3, z = _in_proj(x2, norm1_w[l][None, :], w_in, qw2, kw2,
                         layer=l, head_dim=head_dim, tm=tm_proj, tn=tn)

        lam_re2, lam_im2 = dup(lam_re[l]), dup(lam_im[l])
        step2 = jnp.broadcast_to(log_step[l][:, None], (groups, 2 * state))
        b_re_t = jnp.swapaxes(ssm_b_re[l], 1, 2)
        b_im_t = jnp.swapaxes(ssm_b_im[l], 1, 2)
        c_re, c_im = ssm_c_re[l], ssm_c_im[l]
        d_rows = jnp.broadcast_to(ssm_d[l][:, None], (ssm_width, LANES)).reshape(nblk, LANES, LANES)
        m_op, e_op, ft_op, p_op, q_op = _ssm_prep(
            (per_row(lam_re2), per_row(lam_im2), per_row(step2),
             as_rows(jnp.concatenate([b_re_t, b_im_t], axis=-1)),
             as_rows(jnp.concatenate([b_im_t, b_re_t], axis=-1)),
             as_rows(jnp.concatenate([c_re, -c_im], axis=-1)),
             as_rows(jnp.concatenate([-c_im, -c_re], axis=-1)), d_rows),
            (per_lane(lam_re2), per_lane(lam_im2), per_lane(step2)),
            state=state, group=group, nsteps=nsteps)
        yg = _ssm(u3, m_op, e_op, ft_op, p_op, q_op, kchunks=kchunks, nsteps=nsteps, rt=rt)

        ya = _attention(z, bias, lambda_q1[l][None, :], lambda_k1[l][None, :],
                        lambda_q2[l][None, :], lambda_k2[l][None, :], subln_w[l][None, :],
                        batch=batch, seq=seq, n_heads=n_heads, head_dim=head_dim, tq=tq,
                        q_col=q_col, k_col=k_col, v_col=v_col, lambda_init=lambda_init)

        x2 = _merge(yg, w_glu, b_glu[l][None, :], ya, z, w_proj_ssm, w_proj_attn, w_out, x2,
                    layer=l, gs_col=gs_col, ga_col=ga_col, tm=tm_merge, tj=tj)
        x2 = _ffn(x2, norm2_w[l][None, :], w_ffn_gate, w_ffn_up, w_ffn_down,
                  layer=l, tm=tm_ffn, tf=tf)

    return x2.reshape(batch, seq, d_model)
```
